```python
import jax
import jax.numpy as jnp
from jax import lax
import numpy as np

D_MODEL = 2048
BATCH = 8
SEQ = 2048
DEPTH = 2

CTX_LEN = 256
GRID_W = 64
W_A = 1024
HEAD_DIM = 64
N_HEADS = W_A // HEAD_DIM
W_B = 1024
POOL_WINDOWS = (2, 4, 8, 16)
N_POOL_GROUPS = len(POOL_WINDOWS)
POOL_GROUP_W = W_B // N_POOL_GROUPS
R_DECAY = 64
R_ICLR = 64
R_VRES = 32
NORM_EPS = 1e-6
GN_EPS = HEAD_DIM * 1e-5
OFF_RKV = 0
OFF_POOL = OFF_RKV + 3 * W_A
OFF_GATE_A = OFF_POOL + W_B
OFF_GATE_B = OFF_GATE_A + W_A
OFF_DECAY = OFF_GATE_B + W_B
OFF_ICLR = OFF_DECAY + 2 * R_DECAY
OFF_MERGE = OFF_ICLR + 2 * R_ICLR
IN_WIDTH = OFF_MERGE + 2 * D_MODEL

kernel_name = 'hybrid_bi_rwkv7_multipool_prefix_dit'


def rmsnorm(x, g):
    x32 = x.astype(jnp.float32)
    y = x32 * lax.rsqrt(jnp.mean(x32 * x32, axis=-1, keepdims=True) + NORM_EPS)
    return (y * g.astype(jnp.float32)).astype(x.dtype)


def grid_shift(u):
    b, t, m, ch = u.shape
    rows = t // GRID_W
    q = ch // 4
    g = u.reshape(b, rows, GRID_W, m, ch)
    left = jnp.pad(g[:, :, :-1, :, :q], ((0, 0), (0, 0), (1, 0), (0, 0), (0, 0)))
    right = jnp.pad(g[:, :, 1:, :, q:2 * q], ((0, 0), (0, 0), (0, 1), (0, 0), (0, 0)))
    up = jnp.pad(g[:, :-1, :, :, 2 * q:3 * q], ((0, 0), (1, 0), (0, 0), (0, 0), (0, 0)))
    down = jnp.pad(g[:, 1:, :, :, 3 * q:], ((0, 0), (0, 1), (0, 0), (0, 0), (0, 0)))
    return jnp.concatenate([left, right, up, down], axis=-1).reshape(b, t, m, ch)


def seq_shift(u):
    h = u.shape[-1] // 2
    prev = jnp.pad(u[:, :-1, :, :h], ((0, 0), (1, 0), (0, 0), (0, 0)))
    nxt = jnp.pad(u[:, 1:, :, h:], ((0, 0), (0, 1), (0, 0), (0, 0)))
    return jnp.concatenate([prev, nxt], axis=-1)


def multiscale_pool(p):
    t = p.shape[1]
    p32 = p.astype(jnp.float32)
    cs = jnp.pad(jnp.cumsum(p32, axis=1), ((0, 0), (1, 0), (0, 0)))
    pos = jnp.arange(t)
    outs = []
    for gi, win in enumerate(POOL_WINDOWS):
        left = win // 2
        right = win - 1 - left
        lo = jnp.clip(pos - left, 0, t)
        hi = jnp.clip(pos + right + 1, 0, t)
        sl = slice(gi * POOL_GROUP_W, (gi + 1) * POOL_GROUP_W)
        csg = cs[:, :, sl]
        mean = (csg[:, hi] - csg[:, lo]) / (hi - lo).astype(jnp.float32)[None, :, None]
        outs.append(mean - p32[:, :, sl])
    return jnp.stack(outs, axis=2)


def _heads(z):
    return z.reshape(z.shape[:-1] + (N_HEADS, HEAD_DIM))


def _dirs_shared(z):
    return jnp.stack([z, jnp.flip(z, 1)], axis=0)


def _dirs_split(z):
    return _heads(jnp.stack([z[:, :, 0], jnp.flip(z[:, :, 1], 1)], axis=0))


def wkv_scan(s0, w, k, v, kk, a, r):
    xs = (w, k, v, kk, a) + ((r,) if r is not None else ())
    xs = tuple(jnp.moveaxis(z, 2, 0) for z in xs)

    def step(S, inp):
        w_t, k_t, v_t, kk_t, a_t = inp[:5]
        sa = jnp.einsum('dbhvk,dbhk->dbhv', S, -kk_t)
        S = (S * w_t[..., None, :] + sa[..., :, None] * (kk_t * a_t)[..., None, :]
             + v_t[..., :, None] * k_t[..., None, :])
        y = jnp.einsum('dbhvk,dbhk->dbhv', S, inp[5]) if len(inp) > 5 else None
        return S, y

    s_fin, ys = lax.scan(step, s0, xs)
    return s_fin, (None if r is None else jnp.moveaxis(ys, 0, 2))


def stream_mixer(xn, u, p, v_first, shift_fn, s0, emit_out):
    f32 = jnp.float32
    b, t, _ = u.shape
    rkv = u[..., OFF_RKV:OFF_RKV + 3 * W_A].reshape(b, t, 3, W_A)
    rkv = rkv + p['tok_mu'] * (shift_fn(rkv) - rkv)
    r, k, v = (rkv[:, :, i].astype(f32) for i in range(3))
    if v_first is not None:
        v_gate = jax.nn.sigmoid((p['vres_v0'] + (xn @ p['vres_lora_a']) @ p['vres_lora_b']).astype(f32))
        v = v + (v_first - v) * v_gate
    d_lr = jnp.tanh(u[..., OFF_DECAY:OFF_DECAY + 2 * R_DECAY].reshape(b, t, 2, R_DECAY).astype(f32))
    w_log = -jax.nn.softplus(-(p['decay_w0'] + jnp.einsum('btdr,drc->btdc', d_lr, p['decay_lora_b']))) - 0.5
    decay = jnp.exp(-jnp.exp(w_log))
    a_lr = u[..., OFF_ICLR:OFF_ICLR + 2 * R_ICLR].reshape(b, t, 2, R_ICLR).astype(f32)
    a = jax.nn.sigmoid(p['iclr_a0'] + jnp.einsum('btdr,drc->btdc', a_lr, p['iclr_lora_b']))
    kk = _heads(k * p['key_k'])
    kk = kk / jnp.maximum(jnp.sqrt(jnp.sum(kk * kk, axis=-1, keepdims=True)), 1e-12)
    k_dir = k[:, :, None] * (1.0 + (a - 1.0) * p['key_a'])
    r_h, v_h = _heads(r), _heads(v)
    s_fin, y = wkv_scan(s0, _dirs_split(decay), _dirs_split(k_dir), _dirs_shared(v_h), _dirs_shared(kk),
                        _dirs_split(a), _dirs_shared(r_h) if emit_out else None)
    if not emit_out:
        return None, s_fin, v
    y = y[0] + jnp.flip(y[1], 1)
    mu = jnp.mean(y, axis=-1, keepdims=True)
    var = jnp.mean(jnp.square(y - mu), axis=-1, keepdims=True)
    y = ((y - mu) * lax.rsqrt(var + GN_EPS)).reshape(b, t, W_A) * p['gn_w'] + p['gn_b']
    bonus = jnp.sum(jnp.sum(r_h[:, :, None] * _heads(k_dir) * p['bonus_rk'], axis=-1, keepdims=True)
                    * v_h[:, :, None], axis=2)
    y = y + bonus.reshape(b, t, W_A)
    y_a = (y * jax.nn.silu(u[..., OFF_GATE_A:OFF_GATE_A + W_A].astype(f32))).astype(u.dtype) @ p['w_out_a']
    pooled = multiscale_pool(u[..., OFF_POOL:OFF_POOL + W_B])
    mixed = jnp.einsum('btgi,gio->btgo', pooled, p['pool_w']).reshape(b, t, W_B) * p['pool_scale']
    y_b = (mixed * jax.nn.silu(u[..., OFF_GATE_B:OFF_GATE_B + W_B].astype(f32))).astype(u.dtype) @ p['w_out_b']
    mg = jax.nn.sigmoid(u[..., OFF_MERGE:OFF_MERGE + 2 * D_MODEL].astype(f32)).reshape(b, t, 2, D_MODEL)
    merged = mg[:, :, 0] * y_a + mg[:, :, 1] * y_b
    out = merged.astype(u.dtype) @ p['w_out']
    return out, s_fin, v


def setup_inputs(seed: int = 0) -> dict:
    key = jax.random.key(seed)
    ks = jax.random.split(key, 32)
    f32 = jnp.float32

    def nrm(k, shape, scale):
        return jax.random.normal(k, shape, f32) * scale

    L = DEPTH
    return {
        'x': nrm(ks[0], (BATCH, SEQ, D_MODEL), 1.0),
        'c': nrm(ks[1], (BATCH, D_MODEL), 1.0),
        'ctx': nrm(ks[2], (BATCH, CTX_LEN, D_MODEL), 1.0),
        'c_ctx': nrm(ks[3], (D_MODEL,), 1.0),
        'ada_w': nrm(ks[4], (L, D_MODEL, 3 * D_MODEL), 0.5 * D_MODEL ** -0.5),
        'ada_b': nrm(ks[5], (L, 3 * D_MODEL), 0.02),
        'norm_g': 1.0 + nrm(ks[6], (L, D_MODEL), 0.02),
        'w_in': nrm(ks[7], (L, D_MODEL, IN_WIDTH), D_MODEL ** -0.5),
        'tok_mu': jax.random.uniform(ks[8], (L, 3, W_A), f32),
        'decay_w0': jax.random.uniform(ks[9], (L, 2, W_A), f32, -5.0, 0.0),
        'decay_lora_b': nrm(ks[10], (L, 2, R_DECAY, W_A), 0.1 * R_DECAY ** -0.5),
        'iclr_a0': nrm(ks[11], (L, 2, W_A), 0.1),
        'iclr_lora_b': nrm(ks[12], (L, 2, R_ICLR, W_A), 0.1 * R_ICLR ** -0.5),
        'key_k': 0.85 + nrm(ks[13], (L, W_A), 0.05),
        'key_a': 1.0 + nrm(ks[14], (L, 2, W_A), 0.05),
        'bonus_rk': nrm(ks[15], (L, 2, N_HEADS, HEAD_DIM), 0.1),
        'gn_w': 1.0 + nrm(ks[16], (L, W_A), 0.02),
        'gn_b': nrm(ks[17], (L, W_A), 0.02),
        'vres_v0': nrm(ks[18], (L - 1, W_A), 0.1),
        'vres_lora_a': nrm(ks[19], (L - 1, D_MODEL, R_VRES), D_MODEL ** -0.5),
        'vres_lora_b': nrm(ks[20], (L - 1, R_VRES, W_A), 0.1 * R_VRES ** -0.5),
        'pool_w': nrm(ks[21], (L, N_POOL_GROUPS, POOL_GROUP_W, POOL_GROUP_W), POOL_GROUP_W ** -0.5),
        'pool_scale': 1.0 + nrm(ks[22], (L, W_B), 0.05),
        'w_out_a': nrm(ks[23], (L, W_A, D_MODEL), W_A ** -0.5),
        'w_out_b': nrm(ks[24], (L, W_B, D_MODEL), W_B ** -0.5),
        'w_out': nrm(ks[25], (L, D_MODEL, D_MODEL), D_MODEL ** -0.5),
        'final_g': 1.0 + nrm(ks[26], (D_MODEL,), 0.02),
    }


def reference(x, c, ctx, c_ctx, ada_w, ada_b, norm_g, w_in, tok_mu, decay_w0, decay_lora_b, iclr_a0,
              iclr_lora_b, key_k, key_a, bonus_rk, gn_w, gn_b, vres_v0, vres_lora_a, vres_lora_b,
              pool_w, pool_scale, w_out_a, w_out_b, w_out, final_g):
    b = x.shape[0]
    s_zero = jnp.zeros((2, b, N_HEADS, HEAD_DIM, HEAD_DIM), jnp.float32)
    h_lat, h_ctx = x, ctx
    vf_lat = None
    vf_ctx = None
    for l in range(DEPTH):
        last = l == DEPTH - 1
        p = dict(tok_mu=tok_mu[l], decay_w0=decay_w0[l], decay_lora_b=decay_lora_b[l], iclr_a0=iclr_a0[l],
                 iclr_lora_b=iclr_lora_b[l], key_k=key_k[l], key_a=key_a[l], bonus_rk=bonus_rk[l],
                 gn_w=gn_w[l], gn_b=gn_b[l], pool_w=pool_w[l], pool_scale=pool_scale[l],
                 w_out_a=w_out_a[l], w_out_b=w_out_b[l], w_out=w_out[l])
        if l > 0:
            p.update(vres_v0=vres_v0[l - 1], vres_lora_a=vres_lora_a[l - 1], vres_lora_b=vres_lora_b[l - 1])
        shift_l, scale_l, gate_l = jnp.split((jax.nn.silu(c) @ ada_w[l] + ada_b[l])[:, None, :], 3, axis=-1)
        shift_c, scale_c, gate_c = jnp.split(jax.nn.silu(c_ctx) @ ada_w[l] + ada_b[l], 3, axis=-1)
        xn_c = rmsnorm(h_ctx, norm_g[l]) * (1.0 + scale_c) + shift_c
        xn_l = rmsnorm(h_lat, norm_g[l]) * (1.0 + scale_l) + shift_l
        out_c, s_ctx, v_c = stream_mixer(xn_c, xn_c @ w_in[l], p, vf_ctx, seq_shift, s_zero, not last)
        out_l, _, v_l = stream_mixer(xn_l, xn_l @ w_in[l], p, vf_lat, grid_shift, s_ctx, True)
        if l == 0:
            vf_ctx, vf_lat = v_c, v_l
        h_lat = h_lat + gate_l * out_l
        if not last:
            h_ctx = h_ctx + gate_c * out_c
    return rmsnorm(h_lat, final_g)
```

```python
import functools
import math

import jax
import jax.numpy as jnp
from jax import lax
from jax.experimental import pallas as pl
from jax.experimental.pallas import tpu as pltpu

F32 = jnp.float32
BF16 = jnp.bfloat16

GRID_W = 64
W_A = 1024
W_B = 1024
HEAD_DIM = 64
POOL_WINDOWS = (2, 4, 8, 16)
POOL_GROUP_W = W_B // len(POOL_WINDOWS)
R_DECAY = 64
R_ICLR = 64
R_VRES = 32
NORM_EPS = 1e-6
GN_EPS = HEAD_DIM * 1e-5
REF_OFF_POOL = 3 * W_A
REF_OFF_GATE_A = REF_OFF_POOL + W_B
REF_OFF_GATE_B = REF_OFF_GATE_A + W_A
REF_OFF_DECAY = REF_OFF_GATE_B + W_B
REF_OFF_ICLR = REF_OFF_DECAY + 2 * R_DECAY
REF_OFF_MERGE = REF_OFF_ICLR + 2 * R_ICLR

LANES = 128
MXU_DIM = 256
VMEM_LIMIT_BYTES = 56 * 1024 * 1024

GROUP_W = MXU_DIM
GROUP_HEADS = GROUP_W // HEAD_DIM
CHUNK = 64
HALO = 64
POOL_HALO = 8
LOG_DECAY_SCALE = math.exp(-0.5)


def _cparams(sem):
    return pltpu.CompilerParams(dimension_semantics=sem, vmem_limit_bytes=VMEM_LIMIT_BYTES)


def _sigmoid(x):
    return 1.0 / (1.0 + jnp.exp(-x))


def _dot(a, b):
    return jnp.dot(a, b, preferred_element_type=F32)


def _dot_nt(a, b):
    return lax.dot_general(a, b, (((1,), (1,)), ((), ())), preferred_element_type=F32)


def _dot_tn(a, b):
    return lax.dot_general(a, b, (((0,), (0,)), ((), ())), preferred_element_type=F32)


def _ada_kernel(c_ref, w_ref, b_ref, o_ref):
    c = c_ref[...]
    s = (c * _sigmoid(c)).astype(BF16)
    o_ref[0] = _dot(s, w_ref[0].astype(BF16)) + b_ref[0]


def _ada_call(c_all, ada_w, ada_b):
    n_layers, d, d3 = ada_w.shape
    rows = c_all.shape[0]
    tn = 768
    return pl.pallas_call(
        _ada_kernel,
        out_shape=jax.ShapeDtypeStruct((n_layers, rows, d3), F32),
        grid=(n_layers, d3 // tn),
        in_specs=[
            pl.BlockSpec((rows, d), lambda l, n: (0, 0)),
            pl.BlockSpec((1, d, tn), lambda l, n: (l, 0, n)),
            pl.BlockSpec((1, 1, tn), lambda l, n: (l, 0, n)),
        ],
        out_specs=pl.BlockSpec((1, rows, tn), lambda l, n: (l, 0, n)),
        compiler_params=_cparams(("arbitrary", "arbitrary")),
    )(c_all, ada_w, ada_b.reshape(n_layers, 1, d3))


def _inproj_kernel(h_ref, shift_ref, scale_ref, g_ref, w_ref, o_ref, xn_ref):
    @pl.when(pl.program_id(2) == 0)
    def _():
        x = h_ref[0]
        ms = jnp.mean(x * x, axis=-1, keepdims=True)
        y = x * lax.rsqrt(ms + NORM_EPS) * g_ref[...]
        xn_ref[...] = (y * (1.0 + scale_ref[0]) + shift_ref[0]).astype(BF16)

    o_ref[0] = _dot(xn_ref[...], w_ref[...])


def _inproj_call(h, shift, scale, g, w):
    bm, t, d = h.shape
    nu = w.shape[1]
    tm = min(1024, t)
    tn = 768
    return pl.pallas_call(
        _inproj_kernel,
        out_shape=jax.ShapeDtypeStruct((bm, t, nu), F32),
        grid=(bm, t // tm, nu // tn),
        in_specs=[
            pl.BlockSpec((1, tm, d), lambda b, m, n: (b, m, 0)),
            pl.BlockSpec((1, 1, d), lambda b, m, n: (b, 0, 0)),
            pl.BlockSpec((1, 1, d), lambda b, m, n: (b, 0, 0)),
            pl.BlockSpec((1, d), lambda b, m, n: (0, 0)),
            pl.BlockSpec((d, tn), lambda b, m, n: (0, n)),
        ],
        out_specs=pl.BlockSpec((1, tm, tn), lambda b, m, n: (b, m, n)),
        scratch_shapes=[pltpu.VMEM((tm, d), BF16)],
        compiler_params=_cparams(("arbitrary", "arbitrary", "arbitrary")),
    )(h, shift, scale, g, w)


def _mix_kernel(*refs, grid_mode, has_vres, tb):
    if has_vres:
        cur, prv, nxt, mu_ref, vf_ref, lo_ref, vb_ref, v0_ref, o_ref = refs
    else:
        cur, prv, nxt, mu_ref, o_ref = refs
    i = pl.program_id(1)
    nt = pl.num_programs(1)
    pscale = jnp.where(i > 0, 1.0, 0.0).astype(F32)
    nscale = jnp.where(i < nt - 1, 1.0, 0.0).astype(F32)
    n_ext = tb + 2 * HALO
    row = lax.broadcasted_iota(jnp.int32, (tb, 1), 0)
    col = jnp.bitwise_and(row, GRID_W - 1)
    not_first_col = jnp.where(col != 0, 1.0, 0.0).astype(F32)
    not_last_col = jnp.where(col != GRID_W - 1, 1.0, 0.0).astype(F32)

    if grid_mode:
        q = W_A // 4
        plan = [(0, q, -1, not_first_col), (q, q, 1, not_last_col), (2 * q, q, -GRID_W, None), (3 * q, q, GRID_W, None)]
    else:
        hw = W_A // 2
        plan = [(0, hw, -1, None), (hw, hw, 1, None)]

    for m in range(3):
        for (c0, cw, off, rmask) in plan:
            cs = slice(m * W_A + c0, m * W_A + c0 + cw)
            x = cur[0, :, cs]
            ext = jnp.concatenate([prv[0, :, cs] * pscale, x, nxt[0, :, cs] * nscale], axis=0)
            if off == -1:
                sh = pltpu.roll(ext, 1, axis=0)[HALO:HALO + tb]
            elif off == 1:
                sh = pltpu.roll(ext, n_ext - 1, axis=0)[HALO:HALO + tb]
            else:
                sh = ext[HALO + off:HALO + off + tb]
            if rmask is not None:
                sh = sh * rmask
            y = x + mu_ref[:, cs] * (sh - x)
            if has_vres and m == 2:
                vcs = slice(c0, c0 + cw)
                lo = lo_ref[0].astype(BF16)
                gate = _sigmoid(v0_ref[:, vcs] + _dot(lo, vb_ref[:, vcs]))
                y = y + (vf_ref[0, :, vcs] - y) * gate
            o_ref[0, :, cs] = y


def _mix_call(u, tok_mu, grid_mode, vres=None):
    b, t, _ = u.shape
    tb = min(256, t)
    nt = t // tb
    hb = tb // HALO
    nh = t // HALO
    w3 = 3 * W_A
    has_vres = vres is not None
    in_specs = [
        pl.BlockSpec((1, tb, w3), lambda bi, i: (bi, i, 0)),
        pl.BlockSpec((1, HALO, w3), lambda bi, i: (bi, jnp.maximum(i * hb - 1, 0), 0)),
        pl.BlockSpec((1, HALO, w3), lambda bi, i: (bi, jnp.minimum((i + 1) * hb, nh - 1), 0)),
        pl.BlockSpec((1, w3), lambda bi, i: (0, 0)),
    ]
    args = [u, u, u, tok_mu.reshape(1, w3)]
    if has_vres:
        v_first, vres_col, vres_b, vres_v0 = vres
        in_specs += [
            pl.BlockSpec((1, tb, W_A), lambda bi, i: (bi, i, 2)),
            pl.BlockSpec((1, tb, LANES), lambda bi, i: (bi, i, vres_col)),
            pl.BlockSpec((LANES, W_A), lambda bi, i: (0, 0)),
            pl.BlockSpec((1, W_A), lambda bi, i: (0, 0)),
        ]
        args += [v_first, u, vres_b, vres_v0]
    return pl.pallas_call(
        functools.partial(_mix_kernel, grid_mode=grid_mode, has_vres=has_vres, tb=tb),
        out_shape=jax.ShapeDtypeStruct((b, t, w3), F32),
        grid=(b, nt),
        in_specs=in_specs,
        out_specs=pl.BlockSpec((1, tb, w3), lambda bi, i: (bi, i, 0)),
        compiler_params=_cparams(("arbitrary", "arbitrary")),
    )(*args)


def _scan_kernel(*refs, tb, use_s0, emit_out):
    (rf, kf, vf, rb, kb, vb, dlf, dlb) = refs[:8]
    pos = 8
    if use_s0:
        s0_ref = refs[pos]
        pos += 1
    (dw0, dlw, ia0, ilw, kkw, kaw, brk, bm_ref) = refs[pos:pos + 8]
    pos += 8
    if emit_out:
        y_refs = refs[pos:pos + 2]
        bo_refs = refs[pos + 2:pos + 4]
        pos += 4
    sfin = refs[pos]
    s_ref = refs[pos + 1]

    i = pl.program_id(2)
    nt = pl.num_programs(2)

    @pl.when(i == 0)
    def _():
        if use_s0:
            s_ref[...] = s0_ref[:, 0, 0]
        else:
            s_ref[...] = jnp.zeros_like(s_ref)

    bm = bm_ref[...]
    bmf = bm.astype(F32)

    def bd(x):
        xb = x.astype(BF16)
        return jnp.concatenate([xb] * GROUP_HEADS, axis=0) * bm

    def headsum(x):
        hi = x.astype(BF16)
        lo = (x - hi.astype(F32)).astype(BF16)
        return _dot(hi, bm) + _dot(lo, bm)

    trow = lax.broadcasted_iota(jnp.int32, (CHUNK, GROUP_W), 0)
    scol = jnp.bitwise_and(lax.broadcasted_iota(jnp.int32, (CHUNK, GROUP_W), 1), CHUNK - 1)
    eye = jnp.where(trow == scol, 1.0, 0.0).astype(F32)
    same16 = (trow >> 4) == (scol >> 4)
    same32 = (trow >> 5) == (scol >> 5)
    m_d16 = jnp.where(same16, 1.0, 0.0).astype(F32)
    m_o32 = jnp.where(jnp.logical_and(same32, jnp.logical_not(same16)), 1.0, 0.0).astype(F32)
    m_o64 = jnp.where(same32, 0.0, 1.0).astype(F32)
    brow = lax.broadcasted_iota(jnp.int32, (tb, tb), 0)
    bcol = lax.broadcasted_iota(jnp.int32, (tb, tb), 1)
    same_chunk = (brow >> 6) == (bcol >> 6)

    def prod(p, q):
        return _dot(p.astype(BF16), bd(q))

    n_chunk = tb // CHUNK
    for d in (0, 1):
        r = (rf, rb)[d][0]
        k = (kf, kb)[d][0]
        v = (vf, vb)[d][0]
        dl = (dlf, dlb)[d][0]
        dlo = jnp.tanh(dl[:, :LANES]).astype(BF16)
        alo = dl[:, LANES:].astype(BF16)
        z = dw0[d:d + 1, :] + _dot(dlo, dlw[d])
        g = -LOG_DECAY_SCALE * _sigmoid(z)
        a = _sigmoid(ia0[d:d + 1, :] + _dot(alo, ilw[d]))
        kk = k * kkw[...]
        kk = kk / jnp.maximum(jnp.sqrt(headsum(kk * kk)), 1e-12)
        kd = k * (1.0 + (a - 1.0) * kaw[d:d + 1, :])
        bb = kk * a
        if emit_out:
            bo_refs[d][0] = headsum(r * kd * brk[d:d + 1, :]) * v

        if d == 0:
            strict = jnp.where(scol < trow, 1.0, 0.0).astype(F32)
            incl = jnp.where(scol <= trow, 1.0, 0.0).astype(F32)
            tri = jnp.where(jnp.logical_and(same_chunk, bcol <= brow), 1.0, 0.0).astype(BF16)
        else:
            strict = jnp.where(scol > trow, 1.0, 0.0).astype(F32)
            incl = jnp.where(scol >= trow, 1.0, 0.0).astype(F32)
            tri = jnp.where(jnp.logical_and(same_chunk, bcol >= brow), 1.0, 0.0).astype(BF16)
        g1 = g.astype(BF16)
        g2 = (g - g1.astype(F32)).astype(BF16)
        g3 = (g - g1.astype(F32) - g2.astype(F32)).astype(BF16)
        cum = _dot(tri, g1) + _dot(tri, g2) + _dot(tri, g3)

        order = range(n_chunk) if d == 0 else range(n_chunk - 1, -1, -1)
        for c in order:
            sl = slice(c * CHUNK, (c + 1) * CHUNK)
            lc = cum[sl]
            gc = g[sl]
            ltot = lc[CHUNK - 1:CHUNK] if d == 0 else lc[0:1]
            e_pos = jnp.exp(lc)
            e_neg = jnp.exp(-lc)
            e_tail = jnp.exp(ltot - lc)
            r_h = r[sl] * e_pos
            kap_h = kk[sl] * jnp.exp(lc - gc)
            k_h = kd[sl] * e_neg
            b_h = bb[sl] * e_neg
            k_t = kd[sl] * e_tail
            b_t = bb[sl] * e_tail
            vc = v[sl]

            x2 = jnp.concatenate([kap_h, r_h], axis=0).astype(BF16)
            gb = _dot_nt(x2, bd(b_h))
            gk = _dot_nt(x2, bd(k_h))
            a_ab = gb[:CHUNK] * strict
            a_rb = gb[CHUNK:] * incl
            a_ak = gk[:CHUNK] * strict
            a_rk = gk[CHUNK:] * incl

            n1 = -(a_ab * m_d16)
            n2 = prod(n1, n1)
            n4 = prod(n2, n2)
            n8 = prod(n4, n4)
            x = eye + n1
            x = x + prod(x, n2)
            x = x + prod(x, n4)
            x = x + prod(x, n8)
            x = x - prod(prod(x, a_ab * m_o32), x)
            tinv = x - prod(prod(x, a_ab * m_o64), x)

            m1y = _dot(jnp.concatenate([a_ak, a_rk], axis=0).astype(BF16), bd(vc))
            m1 = m1y[:CHUNK]
            tb16 = tinv.astype(BF16)
            w = _dot(tb16, bd(kap_h))
            ut = _dot(tb16, bd(m1))

            s_cur = s_ref[d]
            wr = _dot_nt(jnp.concatenate([w, r_h], axis=0).astype(BF16), s_cur.astype(BF16))
            u = -(wr[:CHUNK] + ut)
            if emit_out:
                y_refs[d][0, sl, :] = wr[CHUNK:] + prod(a_rb, u) + m1y[CHUNK:]
            ds = _dot_tn(jnp.concatenate([u, vc], axis=0).astype(BF16),
                         jnp.concatenate([b_t, k_t], axis=0).astype(BF16)) * bmf
            s_ref[d] = s_cur * jnp.exp(ltot) + ds

    @pl.when(i == nt - 1)
    def _():
        sfin[:, 0, 0] = s_ref[...]


def _scan_call(rkv, u, dl_col, s0, lp, emit_out):
    b, t, _ = rkv.shape
    tb = min(256, t)
    nt = t // tb
    ng = W_A // GROUP_W
    use_s0 = s0 is not None

    def fwd(off):
        return pl.BlockSpec((1, tb, GROUP_W), lambda bi, hg, i: (bi, i, off + hg))

    def bwd(off):
        return pl.BlockSpec((1, tb, GROUP_W), lambda bi, hg, i: (bi, nt - 1 - i, off + hg))

    in_specs = [fwd(0), fwd(ng), fwd(2 * ng), bwd(0), bwd(ng), bwd(2 * ng),
                pl.BlockSpec((1, tb, GROUP_W), lambda bi, hg, i: (bi, i, dl_col)),
                pl.BlockSpec((1, tb, GROUP_W), lambda bi, hg, i: (bi, nt - 1 - i, dl_col))]
    args = [rkv] * 6 + [u, u]
    state_spec = pl.BlockSpec((2, 1, 1, GROUP_W, GROUP_W), lambda bi, hg, i: (0, bi, hg, 0, 0))
    if use_s0:
        in_specs.append(state_spec)
        args.append(s0)
    vec2 = pl.BlockSpec((2, GROUP_W), lambda bi, hg, i: (0, hg))
    lora = pl.BlockSpec((2, LANES, GROUP_W), lambda bi, hg, i: (0, 0, hg))
    in_specs += [vec2, lora, vec2, lora,
                 pl.BlockSpec((1, GROUP_W), lambda bi, hg, i: (0, hg)), vec2, vec2,
                 pl.BlockSpec((GROUP_W, GROUP_W), lambda bi, hg, i: (0, 0))]
    args += [lp['decay_w0'], lp['decay_lora'], lp['iclr_a0'], lp['iclr_lora'],
             lp['key_k'], lp['key_a'], lp['bonus_rk'], lp['head_mask']]

    out_shape = []
    out_specs = []
    if emit_out:
        y_shape = jax.ShapeDtypeStruct((b, t, W_A), F32)
        o_f = pl.BlockSpec((1, tb, GROUP_W), lambda bi, hg, i: (bi, i, hg))
        o_b = pl.BlockSpec((1, tb, GROUP_W), lambda bi, hg, i: (bi, nt - 1 - i, hg))
        out_shape += [y_shape] * 4
        out_specs += [o_f, o_b, o_f, o_b]
    out_shape.append(jax.ShapeDtypeStruct((2, b, ng, GROUP_W, GROUP_W), F32))
    out_specs.append(state_spec)

    outs = pl.pallas_call(
        functools.partial(_scan_kernel, tb=tb, use_s0=use_s0, emit_out=emit_out),
        out_shape=out_shape,
        grid=(b, ng, nt),
        in_specs=in_specs,
        out_specs=out_specs,
        scratch_shapes=[pltpu.VMEM((2, GROUP_W, GROUP_W), F32)],
        compiler_params=_cparams(("arbitrary", "arbitrary", "arbitrary")),
    )(*args)
    return outs


def _post1_kernel(y0, y1, b0, b1, ga, gb, pc, pp, pn, mg0, mg1, gnw, gnb, pw, ps, woa, wob, bm_ref, o_ref, *, tm, t_total):
    i = pl.program_id(1)
    nt = pl.num_programs(1)
    bm = bm_ref[...]

    def headsum(x):
        hi = x.astype(BF16)
        lo = (x - hi.astype(F32)).astype(BF16)
        return _dot(hi, bm) + _dot(lo, bm)

    parts = []
    for hg in range(W_A // GROUP_W):
        cs = slice(hg * GROUP_W, (hg + 1) * GROUP_W)
        y = y0[0, :, cs] + y1[0, :, cs]
        mu = headsum(y) * (1.0 / HEAD_DIM)
        yc = y - mu
        var = headsum(yc * yc) * (1.0 / HEAD_DIM)
        yn = yc * lax.rsqrt(var + GN_EPS) * gnw[:, cs] + gnb[:, cs]
        yn = yn + b0[0, :, cs] + b1[0, :, cs]
        gate = ga[0, :, cs]
        parts.append((yn * (gate * _sigmoid(gate))).astype(BF16))
    y_a = _dot(jnp.concatenate(parts, axis=1), woa[...])

    pscale = jnp.where(i > 0, 1.0, 0.0).astype(F32)
    nscale = jnp.where(i < nt - 1, 1.0, 0.0).astype(F32)
    n_ext = tm + 2 * POOL_HALO
    tpos = (lax.broadcasted_iota(jnp.int32, (tm, 1), 0) + i * tm).astype(F32)
    parts = []
    for gi, win in enumerate(POOL_WINDOWS):
        left = win // 2
        right = win - 1 - left
        cs = slice(gi * POOL_GROUP_W, (gi + 1) * POOL_GROUP_W)
        p = pc[0, :, cs]
        ext = jnp.concatenate([pp[0, :, cs] * pscale, p, pn[0, :, cs] * nscale], axis=0)
        s = ext
        step = 1
        while step < win:
            s = s + pltpu.roll(s, step, axis=0)
            step *= 2
        if right > 0:
            s = pltpu.roll(s, n_ext - right, axis=0)
        s = s[POOL_HALO:POOL_HALO + tm]
        cnt = jnp.minimum(tpos + (right + 1), float(t_total)) - jnp.maximum(tpos - left, 0.0)
        pooled = s / cnt - p
        mixed = _dot(pooled.astype(BF16), pw[gi]) * ps[:, cs]
        gate = gb[0, :, cs]
        parts.append((mixed * (gate * _sigmoid(gate))).astype(BF16))
    y_b = _dot(jnp.concatenate(parts, axis=1), wob[...])

    merged = _sigmoid(mg0[0]) * y_a + _sigmoid(mg1[0]) * y_b
    o_ref[0] = merged.astype(BF16)


def _post1_call(ys, u, cols, lp):
    y0, y1, b0, b1 = ys
    b, t, _ = y0.shape
    d = lp['w_out_a'].shape[1]
    tm = min(256, t)
    nt = t // tm
    hb = tm // POOL_HALO
    nh = t // POOL_HALO
    blk = lambda c: pl.BlockSpec((1, tm, W_A), lambda bi, i: (bi, i, c))
    wide = lambda c: pl.BlockSpec((1, tm, d), lambda bi, i: (bi, i, c))
    const2 = lambda shape: pl.BlockSpec(shape, lambda bi, i: (0, 0))
    pool_col = cols['pool']
    pool_blocks = W_A // W_A
    in_specs = [blk(0), blk(0), blk(0), blk(0),
                blk(cols['gate_a']), blk(cols['gate_b']), blk(pool_col),
                pl.BlockSpec((1, POOL_HALO, W_B), lambda bi, i: (bi, jnp.maximum(i * hb - 1, 0), pool_col * pool_blocks)),
                pl.BlockSpec((1, POOL_HALO, W_B), lambda bi, i: (bi, jnp.minimum((i + 1) * hb, nh - 1), pool_col * pool_blocks)),
                wide(cols['merge0']), wide(cols['merge1']),
                const2((1, W_A)), const2((1, W_A)),
                pl.BlockSpec((len(POOL_WINDOWS), POOL_GROUP_W, POOL_GROUP_W), lambda bi, i: (0, 0, 0)),
                const2((1, W_B)), const2((W_A, d)), const2((W_B, d)), const2((GROUP_W, GROUP_W))]
    args = [y0, y1, b0, b1, u, u, u, u, u, u, u,
            lp['gn_w'], lp['gn_b'], lp['pool_w'], lp['pool_scale'], lp['w_out_a'], lp['w_out_b'], lp['head_mask']]
    return pl.pallas_call(
        functools.partial(_post1_kernel, tm=tm, t_total=t),
        out_shape=jax.ShapeDtypeStruct((b, t, d), BF16),
        grid=(b, nt),
        in_specs=in_specs,
        out_specs=pl.BlockSpec((1, tm, d), lambda bi, i: (bi, i, 0)),
        compiler_params=_cparams(("arbitrary", "arbitrary")),
    )(*args)


def _post2_kernel(m_ref, h_ref, gate_ref, w_ref, fg_ref, o_ref, *, final_norm):
    out = _dot(m_ref[0], w_ref[...])
    h = h_ref[0] + gate_ref[0] * out
    if final_norm:
        ms = jnp.mean(h * h, axis=-1, keepdims=True)
        h = h * lax.rsqrt(ms + NORM_EPS) * fg_ref[...]
    o_ref[0] = h


def _post2_call(merged, h, gate, w_out, final_g, final_norm):
    bm, t, d = h.shape
    tm = min(512, t)
    return pl.pallas_call(
        functools.partial(_post2_kernel, final_norm=final_norm),
        out_shape=jax.ShapeDtypeStruct((bm, t, d), F32),
        grid=(bm, t // tm),
        in_specs=[
            pl.BlockSpec((1, tm, d), lambda b, m: (b, m, 0)),
            pl.BlockSpec((1, tm, d), lambda b, m: (b, m, 0)),
            pl.BlockSpec((1, 1, d), lambda b, m: (b, 0, 0)),
            pl.BlockSpec((d, d), lambda b, m: (0, 0)),
            pl.BlockSpec((1, d), lambda b, m: (0, 0)),
        ],
        out_specs=pl.BlockSpec((1, tm, d), lambda b, m: (b, m, 0)),
        compiler_params=_cparams(("arbitrary", "arbitrary")),
    )(merged, h, gate, w_out, final_g)


def _pad_rows(w, rows):
    return jnp.pad(w, ((0, 0),) * (w.ndim - 2) + ((0, rows - w.shape[-2]), (0, 0)))


def kernel(x, c, ctx, c_ctx, ada_w, ada_b, norm_g, w_in, tok_mu, decay_w0, decay_lora_b, iclr_a0, iclr_lora_b, key_k, key_a, bonus_rk, gn_w, gn_b, vres_v0, vres_lora_a, vres_lora_b, pool_w, pool_scale, w_out_a, w_out_b, w_out, final_g):
    depth = w_in.shape[0]
    batch, seq, d = x.shape
    ctx_len = ctx.shape[1]
    assert seq % GRID_W == 0 and seq % CHUNK == 0 and ctx_len % CHUNK == 0
    assert d % LANES == 0

    off_merge = 3 * W_A + W_B + W_A + W_B
    off_dl = off_merge + 2 * d
    off_vres = off_dl + 2 * R_DECAY + 2 * R_ICLR
    nu = -(-(off_vres + LANES) // 768) * 768
    cols = dict(pool=REF_OFF_POOL // W_B, gate_a=REF_OFF_GATE_A // W_A, gate_b=REF_OFF_GATE_B // W_A,
                merge0=off_merge // d, merge1=off_merge // d + 1)
    assert off_merge % d == 0 and off_dl % GROUP_W == 0 and off_vres % LANES == 0

    head_mask = (jnp.arange(GROUP_W)[:, None] // HEAD_DIM == jnp.arange(GROUP_W)[None, :] // HEAD_DIM).astype(BF16)

    rows = -(-(batch + 1) // 8) * 8
    c_all = jnp.zeros((rows, d), F32).at[:batch].set(c).at[batch].set(c_ctx)
    mod = _ada_call(c_all, ada_w, ada_b)

    h_lat, h_ctx = x, ctx
    s_ctx = None
    vfirst = {}
    for l in range(depth):
        last = l == depth - 1
        w_l = w_in[l]
        pieces = [w_l[:, :REF_OFF_DECAY], w_l[:, REF_OFF_MERGE:REF_OFF_MERGE + 2 * d],
                  w_l[:, REF_OFF_DECAY:REF_OFF_MERGE]]
        if l > 0:
            pieces.append(vres_lora_a[l - 1])
        w_perm = jnp.concatenate(pieces, axis=1)
        w_perm = jnp.pad(w_perm, ((0, 0), (0, nu - w_perm.shape[1]))).astype(BF16)

        zpad = lambda w: jnp.pad(w, ((0, 0), (0, LANES - w.shape[1]), (0, 0)))
        dlw = decay_lora_b[l]
        dlw = jnp.stack([jnp.pad(dlw[0], ((0, LANES - R_DECAY), (0, 0))),
                         jnp.pad(dlw[1], ((R_DECAY, LANES - 2 * R_DECAY), (0, 0)))]).astype(BF16)
        ilw = iclr_lora_b[l]
        ilw = jnp.stack([jnp.pad(ilw[0], ((0, LANES - R_ICLR), (0, 0))),
                         jnp.pad(ilw[1], ((R_ICLR, LANES - 2 * R_ICLR), (0, 0)))]).astype(BF16)
        lp = dict(decay_w0=decay_w0[l], decay_lora=dlw, iclr_a0=iclr_a0[l], iclr_lora=ilw,
                  key_k=key_k[l].reshape(1, W_A), key_a=key_a[l], bonus_rk=bonus_rk[l].reshape(2, W_A),
                  head_mask=head_mask, gn_w=gn_w[l].reshape(1, W_A), gn_b=gn_b[l].reshape(1, W_A),
                  pool_w=pool_w[l].astype(BF16), pool_scale=pool_scale[l].reshape(1, W_B),
                  w_out_a=w_out_a[l].astype(BF16), w_out_b=w_out_b[l].astype(BF16))
        w_out_l = w_out[l].astype(BF16)
        g_l = norm_g[l].reshape(1, d)

        m = mod[l]
        shift_l, scale_l, gate_l = (m[:batch, j * d:(j + 1) * d].reshape(batch, 1, d) for j in range(3))
        shift_c, scale_c, gate_c = (m[batch:batch + 1, j * d:(j + 1) * d].reshape(1, 1, d) for j in range(3))

        u_c = _inproj_call(h_ctx.reshape(1, batch * ctx_len, d), shift_c, scale_c, g_l, w_perm)
        u_c = u_c.reshape(batch, ctx_len, nu)
        u_l = _inproj_call(h_lat, shift_l, scale_l, g_l, w_perm)

        vres_c = vres_l = None
        if l > 0:
            vb = _pad_rows(vres_lora_b[l - 1], LANES).astype(BF16)
            v0 = vres_v0[l - 1].reshape(1, W_A)
            vres_c = (vfirst['c'], off_vres // LANES, vb, v0)
            vres_l = (vfirst['l'], off_vres // LANES, vb, v0)
        rkv_c = _mix_call(u_c, tok_mu[l], False, vres_c)
        rkv_l = _mix_call(u_l, tok_mu[l], True, vres_l)
        if l == 0:
            vfirst = dict(c=rkv_c, l=rkv_l)

        outs_c = _scan_call(rkv_c, u_c, off_dl // GROUP_W, None, lp, not last)
        s_ctx = outs_c[-1]
        outs_l = _scan_call(rkv_l, u_l, off_dl // GROUP_W, s_ctx, lp, True)

        merged_l = _post1_call(outs_l[:4], u_l, cols, lp)
        h_lat = _post2_call(merged_l, h_lat, gate_l, w_out_l, final_g.reshape(1, d), last)
        if not last:
            merged_c = _post1_call(outs_c[:4], u_c, cols, lp)
            h_ctx = _post2_call(merged_c.reshape(1, batch * ctx_len, d), h_ctx.reshape(1, batch * ctx_len, d),
                                gate_c, w_out_l, final_g.reshape(1, d), False).reshape(batch, ctx_len, d)
    return h_lat
```

```python
import functools
import math

import jax
import jax.numpy as jnp
from jax import lax
from jax.experimental import pallas as pl
from jax.experimental.pallas import tpu as pltpu

F32 = jnp.float32
BF16 = jnp.bfloat16

GRID_W = 64
W_A = 1024
W_B = 1024
HEAD_DIM = 64
POOL_WINDOWS = (2, 4, 8, 16)
POOL_GROUP_W = W_B // len(POOL_WINDOWS)
R_DECAY = 64
R_ICLR = 64
R_VRES = 32
NORM_EPS = 1e-6
GN_EPS = HEAD_DIM * 1e-5
REF_OFF_POOL = 3 * W_A
REF_OFF_GATE_A = REF_OFF_POOL + W_B
REF_OFF_GATE_B = REF_OFF_GATE_A + W_A
REF_OFF_DECAY = REF_OFF_GATE_B + W_B
REF_OFF_ICLR = REF_OFF_DECAY + 2 * R_DECAY
REF_OFF_MERGE = REF_OFF_ICLR + 2 * R_ICLR

LANES = 128
MXU_DIM = 256
VMEM_LIMIT_BYTES = 56 * 1024 * 1024

GROUP_W = MXU_DIM
GROUP_HEADS = GROUP_W // HEAD_DIM
CHUNK = 64
HALO = 64
POOL_HALO = 8
LOG_DECAY_SCALE = math.exp(-0.5)


def _cparams(sem):
    return pltpu.CompilerParams(dimension_semantics=sem, vmem_limit_bytes=VMEM_LIMIT_BYTES)


def _sigmoid(x):
    return 1.0 / (1.0 + jnp.exp(-x))


def _dot(a, b):
    return jnp.dot(a, b, preferred_element_type=F32)


def _dot_nt(a, b):
    return lax.dot_general(a, b, (((1,), (1,)), ((), ())), preferred_element_type=F32)


def _dot_tn(a, b):
    return lax.dot_general(a, b, (((0,), (0,)), ((), ())), preferred_element_type=F32)


def _ada_kernel(c_ref, w_ref, b_ref, o_ref):
    c = c_ref[...]
    s = (c * _sigmoid(c)).astype(BF16)
    o_ref[0] = _dot(s, w_ref[0].astype(BF16)) + b_ref[0]


def _ada_call(c_all, ada_w, ada_b):
    n_layers, d, d3 = ada_w.shape
    rows = c_all.shape[0]
    tn = 768
    return pl.pallas_call(
        _ada_kernel,
        out_shape=jax.ShapeDtypeStruct((n_layers, rows, d3), F32),
        grid=(n_layers, d3 // tn),
        in_specs=[
            pl.BlockSpec((rows, d), lambda l, n: (0, 0)),
            pl.BlockSpec((1, d, tn), lambda l, n: (l, 0, n)),
            pl.BlockSpec((1, 1, tn), lambda l, n: (l, 0, n)),
        ],
        out_specs=pl.BlockSpec((1, rows, tn), lambda l, n: (l, 0, n)),
        compiler_params=_cparams(("arbitrary", "arbitrary")),
    )(c_all, ada_w, ada_b.reshape(n_layers, 1, d3))


def _inproj_kernel(h_ref, shift_ref, scale_ref, g_ref, w_ref, o_ref, xn_ref):
    @pl.when(pl.program_id(2) == 0)
    def _():
        x = h_ref[0]
        ms = jnp.mean(x * x, axis=-1, keepdims=True)
        y = x * lax.rsqrt(ms + NORM_EPS) * g_ref[...]
        xn_ref[...] = (y * (1.0 + scale_ref[0]) + shift_ref[0]).astype(BF16)

    o_ref[0] = _dot(xn_ref[...], w_ref[...])


def _inproj_call(h, shift, scale, g, w):
    bm, t, d = h.shape
    nu = w.shape[1]
    tm = min(1024, t)
    tn = 768
    return pl.pallas_call(
        _inproj_kernel,
        out_shape=jax.ShapeDtypeStruct((bm, t, nu), F32),
        grid=(bm, t // tm, nu // tn),
        in_specs=[
            pl.BlockSpec((1, tm, d), lambda b, m, n: (b, m, 0)),
            pl.BlockSpec((1, 1, d), lambda b, m, n: (b, 0, 0)),
            pl.BlockSpec((1, 1, d), lambda b, m, n: (b, 0, 0)),
            pl.BlockSpec((1, d), lambda b, m, n: (0, 0)),
            pl.BlockSpec((d, tn), lambda b, m, n: (0, n)),
        ],
        out_specs=pl.BlockSpec((1, tm, tn), lambda b, m, n: (b, m, n)),
        scratch_shapes=[pltpu.VMEM((tm, d), BF16)],
        compiler_params=_cparams(("arbitrary", "arbitrary", "arbitrary")),
    )(h, shift, scale, g, w)


def _mix_kernel(*refs, grid_mode, has_vres, tb):
    if has_vres:
        cur, prv, nxt, mu_ref, vf_ref, lo_ref, vb_ref, v0_ref, o_ref = refs
    else:
        cur, prv, nxt, mu_ref, o_ref = refs
    i = pl.program_id(1)
    nt = pl.num_programs(1)
    pscale = jnp.where(i > 0, 1.0, 0.0).astype(F32)
    nscale = jnp.where(i < nt - 1, 1.0, 0.0).astype(F32)
    n_ext = tb + 2 * HALO
    row = lax.broadcasted_iota(jnp.int32, (tb, 1), 0)
    col = jnp.bitwise_and(row, GRID_W - 1)
    not_first_col = jnp.where(col != 0, 1.0, 0.0).astype(F32)
    not_last_col = jnp.where(col != GRID_W - 1, 1.0, 0.0).astype(F32)

    if grid_mode:
        q = W_A // 4
        plan = [(0, q, -1, not_first_col), (q, q, 1, not_last_col), (2 * q, q, -GRID_W, None), (3 * q, q, GRID_W, None)]
    else:
        hw = W_A // 2
        plan = [(0, hw, -1, None), (hw, hw, 1, None)]

    for m in range(3):
        for (c0, cw, off, rmask) in plan:
            cs = slice(m * W_A + c0, m * W_A + c0 + cw)
            x = cur[0, :, cs]
            ext = jnp.concatenate([prv[0, :, cs] * pscale, x, nxt[0, :, cs] * nscale], axis=0)
            if off == -1:
                sh = pltpu.roll(ext, 1, axis=0)[HALO:HALO + tb]
            elif off == 1:
                sh = pltpu.roll(ext, n_ext - 1, axis=0)[HALO:HALO + tb]
            else:
                sh = ext[HALO + off:HALO + off + tb]
            if rmask is not None:
                sh = sh * rmask
            y = x + mu_ref[:, cs] * (sh - x)
            if has_vres and m == 2:
                vcs = slice(c0, c0 + cw)
                lo = lo_ref[0].astype(BF16)
                gate = _sigmoid(v0_ref[:, vcs] + _dot(lo, vb_ref[:, vcs]))
                y = y + (vf_ref[0, :, vcs] - y) * gate
            o_ref[0, :, cs] = y


def _mix_call(u, tok_mu, grid_mode, vres=None):
    b, t, _ = u.shape
    tb = min(256, t)
    nt = t // tb
    hb = tb // HALO
    nh = t // HALO
    w3 = 3 * W_A
    has_vres = vres is not None
    in_specs = [
        pl.BlockSpec((1, tb, w3), lambda bi, i: (bi, i, 0)),
        pl.BlockSpec((1, HALO, w3), lambda bi, i: (bi, jnp.maximum(i * hb - 1, 0), 0)),
        pl.BlockSpec((1, HALO, w3), lambda bi, i: (bi, jnp.minimum((i + 1) * hb, nh - 1), 0)),
        pl.BlockSpec((1, w3), lambda bi, i: (0, 0)),
    ]
    args = [u, u, u, tok_mu.reshape(1, w3)]
    if has_vres:
        v_first, vres_col, vres_b, vres_v0 = vres
        in_specs += [
            pl.BlockSpec((1, tb, W_A), lambda bi, i: (bi, i, 2)),
            pl.BlockSpec((1, tb, LANES), lambda bi, i: (bi, i, vres_col)),
            pl.BlockSpec((LANES, W_A), lambda bi, i: (0, 0)),
            pl.BlockSpec((1, W_A), lambda bi, i: (0, 0)),
        ]
        args += [v_first, u, vres_b, vres_v0]
    return pl.pallas_call(
        functools.partial(_mix_kernel, grid_mode=grid_mode, has_vres=has_vres, tb=tb),
        out_shape=jax.ShapeDtypeStruct((b, t, w3), F32),
        grid=(b, nt),
        in_specs=in_specs,
        out_specs=pl.BlockSpec((1, tb, w3), lambda bi, i: (bi, i, 0)),
        compiler_params=_cparams(("arbitrary", "arbitrary")),
    )(*args)


def _scan_kernel(*refs, tb, use_s0, emit_out):
    (rf, kf, vf, rb, kb, vb, dlf, dlb) = refs[:8]
    pos = 8
    if use_s0:
        s0_ref = refs[pos]
        pos += 1
    (dw0, dlw, ia0, ilw, kkw, kaw, brk, bm_ref) = refs[pos:pos + 8]
    pos += 8
    if emit_out:
        y_refs = refs[pos:pos + 2]
        bo_refs = refs[pos + 2:pos + 4]
        pos += 4
    sfin = refs[pos]
    s_ref = refs[pos + 1]

    i = pl.program_id(2)
    nt = pl.num_programs(2)

    @pl.when(i == 0)
    def _():
        if use_s0:
            s_ref[...] = s0_ref[:, 0, 0]
        else:
            s_ref[...] = jnp.zeros_like(s_ref)

    bm = bm_ref[...]
    bmf = bm.astype(F32)

    def bd(x):
        xb = x.astype(BF16)
        return jnp.concatenate([xb] * GROUP_HEADS, axis=0) * bm

    def headsum(x):
        hi = x.astype(BF16)
        lo = (x - hi.astype(F32)).astype(BF16)
        return _dot(hi, bm) + _dot(lo, bm)

    trow = lax.broadcasted_iota(jnp.int32, (CHUNK, GROUP_W), 0)
    scol = jnp.bitwise_and(lax.broadcasted_iota(jnp.int32, (CHUNK, GROUP_W), 1), CHUNK - 1)
    eye = jnp.where(trow == scol, 1.0, 0.0).astype(F32)
    same16 = (trow >> 4) == (scol >> 4)
    same32 = (trow >> 5) == (scol >> 5)
    m_d16 = jnp.where(same16, 1.0, 0.0).astype(F32)
    m_o32 = jnp.where(jnp.logical_and(same32, jnp.logical_not(same16)), 1.0, 0.0).astype(F32)
    m_o64 = jnp.where(same32, 0.0, 1.0).astype(F32)
    brow = lax.broadcasted_iota(jnp.int32, (tb, tb), 0)
    bcol = lax.broadcasted_iota(jnp.int32, (tb, tb), 1)
    same_chunk = (brow >> 6) == (bcol >> 6)

    def prod(p, q):
        return _dot(p.astype(BF16), bd(q))

    n_chunk = tb // CHUNK
    units = []
    for d in (0, 1):
        r = (rf, rb)[d][0]
        k = (kf, kb)[d][0]
        v = (vf, vb)[d][0]
        dl = (dlf, dlb)[d][0]
        dlo = jnp.tanh(dl[:, :LANES]).astype(BF16)
        alo = dl[:, LANES:].astype(BF16)
        z = dw0[d:d + 1, :] + _dot(dlo, dlw[d])
        g = -LOG_DECAY_SCALE * _sigmoid(z)
        a = _sigmoid(ia0[d:d + 1, :] + _dot(alo, ilw[d]))
        kk = k * kkw[...]
        kk = kk / jnp.maximum(jnp.sqrt(headsum(kk * kk)), 1e-12)
        kd = k * (1.0 + (a - 1.0) * kaw[d:d + 1, :])
        bb = kk * a
        if emit_out:
            bo_refs[d][0] = headsum(r * kd * brk[d:d + 1, :]) * v

        if d == 0:
            strict = jnp.where(scol < trow, 1.0, 0.0).astype(F32)
            incl = jnp.where(scol <= trow, 1.0, 0.0).astype(F32)
            tri = jnp.where(jnp.logical_and(same_chunk, bcol <= brow), 1.0, 0.0).astype(BF16)
        else:
            strict = jnp.where(scol > trow, 1.0, 0.0).astype(F32)
            incl = jnp.where(scol >= trow, 1.0, 0.0).astype(F32)
            tri = jnp.where(jnp.logical_and(same_chunk, bcol >= brow), 1.0, 0.0).astype(BF16)
        g1 = g.astype(BF16)
        g2 = (g - g1.astype(F32)).astype(BF16)
        g3 = (g - g1.astype(F32) - g2.astype(F32)).astype(BF16)
        cum = _dot(tri, g1) + _dot(tri, g2) + _dot(tri, g3)

        order = range(n_chunk) if d == 0 else range(n_chunk - 1, -1, -1)
        for c in order:
            sl = slice(c * CHUNK, (c + 1) * CHUNK)
            lc = cum[sl]
            ltot = lc[CHUNK - 1:CHUNK] if d == 0 else lc[0:1]
            e_neg = jnp.exp(-lc)
            e_tail = jnp.exp(ltot - lc)
            units.append(dict(
                d=d, sl=sl, strict=strict, incl=incl, vc=v[sl], p_tot=jnp.exp(ltot),
                r_h=r[sl] * jnp.exp(lc), kap_h=kk[sl] * jnp.exp(lc - g[sl]),
                k_h=kd[sl] * e_neg, b_h=bb[sl] * e_neg, k_t=kd[sl] * e_tail, b_t=bb[sl] * e_tail))

    for q in units:
        q['x2'] = jnp.concatenate([q['kap_h'], q['r_h']], axis=0).astype(BF16)
    for q in units:
        q['gb'] = _dot_nt(q['x2'], bd(q['b_h']))
    for q in units:
        q['gk'] = _dot_nt(q['x2'], bd(q['k_h']))
    for q in units:
        q['a_ab'] = q['gb'][:CHUNK] * q['strict']
        q['a_rb'] = q['gb'][CHUNK:] * q['incl']
        q['a_kk'] = jnp.concatenate([q['gk'][:CHUNK] * q['strict'], q['gk'][CHUNK:] * q['incl']], axis=0).astype(BF16)
        q['n1'] = -(q['a_ab'] * m_d16)
    for q in units:
        q['m1y'] = _dot(q['a_kk'], bd(q['vc']))

    for q in units:
        q['n2'] = prod(q['n1'], q['n1'])
    for q in units:
        q['n4'] = prod(q['n2'], q['n2'])
        q['x'] = eye + q['n1']
    for q in units:
        q['x'] = q['x'] + prod(q['x'], q['n2'])
    for q in units:
        q['n8'] = prod(q['n4'], q['n4'])
    for q in units:
        q['x'] = q['x'] + prod(q['x'], q['n4'])
    for q in units:
        q['x'] = q['x'] + prod(q['x'], q['n8'])
    for m_off in (m_o32, m_o64):
        for q in units:
            q['xo'] = prod(q['x'], q['a_ab'] * m_off)
        for q in units:
            q['x'] = q['x'] - prod(q['xo'], q['x'])

    for q in units:
        q['t16'] = q['x'].astype(BF16)
        q['w'] = _dot(q['t16'], bd(q['kap_h']))
    for q in units:
        q['ut'] = _dot(q['t16'], bd(q['m1y'][:CHUNK]))
    for q in units:
        q['xs'] = (_dot_tn(q['w'].astype(BF16), q['b_t'].astype(BF16)) * bmf).astype(BF16)
    for q in units:
        q['ns'] = _dot_tn(jnp.concatenate([q['vc'], -q['ut']], axis=0).astype(BF16),
                          jnp.concatenate([q['k_t'], q['b_t']], axis=0).astype(BF16)) * bmf

    per_dir = [[q for q in units if q['d'] == d] for d in (0, 1)]
    s_cur = [s_ref[0], s_ref[1]]
    for ci in range(n_chunk):
        for d in (0, 1):
            q = per_dir[d][ci]
            sb = s_cur[d].astype(BF16)
            q['s0'] = sb
            s_cur[d] = s_cur[d] * q['p_tot'] - _dot(sb, q['xs']) + q['ns']
    s_ref[0] = s_cur[0]
    s_ref[1] = s_cur[1]

    if emit_out:
        for q in units:
            q['wr'] = _dot_nt(jnp.concatenate([q['w'], q['r_h']], axis=0).astype(BF16), q['s0'])
        for q in units:
            q['u'] = -(q['wr'][:CHUNK] + q['ut'])
        for q in units:
            y_refs[q['d']][0, q['sl'], :] = q['wr'][CHUNK:] + prod(q['a_rb'], q['u']) + q['m1y'][CHUNK:]

    @pl.when(i == nt - 1)
    def _():
        sfin[:, 0, 0] = s_ref[...]


def _scan_call(rkv, u, dl_col, s0, lp, emit_out):
    b, t, _ = rkv.shape
    tb = min(256, t)
    nt = t // tb
    ng = W_A // GROUP_W
    use_s0 = s0 is not None

    def fwd(off):
        return pl.BlockSpec((1, tb, GROUP_W), lambda bi, hg, i: (bi, i, off + hg))

    def bwd(off):
        return pl.BlockSpec((1, tb, GROUP_W), lambda bi, hg, i: (bi, nt - 1 - i, off + hg))

    in_specs = [fwd(0), fwd(ng), fwd(2 * ng), bwd(0), bwd(ng), bwd(2 * ng),
                pl.BlockSpec((1, tb, GROUP_W), lambda bi, hg, i: (bi, i, dl_col)),
                pl.BlockSpec((1, tb, GROUP_W), lambda bi, hg, i: (bi, nt - 1 - i, dl_col))]
    args = [rkv] * 6 + [u, u]
    state_spec = pl.BlockSpec((2, 1, 1, GROUP_W, GROUP_W), lambda bi, hg, i: (0, bi, hg, 0, 0))
    if use_s0:
        in_specs.append(state_spec)
        args.append(s0)
    vec2 = pl.BlockSpec((2, GROUP_W), lambda bi, hg, i: (0, hg))
    lora = pl.BlockSpec((2, LANES, GROUP_W), lambda bi, hg, i: (0, 0, hg))
    in_specs += [vec2, lora, vec2, lora,
                 pl.BlockSpec((1, GROUP_W), lambda bi, hg, i: (0, hg)), vec2, vec2,
                 pl.BlockSpec((GROUP_W, GROUP_W), lambda bi, hg, i: (0, 0))]
    args += [lp['decay_w0'], lp['decay_lora'], lp['iclr_a0'], lp['iclr_lora'],
             lp['key_k'], lp['key_a'], lp['bonus_rk'], lp['head_mask']]

    out_shape = []
    out_specs = []
    if emit_out:
        y_shape = jax.ShapeDtypeStruct((b, t, W_A), F32)
        o_f = pl.BlockSpec((1, tb, GROUP_W), lambda bi, hg, i: (bi, i, hg))
        o_b = pl.BlockSpec((1, tb, GROUP_W), lambda bi, hg, i: (bi, nt - 1 - i, hg))
        out_shape += [y_shape] * 4
        out_specs += [o_f, o_b, o_f, o_b]
    out_shape.append(jax.ShapeDtypeStruct((2, b, ng, GROUP_W, GROUP_W), F32))
    out_specs.append(state_spec)

    outs = pl.pallas_call(
        functools.partial(_scan_kernel, tb=tb, use_s0=use_s0, emit_out=emit_out),
        out_shape=out_shape,
        grid=(b, ng, nt),
        in_specs=in_specs,
        out_specs=out_specs,
        scratch_shapes=[pltpu.VMEM((2, GROUP_W, GROUP_W), F32)],
        compiler_params=_cparams(("arbitrary", "arbitrary", "arbitrary")),
    )(*args)
    return outs


def _post1_kernel(y0, y1, b0, b1, ga, gb, pc, pp, pn, mg0, mg1, gnw, gnb, pw, ps, woa, wob, bm_ref, o_ref, *, tm, t_total):
    i = pl.program_id(1)
    nt = pl.num_programs(1)
    bm = bm_ref[...]

    def headsum(x):
        hi = x.astype(BF16)
        lo = (x - hi.astype(F32)).astype(BF16)
        return _dot(hi, bm) + _dot(lo, bm)

    parts = []
    for hg in range(W_A // GROUP_W):
        cs = slice(hg * GROUP_W, (hg + 1) * GROUP_W)
        y = y0[0, :, cs] + y1[0, :, cs]
        mu = headsum(y) * (1.0 / HEAD_DIM)
        yc = y - mu
        var = headsum(yc * yc) * (1.0 / HEAD_DIM)
        yn = yc * lax.rsqrt(var + GN_EPS) * gnw[:, cs] + gnb[:, cs]
        yn = yn + b0[0, :, cs] + b1[0, :, cs]
        gate = ga[0, :, cs]
        parts.append((yn * (gate * _sigmoid(gate))).astype(BF16))
    y_a = _dot(jnp.concatenate(parts, axis=1), woa[...])

    pscale = jnp.where(i > 0, 1.0, 0.0).astype(F32)
    nscale = jnp.where(i < nt - 1, 1.0, 0.0).astype(F32)
    n_ext = tm + 2 * POOL_HALO
    tpos = (lax.broadcasted_iota(jnp.int32, (tm, 1), 0) + i * tm).astype(F32)
    parts = []
    for gi, win in enumerate(POOL_WINDOWS):
        left = win // 2
        right = win - 1 - left
        cs = slice(gi * POOL_GROUP_W, (gi + 1) * POOL_GROUP_W)
        p = pc[0, :, cs]
        ext = jnp.concatenate([pp[0, :, cs] * pscale, p, pn[0, :, cs] * nscale], axis=0)
        s = ext
        step = 1
        while step < win:
            s = s + pltpu.roll(s, step, axis=0)
            step *= 2
        if right > 0:
            s = pltpu.roll(s, n_ext - right, axis=0)
        s = s[POOL_HALO:POOL_HALO + tm]
        cnt = jnp.minimum(tpos + (right + 1), float(t_total)) - jnp.maximum(tpos - left, 0.0)
        pooled = s / cnt - p
        mixed = _dot(pooled.astype(BF16), pw[gi]) * ps[:, cs]
        gate = gb[0, :, cs]
        parts.append((mixed * (gate * _sigmoid(gate))).astype(BF16))
    y_b = _dot(jnp.concatenate(parts, axis=1), wob[...])

    merged = _sigmoid(mg0[0]) * y_a + _sigmoid(mg1[0]) * y_b
    o_ref[0] = merged.astype(BF16)


def _post1_call(ys, u, cols, lp):
    y0, y1, b0, b1 = ys
    b, t, _ = y0.shape
    d = lp['w_out_a'].shape[1]
    tm = min(256, t)
    nt = t // tm
    hb = tm // POOL_HALO
    nh = t // POOL_HALO
    blk = lambda c: pl.BlockSpec((1, tm, W_A), lambda bi, i: (bi, i, c))
    wide = lambda c: pl.BlockSpec((1, tm, d), lambda bi, i: (bi, i, c))
    const2 = lambda shape: pl.BlockSpec(shape, lambda bi, i: (0, 0))
    pool_col = cols['pool']
    pool_blocks = W_A // W_A
    in_specs = [blk(0), blk(0), blk(0), blk(0),
                blk(cols['gate_a']), blk(cols['gate_b']), blk(pool_col),
                pl.BlockSpec((1, POOL_HALO, W_B), lambda bi, i: (bi, jnp.maximum(i * hb - 1, 0), pool_col * pool_blocks)),
                pl.BlockSpec((1, POOL_HALO, W_B), lambda bi, i: (bi, jnp.minimum((i + 1) * hb, nh - 1), pool_col * pool_blocks)),
                wide(cols['merge0']), wide(cols['merge1']),
                const2((1, W_A)), const2((1, W_A)),
                pl.BlockSpec((len(POOL_WINDOWS), POOL_GROUP_W, POOL_GROUP_W), lambda bi, i: (0, 0, 0)),
                const2((1, W_B)), const2((W_A, d)), const2((W_B, d)), const2((GROUP_W, GROUP_W))]
    args = [y0, y1, b0, b1, u, u, u, u, u, u, u,
            lp['gn_w'], lp['gn_b'], lp['pool_w'], lp['pool_scale'], lp['w_out_a'], lp['w_out_b'], lp['head_mask']]
    return pl.pallas_call(
        functools.partial(_post1_kernel, tm=tm, t_total=t),
        out_shape=jax.ShapeDtypeStruct((b, t, d), BF16),
        grid=(b, nt),
        in_specs=in_specs,
        out_specs=pl.BlockSpec((1, tm, d), lambda bi, i: (bi, i, 0)),
        compiler_params=_cparams(("arbitrary", "arbitrary")),
    )(*args)


def _post2_kernel(m_ref, h_ref, gate_ref, w_ref, fg_ref, o_ref, *, final_norm):
    out = _dot(m_ref[0], w_ref[...])
    h = h_ref[0] + gate_ref[0] * out
    if final_norm:
        ms = jnp.mean(h * h, axis=-1, keepdims=True)
        h = h * lax.rsqrt(ms + NORM_EPS) * fg_ref[...]
    o_ref[0] = h


def _post2_call(merged, h, gate, w_out, final_g, final_norm):
    bm, t, d = h.shape
    tm = min(512, t)
    return pl.pallas_call(
        functools.partial(_post2_kernel, final_norm=final_norm),
        out_shape=jax.ShapeDtypeStruct((bm, t, d), F32),
        grid=(bm, t // tm),
        in_specs=[
            pl.BlockSpec((1, tm, d), lambda b, m: (b, m, 0)),
            pl.BlockSpec((1, tm, d), lambda b, m: (b, m, 0)),
            pl.BlockSpec((1, 1, d), lambda b, m: (b, 0, 0)),
            pl.BlockSpec((d, d), lambda b, m: (0, 0)),
            pl.BlockSpec((1, d), lambda b, m: (0, 0)),
        ],
        out_specs=pl.BlockSpec((1, tm, d), lambda b, m: (b, m, 0)),
        compiler_params=_cparams(("arbitrary", "arbitrary")),
    )(merged, h, gate, w_out, final_g)


def _pad_rows(w, rows):
    return jnp.pad(w, ((0, 0),) * (w.ndim - 2) + ((0, rows - w.shape[-2]), (0, 0)))


def kernel(x, c, ctx, c_ctx, ada_w, ada_b, norm_g, w_in, tok_mu, decay_w0, decay_lora_b, iclr_a0, iclr_lora_b, key_k, key_a, bonus_rk, gn_w, gn_b, vres_v0, vres_lora_a, vres_lora_b, pool_w, pool_scale, w_out_a, w_out_b, w_out, final_g):
    depth = w_in.shape[0]
    batch, seq, d = x.shape
    ctx_len = ctx.shape[1]
    assert seq % GRID_W == 0 and seq % CHUNK == 0 and ctx_len % CHUNK == 0
    assert d % LANES == 0

    off_merge = 3 * W_A + W_B + W_A + W_B
    off_dl = off_merge + 2 * d
    off_vres = off_dl + 2 * R_DECAY + 2 * R_ICLR
    nu = -(-(off_vres + LANES) // 768) * 768
    cols = dict(pool=REF_OFF_POOL // W_B, gate_a=REF_OFF_GATE_A // W_A, gate_b=REF_OFF_GATE_B // W_A,
                merge0=off_merge // d, merge1=off_merge // d + 1)
    assert off_merge % d == 0 and off_dl % GROUP_W == 0 and off_vres % LANES == 0

    head_mask = (jnp.arange(GROUP_W)[:, None] // HEAD_DIM == jnp.arange(GROUP_W)[None, :] // HEAD_DIM).astype(BF16)

    rows = -(-(batch + 1) // 8) * 8
    c_all = jnp.zeros((rows, d), F32).at[:batch].set(c).at[batch].set(c_ctx)
    mod = _ada_call(c_all, ada_w, ada_b)

    h_lat, h_ctx = x, ctx
    s_ctx = None
    vfirst = {}
    for l in range(depth):
        last = l == depth - 1
        w_l = w_in[l]
        pieces = [w_l[:, :REF_OFF_DECAY], w_l[:, REF_OFF_MERGE:REF_OFF_MERGE + 2 * d],
                  w_l[:, REF_OFF_DECAY:REF_OFF_MERGE]]
        if l > 0:
            pieces.append(vres_lora_a[l - 1])
        w_perm = jnp.concatenate(pieces, axis=1)
        w_perm = jnp.pad(w_perm, ((0, 0), (0, nu - w_perm.shape[1]))).astype(BF16)

        zpad = lambda w: jnp.pad(w, ((0, 0), (0, LANES - w.shape[1]), (0, 0)))
        dlw = decay_lora_b[l]
        dlw = jnp.stack([jnp.pad(dlw[0], ((0, LANES - R_DECAY), (0, 0))),
                         jnp.pad(dlw[1], ((R_DECAY, LANES - 2 * R_DECAY), (0, 0)))]).astype(BF16)
        ilw = iclr_lora_b[l]
        ilw = jnp.stack([jnp.pad(ilw[0], ((0, LANES - R_ICLR), (0, 0))),
                         jnp.pad(ilw[1], ((R_ICLR, LANES - 2 * R_ICLR), (0, 0)))]).astype(BF16)
        lp = dict(decay_w0=decay_w0[l], decay_lora=dlw, iclr_a0=iclr_a0[l], iclr_lora=ilw,
                  key_k=key_k[l].reshape(1, W_A), key_a=key_a[l], bonus_rk=bonus_rk[l].reshape(2, W_A),
                  head_mask=head_mask, gn_w=gn_w[l].reshape(1, W_A), gn_b=gn_b[l].reshape(1, W_A),
                  pool_w=pool_w[l].astype(BF16), pool_scale=pool_scale[l].reshape(1, W_B),
                  w_out_a=w_out_a[l].astype(BF16), w_out_b=w_out_b[l].astype(BF16))
        w_out_l = w_out[l].astype(BF16)
        g_l = norm_g[l].reshape(1, d)

        m = mod[l]
        shift_l, scale_l, gate_l = (m[:batch, j * d:(j + 1) * d].reshape(batch, 1, d) for j in range(3))
        shift_c, scale_c, gate_c = (m[batch:batch + 1, j * d:(j + 1) * d].reshape(1, 1, d) for j in range(3))

        u_c = _inproj_call(h_ctx.reshape(1, batch * ctx_len, d), shift_c, scale_c, g_l, w_perm)
        u_c = u_c.reshape(batch, ctx_len, nu)
        u_l = _inproj_call(h_lat, shift_l, scale_l, g_l, w_perm)

        vres_c = vres_l = None
        if l > 0:
            vb = _pad_rows(vres_lora_b[l - 1], LANES).astype(BF16)
            v0 = vres_v0[l - 1].reshape(1, W_A)
            vres_c = (vfirst['c'], off_vres // LANES, vb, v0)
            vres_l = (vfirst['l'], off_vres // LANES, vb, v0)
        rkv_c = _mix_call(u_c, tok_mu[l], False, vres_c)
        rkv_l = _mix_call(u_l, tok_mu[l], True, vres_l)
        if l == 0:
            vfirst = dict(c=rkv_c, l=rkv_l)

        outs_c = _scan_call(rkv_c, u_c, off_dl // GROUP_W, None, lp, not last)
        s_ctx = outs_c[-1]
        outs_l = _scan_call(rkv_l, u_l, off_dl // GROUP_W, s_ctx, lp, True)

        merged_l = _post1_call(outs_l[:4], u_l, cols, lp)
        h_lat = _post2_call(merged_l, h_lat, gate_l, w_out_l, final_g.reshape(1, d), last)
        if not last:
            merged_c = _post1_call(outs_c[:4], u_c, cols, lp)
            h_ctx = _post2_call(merged_c.reshape(1, batch * ctx_len, d), h_ctx.reshape(1, batch * ctx_len, d),
                                gate_c, w_out_l, final_g.reshape(1, d), False).reshape(batch, ctx_len, d)
    return h_lat
```

```python
import functools
import math

import jax
import jax.numpy as jnp
from jax import lax
from jax.experimental import pallas as pl
from jax.experimental.pallas import tpu as pltpu

F32 = jnp.float32
BF16 = jnp.bfloat16

GRID_W = 64
W_A = 1024
W_B = 1024
HEAD_DIM = 64
POOL_WINDOWS = (2, 4, 8, 16)
POOL_GROUP_W = W_B // len(POOL_WINDOWS)
R_DECAY = 64
R_ICLR = 64
R_VRES = 32
NORM_EPS = 1e-6
GN_EPS = HEAD_DIM * 1e-5
REF_OFF_POOL = 3 * W_A
REF_OFF_GATE_A = REF_OFF_POOL + W_B
REF_OFF_GATE_B = REF_OFF_GATE_A + W_A
REF_OFF_DECAY = REF_OFF_GATE_B + W_B
REF_OFF_ICLR = REF_OFF_DECAY + 2 * R_DECAY
REF_OFF_MERGE = REF_OFF_ICLR + 2 * R_ICLR

LANES = 128
MXU_DIM = 256
VMEM_LIMIT_BYTES = 56 * 1024 * 1024

GROUP_W = MXU_DIM
GROUP_HEADS = GROUP_W // HEAD_DIM
N_GROUPS = W_A // GROUP_W
SCAN_GROUPS = 2
SCAN_TB = 256
CHUNK = 64
HALO = 64
POOL_HALO = 8
LOG_DECAY_SCALE = math.exp(-0.5)


def _cparams(sem):
    return pltpu.CompilerParams(dimension_semantics=sem, vmem_limit_bytes=VMEM_LIMIT_BYTES)


def _sigmoid(x):
    return 1.0 / (1.0 + jnp.exp(-x))


def _dot(a, b):
    return jnp.dot(a, b, preferred_element_type=F32)


def _dot_nt(a, b):
    return lax.dot_general(a, b, (((1,), (1,)), ((), ())), preferred_element_type=F32)


def _dot_tn(a, b):
    return lax.dot_general(a, b, (((0,), (0,)), ((), ())), preferred_element_type=F32)


def _ada_kernel(c_ref, w_ref, b_ref, o_ref):
    c = c_ref[...]
    s = (c * _sigmoid(c)).astype(BF16)
    o_ref[0] = _dot(s, w_ref[0].astype(BF16)) + b_ref[0]


def _ada_call(c_all, ada_w, ada_b):
    n_layers, d, d3 = ada_w.shape
    rows = c_all.shape[0]
    tn = 768
    return pl.pallas_call(
        _ada_kernel,
        out_shape=jax.ShapeDtypeStruct((n_layers, rows, d3), F32),
        grid=(n_layers, d3 // tn),
        in_specs=[
            pl.BlockSpec((rows, d), lambda l, n: (0, 0)),
            pl.BlockSpec((1, d, tn), lambda l, n: (l, 0, n)),
            pl.BlockSpec((1, 1, tn), lambda l, n: (l, 0, n)),
        ],
        out_specs=pl.BlockSpec((1, rows, tn), lambda l, n: (l, 0, n)),
        compiler_params=_cparams(("arbitrary", "arbitrary")),
    )(c_all, ada_w, ada_b.reshape(n_layers, 1, d3))


def _inproj_kernel(h_ref, shift_ref, scale_ref, g_ref, w_ref, o_ref, xn_ref):
    @pl.when(pl.program_id(2) == 0)
    def _():
        x = h_ref[0]
        ms = jnp.mean(x * x, axis=-1, keepdims=True)
        y = x * lax.rsqrt(ms + NORM_EPS) * g_ref[...]
        xn_ref[...] = (y * (1.0 + scale_ref[0]) + shift_ref[0]).astype(BF16)

    o_ref[0] = _dot(xn_ref[...], w_ref[...])


def _inproj_call(h, shift, scale, g, w):
    bm, t, d = h.shape
    nu = w.shape[1]
    tm = min(1024, t)
    tn = 768
    return pl.pallas_call(
        _inproj_kernel,
        out_shape=jax.ShapeDtypeStruct((bm, t, nu), F32),
        grid=(bm, t // tm, nu // tn),
        in_specs=[
            pl.BlockSpec((1, tm, d), lambda b, m, n: (b, m, 0)),
            pl.BlockSpec((1, 1, d), lambda b, m, n: (b, 0, 0)),
            pl.BlockSpec((1, 1, d), lambda b, m, n: (b, 0, 0)),
            pl.BlockSpec((1, d), lambda b, m, n: (0, 0)),
            pl.BlockSpec((d, tn), lambda b, m, n: (0, n)),
        ],
        out_specs=pl.BlockSpec((1, tm, tn), lambda b, m, n: (b, m, n)),
        scratch_shapes=[pltpu.VMEM((tm, d), BF16)],
        compiler_params=_cparams(("arbitrary", "arbitrary", "arbitrary")),
    )(h, shift, scale, g, w)


def _shift_plan(grid_mode, hg):
    if grid_mode:
        return ((-1, 'first'), (1, 'last'), (-GRID_W, None), (GRID_W, None))[hg]
    return ((-1, None), (-1, None), (1, None), (1, None))[hg]


def _scan_kernel(*refs, names, tb, n_groups, grid_mode, has_vres, use_s0, emit_out, emit_v):
    R = dict(zip(names, refs))
    s_ref = R['s_scr']
    mix_ref = R['mix_scr']
    p = pl.program_id(1)
    i = pl.program_id(2)
    nt = pl.num_programs(2)
    n_steps_groups = N_GROUPS // n_groups

    @pl.when(i == 0)
    def _():
        if use_s0:
            s_ref[...] = R['s0'][:, 0]
        else:
            s_ref[...] = jnp.zeros_like(s_ref)

    not_first = jnp.where(i > 0, 1.0, 0.0).astype(F32)
    not_last = jnp.where(i < nt - 1, 1.0, 0.0).astype(F32)
    edge = ((not_first, not_last), (not_last, not_first))
    n_ext = tb + 2 * HALO
    trow1 = lax.broadcasted_iota(jnp.int32, (tb, 1), 0)
    gcol = jnp.bitwise_and(trow1, GRID_W - 1)
    col_mask = dict(first=jnp.where(gcol != 0, 1.0, 0.0).astype(F32),
                    last=jnp.where(gcol != GRID_W - 1, 1.0, 0.0).astype(F32))

    def emit_mix(pv):
        for d in (0, 1):
            ps, ns = edge[d]
            for g in range(n_groups):
                off, cmask = _shift_plan(grid_mode, pv * n_groups + g)
                cs = slice(g * GROUP_W, (g + 1) * GROUP_W)
                if has_vres:
                    lo = R[f'lo{d}'][0].astype(BF16)
                    gate = _sigmoid(R['vres_v0'][:, cs] + _dot(lo, R['vres_b'][:, cs]))
                for m, nm in enumerate('rkv'):
                    x = R[f'{nm}{d}c'][0, :, cs]
                    if off == -GRID_W:
                        sh = jnp.concatenate([R[f'{nm}{d}p'][0, :, cs] * ps, x[:tb - HALO]], axis=0)
                    elif off == GRID_W:
                        sh = jnp.concatenate([x[HALO:], R[f'{nm}{d}n'][0, :, cs] * ns], axis=0)
                    else:
                        ext = jnp.concatenate([R[f'{nm}{d}p'][0, :, cs] * ps, x, R[f'{nm}{d}n'][0, :, cs] * ns], axis=0)
                        sh = pltpu.roll(ext, 1 if off == -1 else n_ext - 1, axis=0)[HALO:HALO + tb]
                    if cmask is not None:
                        sh = sh * col_mask[cmask]
                    y = x + R['tok_mu'][m:m + 1, cs] * (sh - x)
                    if has_vres and m == 2:
                        y = y + (R[f'vf{d}'][0, :, cs] - y) * gate
                    mix_ref[d, m, :, cs] = y

    if n_steps_groups == 1:
        emit_mix(0)
    else:
        for pv in range(n_steps_groups):
            pl.when(p == pv)(functools.partial(emit_mix, pv))
    if emit_v:
        R['vmix'][0] = mix_ref[0, 2]

    bm = R['head_mask'][...]
    bmf = bm.astype(F32)

    def bd(x):
        xb = x.astype(BF16)
        return jnp.concatenate([xb] * GROUP_HEADS, axis=0) * bm

    def headsum(x):
        hi = x.astype(BF16)
        lo = (x - hi.astype(F32)).astype(BF16)
        return _dot(hi, bm) + _dot(lo, bm)

    def prod(a, b):
        return _dot(a.astype(BF16), bd(b))

    trow = lax.broadcasted_iota(jnp.int32, (CHUNK, GROUP_W), 0)
    scol = jnp.bitwise_and(lax.broadcasted_iota(jnp.int32, (CHUNK, GROUP_W), 1), CHUNK - 1)
    eye = jnp.where(trow == scol, 1.0, 0.0).astype(F32)
    same16 = (trow >> 4) == (scol >> 4)
    same32 = (trow >> 5) == (scol >> 5)
    m_d16 = jnp.where(same16, 1.0, 0.0).astype(F32)
    m_o32 = jnp.where(jnp.logical_and(same32, jnp.logical_not(same16)), 1.0, 0.0).astype(F32)
    m_o64 = jnp.where(same32, 0.0, 1.0).astype(F32)
    tri_masks = (
        (jnp.where(scol < trow, 1.0, 0.0).astype(F32), jnp.where(scol <= trow, 1.0, 0.0).astype(F32)),
        (jnp.where(scol > trow, 1.0, 0.0).astype(F32), jnp.where(scol >= trow, 1.0, 0.0).astype(F32)))
    brow = lax.broadcasted_iota(jnp.int32, (tb, tb), 0)
    bcol = lax.broadcasted_iota(jnp.int32, (tb, tb), 1)
    same_chunk = (brow >> 6) == (bcol >> 6)
    tris = (jnp.where(jnp.logical_and(same_chunk, bcol <= brow), 1.0, 0.0).astype(BF16),
            jnp.where(jnp.logical_and(same_chunk, bcol >= brow), 1.0, 0.0).astype(BF16))

    n_chunk = tb // CHUNK
    units = []
    for d in (0, 1):
        dl = R[f'dl{d}'][0]
        dlo = jnp.tanh(dl[:, :LANES]).astype(BF16)
        alo = dl[:, LANES:].astype(BF16)
        strict, incl = tri_masks[d]
        for g in range(n_groups):
            cs = slice(g * GROUP_W, (g + 1) * GROUP_W)
            r = mix_ref[d, 0, :, cs]
            k = mix_ref[d, 1, :, cs]
            v = mix_ref[d, 2, :, cs]
            z = R['decay_w0'][d:d + 1, cs] + _dot(dlo, R['decay_lora'][d, :, cs])
            lg = -LOG_DECAY_SCALE * _sigmoid(z)
            a = _sigmoid(R['iclr_a0'][d:d + 1, cs] + _dot(alo, R['iclr_lora'][d, :, cs]))
            kk = k * R['key_k'][:, cs]
            kk = kk / jnp.maximum(jnp.sqrt(headsum(kk * kk)), 1e-12)
            kd = k * (1.0 + (a - 1.0) * R['key_a'][d:d + 1, cs])
            bb = kk * a
            if emit_out:
                R[f'bo{d}'][0, :, cs] = headsum(r * kd * R['bonus_rk'][d:d + 1, cs]) * v
            g1 = lg.astype(BF16)
            g2 = (lg - g1.astype(F32)).astype(BF16)
            g3 = (lg - g1.astype(F32) - g2.astype(F32)).astype(BF16)
            cum = _dot(tris[d], g1) + _dot(tris[d], g2) + _dot(tris[d], g3)

            order = range(n_chunk) if d == 0 else range(n_chunk - 1, -1, -1)
            for c in order:
                sl = slice(c * CHUNK, (c + 1) * CHUNK)
                lc = cum[sl]
                ltot = lc[CHUNK - 1:CHUNK] if d == 0 else lc[0:1]
                e_neg = jnp.exp(-lc)
                e_tail = jnp.exp(ltot - lc)
                units.append(dict(
                    d=d, g=g, sl=sl, cs=cs, strict=strict, incl=incl, vc=v[sl], p_tot=jnp.exp(ltot),
                    r_h=r[sl] * jnp.exp(lc), kap_h=kk[sl] * jnp.exp(lc - lg[sl]),
                    k_h=kd[sl] * e_neg, b_h=bb[sl] * e_neg, k_t=kd[sl] * e_tail, b_t=bb[sl] * e_tail))

    for q in units:
        q['x2'] = jnp.concatenate([q['kap_h'], q['r_h']], axis=0).astype(BF16)
    for q in units:
        q['gb'] = _dot_nt(q['x2'], bd(q['b_h']))
    for q in units:
        q['gk'] = _dot_nt(q['x2'], bd(q['k_h']))
    for q in units:
        q['a_ab'] = q['gb'][:CHUNK] * q['strict']
        q['a_rb'] = q['gb'][CHUNK:] * q['incl']
        q['a_kk'] = jnp.concatenate([q['gk'][:CHUNK] * q['strict'], q['gk'][CHUNK:] * q['incl']], axis=0).astype(BF16)
        q['n1'] = -(q['a_ab'] * m_d16)
    for q in units:
        q['m1y'] = _dot(q['a_kk'], bd(q['vc']))

    for q in units:
        q['n2'] = prod(q['n1'], q['n1'])
    for q in units:
        q['n4'] = prod(q['n2'], q['n2'])
        q['x'] = eye + q['n1']
    for q in units:
        q['x'] = q['x'] + prod(q['x'], q['n2'])
    for q in units:
        q['n8'] = prod(q['n4'], q['n4'])
    for q in units:
        q['x'] = q['x'] + prod(q['x'], q['n4'])
    for q in units:
        q['x'] = q['x'] + prod(q['x'], q['n8'])
    for m_off in (m_o32, m_o64):
        for q in units:
            q['xo'] = prod(q['x'], q['a_ab'] * m_off)
        for q in units:
            q['x'] = q['x'] - prod(q['xo'], q['x'])

    for q in units:
        q['t16'] = q['x'].astype(BF16)
        q['w'] = _dot(q['t16'], bd(q['kap_h']))
    for q in units:
        q['ut'] = _dot(q['t16'], bd(q['m1y'][:CHUNK]))
    for q in units:
        q['xs'] = (_dot_tn(q['w'].astype(BF16), q['b_t'].astype(BF16)) * bmf).astype(BF16)
    for q in units:
        q['ns'] = _dot_tn(jnp.concatenate([q['vc'], -q['ut']], axis=0).astype(BF16),
                          jnp.concatenate([q['k_t'], q['b_t']], axis=0).astype(BF16)) * bmf

    chains = [[q for q in units if q['d'] == d and q['g'] == g] for d in (0, 1) for g in range(n_groups)]
    s_cur = [s_ref[ch[0]['d'], ch[0]['g']] for ch in chains]
    for ci in range(n_chunk):
        for j, ch in enumerate(chains):
            q = ch[ci]
            sb = s_cur[j].astype(BF16)
            q['s0'] = sb
            s_cur[j] = s_cur[j] * q['p_tot'] - _dot(sb, q['xs']) + q['ns']
    for j, ch in enumerate(chains):
        s_ref[ch[0]['d'], ch[0]['g']] = s_cur[j]

    if emit_out:
        for q in units:
            q['wr'] = _dot_nt(jnp.concatenate([q['w'], q['r_h']], axis=0).astype(BF16), q['s0'])
        for q in units:
            q['u'] = -(q['wr'][:CHUNK] + q['ut'])
        for q in units:
            R[f"y{q['d']}"][0, q['sl'], q['cs']] = q['wr'][CHUNK:] + prod(q['a_rb'], q['u']) + q['m1y'][CHUNK:]

    @pl.when(i == nt - 1)
    def _():
        R['sfin'][:, 0] = s_ref[...]


def _scan_call(u, dl_col, s0, lp, grid_mode, emit_out, vres=None, emit_v=False):
    b, t, _ = u.shape
    tb = min(SCAN_TB, t)
    nt = t // tb
    ng = SCAN_GROUPS
    gw = ng * GROUP_W
    npair = N_GROUPS // ng
    hb = tb // HALO
    nh = t // HALO
    use_s0 = s0 is not None
    has_vres = vres is not None
    assert GROUP_W == W_A // 4 and tb % HALO == 0 and tb % CHUNK == 0

    tblk = (lambda i: i, lambda i: nt - 1 - i)
    names, in_specs, args = [], [], []

    def add(name, spec, arr):
        names.append(name)
        in_specs.append(spec)
        args.append(arr)

    for d in (0, 1):
        for m, nm in enumerate('rkv'):
            add(f'{nm}{d}c', pl.BlockSpec((1, tb, gw), lambda bi, p, i, d=d, m=m: (bi, tblk[d](i), m * npair + p)), u)
            add(f'{nm}{d}p', pl.BlockSpec((1, HALO, gw), lambda bi, p, i, d=d, m=m:
                                          (bi, jnp.maximum(tblk[d](i) * hb - 1, 0), m * npair + p)), u)
            add(f'{nm}{d}n', pl.BlockSpec((1, HALO, gw), lambda bi, p, i, d=d, m=m:
                                          (bi, jnp.minimum((tblk[d](i) + 1) * hb, nh - 1), m * npair + p)), u)
        add(f'dl{d}', pl.BlockSpec((1, tb, GROUP_W), lambda bi, p, i, d=d: (bi, tblk[d](i), dl_col)), u)
        if has_vres:
            v_first, vres_col, vres_b, vres_v0 = vres
            add(f'vf{d}', pl.BlockSpec((1, tb, gw), lambda bi, p, i, d=d: (bi, tblk[d](i), p)), v_first)
            add(f'lo{d}', pl.BlockSpec((1, tb, LANES), lambda bi, p, i, d=d: (bi, tblk[d](i), vres_col)), u)
    state_spec = pl.BlockSpec((2, 1, ng, GROUP_W, GROUP_W), lambda bi, p, i: (0, bi, p, 0, 0))
    if use_s0:
        add('s0', state_spec, s0)
    if has_vres:
        add('vres_b', pl.BlockSpec((LANES, gw), lambda bi, p, i: (0, p)), vres_b)
        add('vres_v0', pl.BlockSpec((1, gw), lambda bi, p, i: (0, p)), vres_v0)
    vec = lambda rows: pl.BlockSpec((rows, gw), lambda bi, p, i: (0, p))
    lora = pl.BlockSpec((2, LANES, gw), lambda bi, p, i: (0, 0, p))
    add('tok_mu', vec(3), lp['tok_mu'])
    add('decay_w0', vec(2), lp['decay_w0'])
    add('decay_lora', lora, lp['decay_lora'])
    add('iclr_a0', vec(2), lp['iclr_a0'])
    add('iclr_lora', lora, lp['iclr_lora'])
    add('key_k', vec(1), lp['key_k'])
    add('key_a', vec(2), lp['key_a'])
    add('bonus_rk', vec(2), lp['bonus_rk'])
    add('head_mask', pl.BlockSpec((GROUP_W, GROUP_W), lambda bi, p, i: (0, 0)), lp['head_mask'])

    out_names, out_shape, out_specs = [], [], []
    if emit_out:
        for nm in ('y', 'bo'):
            for d in (0, 1):
                out_names.append(f'{nm}{d}')
                out_shape.append(jax.ShapeDtypeStruct((b, t, W_A), F32))
                out_specs.append(pl.BlockSpec((1, tb, gw), lambda bi, p, i, d=d: (bi, tblk[d](i), p)))
    if emit_v:
        out_names.append('vmix')
        out_shape.append(jax.ShapeDtypeStruct((b, t, W_A), F32))
        out_specs.append(pl.BlockSpec((1, tb, gw), lambda bi, p, i: (bi, i, p)))
    out_names.append('sfin')
    out_shape.append(jax.ShapeDtypeStruct((2, b, N_GROUPS, GROUP_W, GROUP_W), F32))
    out_specs.append(state_spec)

    all_names = tuple(names + out_names + ['s_scr', 'mix_scr'])
    outs = pl.pallas_call(
        functools.partial(_scan_kernel, names=all_names, tb=tb, n_groups=ng, grid_mode=grid_mode,
                          has_vres=has_vres, use_s0=use_s0, emit_out=emit_out, emit_v=emit_v),
        out_shape=out_shape,
        grid=(b, npair, nt),
        in_specs=in_specs,
        out_specs=out_specs,
        scratch_shapes=[pltpu.VMEM((2, ng, GROUP_W, GROUP_W), F32), pltpu.VMEM((2, 3, tb, gw), F32)],
        compiler_params=_cparams(("arbitrary", "arbitrary", "arbitrary")),
    )(*args)
    return dict(zip(out_names, outs))


def _post1_kernel(y0, y1, b0, b1, ga, gb, pc, pp, pn, mg0, mg1, gnw, gnb, pw, ps, woa, wob, bm_ref, o_ref, *, tm, t_total):
    i = pl.program_id(1)
    nt = pl.num_programs(1)
    bm = bm_ref[...]

    def headsum(x):
        hi = x.astype(BF16)
        lo = (x - hi.astype(F32)).astype(BF16)
        return _dot(hi, bm) + _dot(lo, bm)

    parts = []
    for hg in range(N_GROUPS):
        cs = slice(hg * GROUP_W, (hg + 1) * GROUP_W)
        y = y0[0, :, cs] + y1[0, :, cs]
        mu = headsum(y) * (1.0 / HEAD_DIM)
        yc = y - mu
        var = headsum(yc * yc) * (1.0 / HEAD_DIM)
        yn = yc * lax.rsqrt(var + GN_EPS) * gnw[:, cs] + gnb[:, cs]
        yn = yn + b0[0, :, cs] + b1[0, :, cs]
        gate = ga[0, :, cs]
        parts.append((yn * (gate * _sigmoid(gate))).astype(BF16))
    y_a = _dot(jnp.concatenate(parts, axis=1), woa[...])

    pscale = jnp.where(i > 0, 1.0, 0.0).astype(F32)
    nscale = jnp.where(i < nt - 1, 1.0, 0.0).astype(F32)
    n_ext = tm + 2 * POOL_HALO
    tpos = (lax.broadcasted_iota(jnp.int32, (tm, 1), 0) + i * tm).astype(F32)
    parts = []
    for gi, win in enumerate(POOL_WINDOWS):
        left = win // 2
        right = win - 1 - left
        cs = slice(gi * POOL_GROUP_W, (gi + 1) * POOL_GROUP_W)
        p = pc[0, :, cs]
        ext = jnp.concatenate([pp[0, :, cs] * pscale, p, pn[0, :, cs] * nscale], axis=0)
        s = ext
        step = 1
        while step < win:
            s = s + pltpu.roll(s, step, axis=0)
            step *= 2
        if right > 0:
            s = pltpu.roll(s, n_ext - right, axis=0)
        s = s[POOL_HALO:POOL_HALO + tm]
        cnt = jnp.minimum(tpos + (right + 1), float(t_total)) - jnp.maximum(tpos - left, 0.0)
        pooled = s / cnt - p
        mixed = _dot(pooled.astype(BF16), pw[gi]) * ps[:, cs]
        gate = gb[0, :, cs]
        parts.append((mixed * (gate * _sigmoid(gate))).astype(BF16))
    y_b = _dot(jnp.concatenate(parts, axis=1), wob[...])

    merged = _sigmoid(mg0[0]) * y_a + _sigmoid(mg1[0]) * y_b
    o_ref[0] = merged.astype(BF16)


def _post1_call(sc, u, cols, lp):
    y0, y1, b0, b1 = sc['y0'], sc['y1'], sc['bo0'], sc['bo1']
    b, t, _ = y0.shape
    d = lp['w_out_a'].shape[1]
    tm = min(256, t)
    nt = t // tm
    hb = tm // POOL_HALO
    nh = t // POOL_HALO
    blk = lambda c: pl.BlockSpec((1, tm, W_A), lambda bi, i: (bi, i, c))
    wide = lambda c: pl.BlockSpec((1, tm, d), lambda bi, i: (bi, i, c))
    const2 = lambda shape: pl.BlockSpec(shape, lambda bi, i: (0, 0))
    pool_col = cols['pool']
    in_specs = [blk(0), blk(0), blk(0), blk(0),
                blk(cols['gate_a']), blk(cols['gate_b']), blk(pool_col),
                pl.BlockSpec((1, POOL_HALO, W_B), lambda bi, i: (bi, jnp.maximum(i * hb - 1, 0), pool_col)),
                pl.BlockSpec((1, POOL_HALO, W_B), lambda bi, i: (bi, jnp.minimum((i + 1) * hb, nh - 1), pool_col)),
                wide(cols['merge0']), wide(cols['merge1']),
                const2((1, W_A)), const2((1, W_A)),
                pl.BlockSpec((len(POOL_WINDOWS), POOL_GROUP_W, POOL_GROUP_W), lambda bi, i: (0, 0, 0)),
                const2((1, W_B)), const2((W_A, d)), const2((W_B, d)), const2((GROUP_W, GROUP_W))]
    args = [y0, y1, b0, b1, u, u, u, u, u, u, u,
            lp['gn_w'], lp['gn_b'], lp['pool_w'], lp['pool_scale'], lp['w_out_a'], lp['w_out_b'], lp['head_mask']]
    return pl.pallas_call(
        functools.partial(_post1_kernel, tm=tm, t_total=t),
        out_shape=jax.ShapeDtypeStruct((b, t, d), BF16),
        grid=(b, nt),
        in_specs=in_specs,
        out_specs=pl.BlockSpec((1, tm, d), lambda bi, i: (bi, i, 0)),
        compiler_params=_cparams(("arbitrary", "arbitrary")),
    )(*args)


def _post2_kernel(m_ref, h_ref, gate_ref, w_ref, fg_ref, o_ref, *, final_norm):
    out = _dot(m_ref[0], w_ref[...])
    h = h_ref[0] + gate_ref[0] * out
    if final_norm:
        ms = jnp.mean(h * h, axis=-1, keepdims=True)
        h = h * lax.rsqrt(ms + NORM_EPS) * fg_ref[...]
    o_ref[0] = h


def _post2_call(merged, h, gate, w_out, final_g, final_norm):
    bm, t, d = h.shape
    tm = min(512, t)
    return pl.pallas_call(
        functools.partial(_post2_kernel, final_norm=final_norm),
        out_shape=jax.ShapeDtypeStruct((bm, t, d), F32),
        grid=(bm, t // tm),
        in_specs=[
            pl.BlockSpec((1, tm, d), lambda b, m: (b, m, 0)),
            pl.BlockSpec((1, tm, d), lambda b, m: (b, m, 0)),
            pl.BlockSpec((1, 1, d), lambda b, m: (b, 0, 0)),
            pl.BlockSpec((d, d), lambda b, m: (0, 0)),
            pl.BlockSpec((1, d), lambda b, m: (0, 0)),
        ],
        out_specs=pl.BlockSpec((1, tm, d), lambda b, m: (b, m, 0)),
        compiler_params=_cparams(("arbitrary", "arbitrary")),
    )(merged, h, gate, w_out, final_g)


def _pad_rows(w, rows):
    return jnp.pad(w, ((0, 0),) * (w.ndim - 2) + ((0, rows - w.shape[-2]), (0, 0)))


def kernel(x, c, ctx, c_ctx, ada_w, ada_b, norm_g, w_in, tok_mu, decay_w0, decay_lora_b, iclr_a0, iclr_lora_b, key_k, key_a, bonus_rk, gn_w, gn_b, vres_v0, vres_lora_a, vres_lora_b, pool_w, pool_scale, w_out_a, w_out_b, w_out, final_g):
    depth = w_in.shape[0]
    batch, seq, d = x.shape
    ctx_len = ctx.shape[1]
    assert seq % GRID_W == 0 and seq % CHUNK == 0 and ctx_len % CHUNK == 0
    assert d % LANES == 0

    off_merge = 3 * W_A + W_B + W_A + W_B
    off_dl = off_merge + 2 * d
    off_vres = off_dl + 2 * R_DECAY + 2 * R_ICLR
    nu = -(-(off_vres + LANES) // 768) * 768
    cols = dict(pool=REF_OFF_POOL // W_B, gate_a=REF_OFF_GATE_A // W_A, gate_b=REF_OFF_GATE_B // W_A,
                merge0=off_merge // d, merge1=off_merge // d + 1)
    assert off_merge % d == 0 and off_dl % GROUP_W == 0 and off_vres % LANES == 0

    head_mask = (jnp.arange(GROUP_W)[:, None] // HEAD_DIM == jnp.arange(GROUP_W)[None, :] // HEAD_DIM).astype(BF16)

    rows = -(-(batch + 1) // 8) * 8
    c_all = jnp.zeros((rows, d), F32).at[:batch].set(c).at[batch].set(c_ctx)
    mod = _ada_call(c_all, ada_w, ada_b)

    h_lat, h_ctx = x, ctx
    vfirst = {}
    for l in range(depth):
        last = l == depth - 1
        w_l = w_in[l]
        pieces = [w_l[:, :REF_OFF_DECAY], w_l[:, REF_OFF_MERGE:REF_OFF_MERGE + 2 * d],
                  w_l[:, REF_OFF_DECAY:REF_OFF_MERGE]]
        if l > 0:
            pieces.append(vres_lora_a[l - 1])
        w_perm = jnp.concatenate(pieces, axis=1)
        w_perm = jnp.pad(w_perm, ((0, 0), (0, nu - w_perm.shape[1]))).astype(BF16)

        dlw = decay_lora_b[l]
        dlw = jnp.stack([jnp.pad(dlw[0], ((0, LANES - R_DECAY), (0, 0))),
                         jnp.pad(dlw[1], ((R_DECAY, LANES - 2 * R_DECAY), (0, 0)))]).astype(BF16)
        ilw = iclr_lora_b[l]
        ilw = jnp.stack([jnp.pad(ilw[0], ((0, LANES - R_ICLR), (0, 0))),
                         jnp.pad(ilw[1], ((R_ICLR, LANES - 2 * R_ICLR), (0, 0)))]).astype(BF16)
        lp = dict(tok_mu=tok_mu[l], decay_w0=decay_w0[l], decay_lora=dlw, iclr_a0=iclr_a0[l], iclr_lora=ilw,
                  key_k=key_k[l].reshape(1, W_A), key_a=key_a[l], bonus_rk=bonus_rk[l].reshape(2, W_A),
                  head_mask=head_mask, gn_w=gn_w[l].reshape(1, W_A), gn_b=gn_b[l].reshape(1, W_A),
                  pool_w=pool_w[l].astype(BF16), pool_scale=pool_scale[l].reshape(1, W_B),
                  w_out_a=w_out_a[l].astype(BF16), w_out_b=w_out_b[l].astype(BF16))
        w_out_l = w_out[l].astype(BF16)
        g_l = norm_g[l].reshape(1, d)

        m = mod[l]
        shift_l, scale_l, gate_l = (m[:batch, j * d:(j + 1) * d].reshape(batch, 1, d) for j in range(3))
        shift_c, scale_c, gate_c = (m[batch:batch + 1, j * d:(j + 1) * d].reshape(1, 1, d) for j in range(3))

        u_c = _inproj_call(h_ctx.reshape(1, batch * ctx_len, d), shift_c, scale_c, g_l, w_perm)
        u_c = u_c.reshape(batch, ctx_len, nu)
        u_l = _inproj_call(h_lat, shift_l, scale_l, g_l, w_perm)

        vres_c = vres_l = None
        if l > 0:
            vb = _pad_rows(vres_lora_b[l - 1], LANES).astype(BF16)
            v0 = vres_v0[l - 1].reshape(1, W_A)
            vres_c = (vfirst['c'], off_vres // LANES, vb, v0)
            vres_l = (vfirst['l'], off_vres // LANES, vb, v0)
        keep_v = l == 0 and depth > 1
        sc_c = _scan_call(u_c, off_dl // GROUP_W, None, lp, False, not last, vres_c, keep_v)
        sc_l = _scan_call(u_l, off_dl // GROUP_W, sc_c['sfin'], lp, True, True, vres_l, keep_v)
        if keep_v:
            vfirst = dict(c=sc_c['vmix'], l=sc_l['vmix'])

        merged_l = _post1_call(sc_l, u_l, cols, lp)
        h_lat = _post2_call(merged_l, h_lat, gate_l, w_out_l, final_g.reshape(1, d), last)
        if not last:
            merged_c = _post1_call(sc_c, u_c, cols, lp)
            h_ctx = _post2_call(merged_c.reshape(1, batch * ctx_len, d), h_ctx.reshape(1, batch * ctx_len, d),
                                gate_c, w_out_l, final_g.reshape(1, d), False).reshape(batch, ctx_len, d)
    return h_lat
```

```python
import functools
import math

import jax
import jax.numpy as jnp
from jax import lax
from jax.experimental import pallas as pl
from jax.experimental.pallas import tpu as pltpu

F32 = jnp.float32
BF16 = jnp.bfloat16

GRID_W = 64
W_A = 1024
W_B = 1024
HEAD_DIM = 64
POOL_WINDOWS = (2, 4, 8, 16)
POOL_GROUP_W = W_B // len(POOL_WINDOWS)
R_DECAY = 64
R_ICLR = 64
R_VRES = 32
NORM_EPS = 1e-6
GN_EPS = HEAD_DIM * 1e-5
REF_OFF_POOL = 3 * W_A
REF_OFF_GATE_A = REF_OFF_POOL + W_B
REF_OFF_GATE_B = REF_OFF_GATE_A + W_A
REF_OFF_DECAY = REF_OFF_GATE_B + W_B
REF_OFF_ICLR = REF_OFF_DECAY + 2 * R_DECAY
REF_OFF_MERGE = REF_OFF_ICLR + 2 * R_ICLR

LANES = 128
MXU_DIM = 256
VMEM_LIMIT_BYTES = 56 * 1024 * 1024

GROUP_W = MXU_DIM
GROUP_HEADS = GROUP_W // HEAD_DIM
N_GROUPS = W_A // GROUP_W
SCAN_GROUPS = 2
SCAN_TB = 256
CHUNK = 64
HALO = 64
POOL_HALO = 8
INPROJ_TN = 1536
LOG_DECAY_SCALE = math.exp(-0.5)


def _cparams(sem):
    return pltpu.CompilerParams(dimension_semantics=sem, vmem_limit_bytes=VMEM_LIMIT_BYTES)


def _sigmoid(x):
    return 1.0 / (1.0 + jnp.exp(-x))


def _dot(a, b):
    return jnp.dot(a, b, preferred_element_type=F32)


def _dot_nt(a, b):
    return lax.dot_general(a, b, (((1,), (1,)), ((), ())), preferred_element_type=F32)


def _dot_tn(a, b):
    return lax.dot_general(a, b, (((0,), (0,)), ((), ())), preferred_element_type=F32)


def _ada_kernel(c_ref, w_ref, b_ref, o_ref):
    c = c_ref[...]
    s = (c * _sigmoid(c)).astype(BF16)
    o_ref[0] = _dot(s, w_ref[0].astype(BF16)) + b_ref[0]


def _ada_call(c_all, ada_w, ada_b):
    n_layers, d, d3 = ada_w.shape
    rows = c_all.shape[0]
    tn = 768
    return pl.pallas_call(
        _ada_kernel,
        out_shape=jax.ShapeDtypeStruct((n_layers, rows, d3), F32),
        grid=(n_layers, d3 // tn),
        in_specs=[
            pl.BlockSpec((rows, d), lambda l, n: (0, 0)),
            pl.BlockSpec((1, d, tn), lambda l, n: (l, 0, n)),
            pl.BlockSpec((1, 1, tn), lambda l, n: (l, 0, n)),
        ],
        out_specs=pl.BlockSpec((1, rows, tn), lambda l, n: (l, 0, n)),
        compiler_params=_cparams(("arbitrary", "arbitrary")),
    )(c_all, ada_w, ada_b.reshape(n_layers, 1, d3))


def _inproj_kernel(h_ref, shift_ref, scale_ref, g_ref, w_ref, o_ref, xn_ref):
    @pl.when(pl.program_id(2) == 0)
    def _():
        x = h_ref[0]
        ms = jnp.mean(x * x, axis=-1, keepdims=True)
        y = x * lax.rsqrt(ms + NORM_EPS) * g_ref[...]
        xn_ref[...] = (y * (1.0 + scale_ref[0]) + shift_ref[0]).astype(BF16)

    o_ref[0] = _dot(xn_ref[...], w_ref[...])


def _inproj_call(h, shift, scale, g, w):
    bm, t, d = h.shape
    nu = w.shape[1]
    tm = min(1024, t)
    tn = INPROJ_TN
    assert nu % tn == 0
    return pl.pallas_call(
        _inproj_kernel,
        out_shape=jax.ShapeDtypeStruct((bm, t, nu), F32),
        grid=(bm, t // tm, nu // tn),
        in_specs=[
            pl.BlockSpec((1, tm, d), lambda b, m, n: (b, m, 0)),
            pl.BlockSpec((1, 1, d), lambda b, m, n: (b, 0, 0)),
            pl.BlockSpec((1, 1, d), lambda b, m, n: (b, 0, 0)),
            pl.BlockSpec((1, d), lambda b, m, n: (0, 0)),
            pl.BlockSpec((d, tn), lambda b, m, n: (0, n)),
        ],
        out_specs=pl.BlockSpec((1, tm, tn), lambda b, m, n: (b, m, n)),
        scratch_shapes=[pltpu.VMEM((tm, d), BF16)],
        compiler_params=_cparams(("arbitrary", "arbitrary", "arbitrary")),
    )(h, shift, scale, g, w)


def _shift_plan(grid_mode, hg):
    if grid_mode:
        return ((-1, 'first'), (1, 'last'), (-GRID_W, None), (GRID_W, None))[hg]
    return ((-1, None), (-1, None), (1, None), (1, None))[hg]


def _scan_kernel(*refs, names, tb, n_groups, grid_mode, has_vres, use_s0, emit_out, emit_v):
    R = dict(zip(names, refs))
    s_ref = R['s_scr']
    mix_ref = R['mix_scr']
    p = pl.program_id(1)
    i = pl.program_id(2)
    nt = pl.num_programs(2)
    n_steps_groups = N_GROUPS // n_groups

    @pl.when(i == 0)
    def _():
        if use_s0:
            s_ref[...] = R['s0'][:, 0]
        else:
            s_ref[...] = jnp.zeros_like(s_ref)

    not_first = jnp.where(i > 0, 1.0, 0.0).astype(F32)
    not_last = jnp.where(i < nt - 1, 1.0, 0.0).astype(F32)
    edge = ((not_first, not_last), (not_last, not_first))
    n_ext = tb + 2 * HALO
    trow1 = lax.broadcasted_iota(jnp.int32, (tb, 1), 0)
    gcol = jnp.bitwise_and(trow1, GRID_W - 1)
    col_mask = dict(first=jnp.where(gcol != 0, 1.0, 0.0).astype(F32),
                    last=jnp.where(gcol != GRID_W - 1, 1.0, 0.0).astype(F32))

    def emit_mix(pv):
        for d in (0, 1):
            ps, ns = edge[d]
            for g in range(n_groups):
                off, cmask = _shift_plan(grid_mode, pv * n_groups + g)
                cs = slice(g * GROUP_W, (g + 1) * GROUP_W)
                if has_vres:
                    lo = R[f'lo{d}'][0].astype(BF16)
                    gate = _sigmoid(R['vres_v0'][:, cs] + _dot(lo, R['vres_b'][:, cs]))
                for m, nm in enumerate('rkv'):
                    x = R[f'{nm}{d}c'][0, :, cs]
                    if off == -GRID_W:
                        sh = jnp.concatenate([R[f'{nm}{d}p'][0, :, cs] * ps, x[:tb - HALO]], axis=0)
                    elif off == GRID_W:
                        sh = jnp.concatenate([x[HALO:], R[f'{nm}{d}n'][0, :, cs] * ns], axis=0)
                    else:
                        ext = jnp.concatenate([R[f'{nm}{d}p'][0, :, cs] * ps, x, R[f'{nm}{d}n'][0, :, cs] * ns], axis=0)
                        sh = pltpu.roll(ext, 1 if off == -1 else n_ext - 1, axis=0)[HALO:HALO + tb]
                    if cmask is not None:
                        sh = sh * col_mask[cmask]
                    y = x + R['tok_mu'][m:m + 1, cs] * (sh - x)
                    if has_vres and m == 2:
                        y = y + (R[f'vf{d}'][0, :, cs] - y) * gate
                    mix_ref[d, m, :, cs] = y

    if n_steps_groups == 1:
        emit_mix(0)
    else:
        for pv in range(n_steps_groups):
            pl.when(p == pv)(functools.partial(emit_mix, pv))
    if emit_v:
        R['vmix'][0] = mix_ref[0, 2]

    bm = R['head_mask'][...]
    lane1 = lax.broadcasted_iota(jnp.int32, (1, LANES), 1)
    half_f = (jnp.where(lane1 < HEAD_DIM, 1.0, 0.0).astype(F32), jnp.where(lane1 >= HEAD_DIM, 1.0, 0.0).astype(F32))
    half_b = (half_f[0].astype(BF16), half_f[1].astype(BF16))
    zero_tile = jnp.zeros((HEAD_DIM, LANES), BF16)
    n_tiles = GROUP_W // LANES

    def bd(x):
        xb = x.astype(BF16)
        rows = []
        for j in range(GROUP_HEADS):
            t = (j * HEAD_DIM) // LANES
            piece = xb[:, t * LANES:(t + 1) * LANES] * half_b[(j * HEAD_DIM % LANES) // HEAD_DIM]
            rows.append(jnp.concatenate([piece if tt == t else zero_tile for tt in range(n_tiles)], axis=1))
        return jnp.concatenate(rows, axis=0)

    def head_blocks(full):
        tiles = []
        for t in range(n_tiles):
            acc = None
            for j in range(GROUP_HEADS):
                if (j * HEAD_DIM) // LANES == t:
                    part = (full[j * HEAD_DIM:(j + 1) * HEAD_DIM, t * LANES:(t + 1) * LANES]
                            * half_f[(j * HEAD_DIM % LANES) // HEAD_DIM])
                    acc = part if acc is None else acc + part
            tiles.append(acc)
        return jnp.concatenate(tiles, axis=1)

    def headsums(xs):
        rows = xs[0].shape[0]
        parts = []
        for x in xs:
            hi = x.astype(BF16)
            parts += [hi, (x - hi.astype(F32)).astype(BF16)]
        s = _dot(jnp.concatenate(parts, axis=0), bm)
        return [s[2 * n * rows:(2 * n + 1) * rows] + s[(2 * n + 1) * rows:(2 * n + 2) * rows] for n in range(len(xs))]

    def prod(a, b):
        return _dot(a.astype(BF16), bd(b))

    trow = lax.broadcasted_iota(jnp.int32, (CHUNK, GROUP_W), 0)
    scol = jnp.bitwise_and(lax.broadcasted_iota(jnp.int32, (CHUNK, GROUP_W), 1), CHUNK - 1)
    eye = jnp.where(trow == scol, 1.0, 0.0).astype(F32)
    same16 = (trow >> 4) == (scol >> 4)
    same32 = (trow >> 5) == (scol >> 5)
    tri_masks = []
    for before, upto in ((scol < trow, scol <= trow), (scol > trow, scol >= trow)):
        tri_masks.append(dict(
            neg_d16=jnp.where(jnp.logical_and(before, same16), -1.0, 0.0).astype(F32),
            o32=jnp.where(jnp.logical_and(before, jnp.logical_and(same32, jnp.logical_not(same16))), 1.0, 0.0).astype(BF16),
            o64=jnp.where(jnp.logical_and(before, jnp.logical_not(same32)), 1.0, 0.0).astype(BF16),
            strict=jnp.where(before, 1.0, 0.0).astype(F32),
            incl=jnp.where(upto, 1.0, 0.0).astype(F32)))
    brow = lax.broadcasted_iota(jnp.int32, (tb, tb), 0)
    bcol = lax.broadcasted_iota(jnp.int32, (tb, tb), 1)
    same_chunk = (brow >> 6) == (bcol >> 6)
    tris = (jnp.where(jnp.logical_and(same_chunk, bcol <= brow), 1.0, 0.0).astype(BF16),
            jnp.where(jnp.logical_and(same_chunk, bcol >= brow), 1.0, 0.0).astype(BF16))

    n_chunk = tb // CHUNK
    units = []
    for d in (0, 1):
        dl = R[f'dl{d}'][0]
        dlo = jnp.tanh(dl[:, :LANES]).astype(BF16)
        alo = dl[:, LANES:].astype(BF16)
        for g in range(n_groups):
            cs = slice(g * GROUP_W, (g + 1) * GROUP_W)
            r = mix_ref[d, 0, :, cs]
            k = mix_ref[d, 1, :, cs]
            v = mix_ref[d, 2, :, cs]
            z = R['decay_w0'][d:d + 1, cs] + _dot(dlo, R['decay_lora'][d, :, cs])
            lg = -LOG_DECAY_SCALE * _sigmoid(z)
            a = _sigmoid(R['iclr_a0'][d:d + 1, cs] + _dot(alo, R['iclr_lora'][d, :, cs]))
            kk = k * R['key_k'][:, cs]
            kd = k * (1.0 + (a - 1.0) * R['key_a'][d:d + 1, cs])
            if emit_out:
                kk_ss, rk_sum = headsums([kk * kk, r * kd * R['bonus_rk'][d:d + 1, cs]])
                R[f'bo{d}'][0, :, cs] = rk_sum * v
            else:
                kk_ss, = headsums([kk * kk])
            kk = kk / jnp.maximum(jnp.sqrt(kk_ss), 1e-12)
            bb = kk * a
            g1 = lg.astype(BF16)
            g2 = (lg - g1.astype(F32)).astype(BF16)
            g3 = (lg - g1.astype(F32) - g2.astype(F32)).astype(BF16)
            cum = _dot(tris[d], g1) + _dot(tris[d], g2) + _dot(tris[d], g3)

            order = range(n_chunk) if d == 0 else range(n_chunk - 1, -1, -1)
            for c in order:
                sl = slice(c * CHUNK, (c + 1) * CHUNK)
                lc = cum[sl]
                ltot = lc[CHUNK - 1:CHUNK] if d == 0 else lc[0:1]
                e_neg = jnp.exp(-lc)
                e_tail = jnp.exp(ltot - lc)
                units.append(dict(
                    d=d, g=g, sl=sl, cs=cs, masks=tri_masks[d], vc=v[sl].astype(BF16), p_tot=jnp.exp(ltot),
                    r_h=(r[sl] * jnp.exp(lc)).astype(BF16), kap_h=(kk[sl] * jnp.exp(lc - lg[sl])).astype(BF16),
                    k_h=(kd[sl] * e_neg).astype(BF16), b_h=(bb[sl] * e_neg).astype(BF16),
                    k_t=(kd[sl] * e_tail).astype(BF16), b_t=(bb[sl] * e_tail).astype(BF16)))

    for q in units:
        q['x2'] = jnp.concatenate([q['kap_h'], q['r_h']], axis=0)
    for q in units:
        q['gb'] = _dot_nt(q['x2'], bd(q['b_h']))
    for q in units:
        q['gk'] = _dot_nt(q['x2'], bd(q['k_h']))
    for q in units:
        mk = q['masks']
        ab = q['gb'][:CHUNK]
        abb = ab.astype(BF16)
        q['n1'] = ab * mk['neg_d16']
        q['n1b'] = q['n1'].astype(BF16)
        q['o32'] = abb * mk['o32']
        q['o64'] = abb * mk['o64']
        q['a_rb'] = (q['gb'][CHUNK:] * mk['incl']).astype(BF16)
        q['a_kk'] = jnp.concatenate([q['gk'][:CHUNK] * mk['strict'], q['gk'][CHUNK:] * mk['incl']], axis=0).astype(BF16)
    for q in units:
        q['m1y'] = _dot(q['a_kk'], bd(q['vc']))

    for q in units:
        q['nk'] = prod(q['n1b'], q['n1b']).astype(BF16)
        q['x'] = eye + q['n1']
    for _ in range(2):
        for q in units:
            both = _dot(jnp.concatenate([q['nk'], q['x'].astype(BF16)], axis=0), bd(q['nk']))
            q['nk'] = both[:CHUNK].astype(BF16)
            q['x'] = q['x'] + both[CHUNK:]
    for q in units:
        q['x'] = q['x'] + prod(q['x'], q['nk'])
    for off_key in ('o32', 'o64'):
        for q in units:
            q['xb'] = q['x'].astype(BF16)
            q['xo'] = prod(q['xb'], q[off_key]).astype(BF16)
        for q in units:
            q['x'] = q['x'] - prod(q['xo'], q['xb'])

    for q in units:
        q['t16'] = q['x'].astype(BF16)
        q['w'] = _dot(q['t16'], bd(q['kap_h'])).astype(BF16)
    for q in units:
        q['ut'] = _dot(q['t16'], bd(q['m1y'][:CHUNK]))
    for q in units:
        q['xs'] = bd(head_blocks(_dot_tn(q['w'], q['b_t'])))
    for q in units:
        q['ns'] = head_blocks(_dot_tn(jnp.concatenate([q['vc'], (-q['ut']).astype(BF16)], axis=0),
                                      jnp.concatenate([q['k_t'], q['b_t']], axis=0)))

    chains = [[q for q in units if q['d'] == d and q['g'] == g] for d in (0, 1) for g in range(n_groups)]
    s_cur = [s_ref[ch[0]['d'], ch[0]['g']] for ch in chains]
    for ci in range(n_chunk):
        for j, ch in enumerate(chains):
            q = ch[ci]
            sb = s_cur[j].astype(BF16)
            q['s0'] = sb
            s_cur[j] = s_cur[j] * q['p_tot'] - _dot(sb, q['xs']) + q['ns']
    for j, ch in enumerate(chains):
        s_ref[ch[0]['d'], ch[0]['g']] = s_cur[j]

    if emit_out:
        for q in units:
            q['wr'] = _dot_nt(jnp.concatenate([q['w'], q['r_h']], axis=0), bd(q['s0']))
        for q in units:
            q['u'] = -(q['wr'][:CHUNK] + q['ut'])
        for q in units:
            R[f"y{q['d']}"][0, q['sl'], q['cs']] = q['wr'][CHUNK:] + prod(q['a_rb'], q['u']) + q['m1y'][CHUNK:]

    @pl.when(i == nt - 1)
    def _():
        R['sfin'][:, 0] = s_ref[...]


def _scan_call(u, dl_col, s0, lp, grid_mode, emit_out, vres=None, emit_v=False):
    b, t, _ = u.shape
    tb = min(SCAN_TB, t)
    nt = t // tb
    ng = SCAN_GROUPS
    gw = ng * GROUP_W
    npair = N_GROUPS // ng
    hb = tb // HALO
    nh = t // HALO
    use_s0 = s0 is not None
    has_vres = vres is not None
    assert GROUP_W == W_A // 4 and tb % HALO == 0 and tb % CHUNK == 0

    tblk = (lambda i: i, lambda i: nt - 1 - i)
    names, in_specs, args = [], [], []

    def add(name, spec, arr):
        names.append(name)
        in_specs.append(spec)
        args.append(arr)

    for d in (0, 1):
        for m, nm in enumerate('rkv'):
            add(f'{nm}{d}c', pl.BlockSpec((1, tb, gw), lambda bi, p, i, d=d, m=m: (bi, tblk[d](i), m * npair + p)), u)
            add(f'{nm}{d}p', pl.BlockSpec((1, HALO, gw), lambda bi, p, i, d=d, m=m:
                                          (bi, jnp.maximum(tblk[d](i) * hb - 1, 0), m * npair + p)), u)
            add(f'{nm}{d}n', pl.BlockSpec((1, HALO, gw), lambda bi, p, i, d=d, m=m:
                                          (bi, jnp.minimum((tblk[d](i) + 1) * hb, nh - 1), m * npair + p)), u)
        add(f'dl{d}', pl.BlockSpec((1, tb, GROUP_W), lambda bi, p, i, d=d: (bi, tblk[d](i), dl_col)), u)
        if has_vres:
            v_first, vres_col, vres_b, vres_v0 = vres
            add(f'vf{d}', pl.BlockSpec((1, tb, gw), lambda bi, p, i, d=d: (bi, tblk[d](i), p)), v_first)
            add(f'lo{d}', pl.BlockSpec((1, tb, LANES), lambda bi, p, i, d=d: (bi, tblk[d](i), vres_col)), u)
    state_spec = pl.BlockSpec((2, 1, ng, HEAD_DIM, GROUP_W), lambda bi, p, i: (0, bi, p, 0, 0))
    if use_s0:
        add('s0', state_spec, s0)
    if has_vres:
        add('vres_b', pl.BlockSpec((LANES, gw), lambda bi, p, i: (0, p)), vres_b)
        add('vres_v0', pl.BlockSpec((1, gw), lambda bi, p, i: (0, p)), vres_v0)
    vec = lambda rows: pl.BlockSpec((rows, gw), lambda bi, p, i: (0, p))
    lora = pl.BlockSpec((2, LANES, gw), lambda bi, p, i: (0, 0, p))
    add('tok_mu', vec(3), lp['tok_mu'])
    add('decay_w0', vec(2), lp['decay_w0'])
    add('decay_lora', lora, lp['decay_lora'])
    add('iclr_a0', vec(2), lp['iclr_a0'])
    add('iclr_lora', lora, lp['iclr_lora'])
    add('key_k', vec(1), lp['key_k'])
    add('key_a', vec(2), lp['key_a'])
    add('bonus_rk', vec(2), lp['bonus_rk'])
    add('head_mask', pl.BlockSpec((GROUP_W, GROUP_W), lambda bi, p, i: (0, 0)), lp['head_mask'])

    out_names, out_shape, out_specs = [], [], []
    if emit_out:
        for nm in ('y', 'bo'):
            for d in (0, 1):
                out_names.append(f'{nm}{d}')
                out_shape.append(jax.ShapeDtypeStruct((b, t, W_A), F32))
                out_specs.append(pl.BlockSpec((1, tb, gw), lambda bi, p, i, d=d: (bi, tblk[d](i), p)))
    if emit_v:
        out_names.append('vmix')
        out_shape.append(jax.ShapeDtypeStruct((b, t, W_A), F32))
        out_specs.append(pl.BlockSpec((1, tb, gw), lambda bi, p, i: (bi, i, p)))
    out_names.append('sfin')
    out_shape.append(jax.ShapeDtypeStruct((2, b, N_GROUPS, HEAD_DIM, GROUP_W), F32))
    out_specs.append(state_spec)

    all_names = tuple(names + out_names + ['s_scr', 'mix_scr'])
    outs = pl.pallas_call(
        functools.partial(_scan_kernel, names=all_names, tb=tb, n_groups=ng, grid_mode=grid_mode,
                          has_vres=has_vres, use_s0=use_s0, emit_out=emit_out, emit_v=emit_v),
        out_shape=out_shape,
        grid=(b, npair, nt),
        in_specs=in_specs,
        out_specs=out_specs,
        scratch_shapes=[pltpu.VMEM((2, ng, HEAD_DIM, GROUP_W), F32), pltpu.VMEM((2, 3, tb, gw), F32)],
        compiler_params=_cparams(("arbitrary", "arbitrary", "arbitrary")),
    )(*args)
    return dict(zip(out_names, outs))


def _post1_kernel(y0, y1, b0, b1, ga, gb, pc, pp, pn, mg0, mg1, gnw, gnb, pw, ps, woa, wob, bm_ref, o_ref, *, tm, t_total):
    i = pl.program_id(1)
    nt = pl.num_programs(1)
    bm = bm_ref[...]

    def headsum(x):
        hi = x.astype(BF16)
        lo = (x - hi.astype(F32)).astype(BF16)
        return _dot(hi, bm) + _dot(lo, bm)

    parts = []
    for hg in range(N_GROUPS):
        cs = slice(hg * GROUP_W, (hg + 1) * GROUP_W)
        y = y0[0, :, cs] + y1[0, :, cs]
        mu = headsum(y) * (1.0 / HEAD_DIM)
        yc = y - mu
        var = headsum(yc * yc) * (1.0 / HEAD_DIM)
        yn = yc * lax.rsqrt(var + GN_EPS) * gnw[:, cs] + gnb[:, cs]
        yn = yn + b0[0, :, cs] + b1[0, :, cs]
        gate = ga[0, :, cs]
        parts.append((yn * (gate * _sigmoid(gate))).astype(BF16))
    y_a = _dot(jnp.concatenate(parts, axis=1), woa[...])

    pscale = jnp.where(i > 0, 1.0, 0.0).astype(F32)
    nscale = jnp.where(i < nt - 1, 1.0, 0.0).astype(F32)
    n_ext = tm + 2 * POOL_HALO
    tpos = (lax.broadcasted_iota(jnp.int32, (tm, 1), 0) + i * tm).astype(F32)
    parts = []
    for gi, win in enumerate(POOL_WINDOWS):
        left = win // 2
        right = win - 1 - left
        cs = slice(gi * POOL_GROUP_W, (gi + 1) * POOL_GROUP_W)
        p = pc[0, :, cs]
        ext = jnp.concatenate([pp[0, :, cs] * pscale, p, pn[0, :, cs] * nscale], axis=0)
        s = ext
        step = 1
        while step < win:
            s = s + pltpu.roll(s, step, axis=0)
            step *= 2
        if right > 0:
            s = pltpu.roll(s, n_ext - right, axis=0)
        s = s[POOL_HALO:POOL_HALO + tm]
        cnt = jnp.minimum(tpos + (right + 1), float(t_total)) - jnp.maximum(tpos - left, 0.0)
        pooled = s / cnt - p
        mixed = _dot(pooled.astype(BF16), pw[gi]) * ps[:, cs]
        gate = gb[0, :, cs]
        parts.append((mixed * (gate * _sigmoid(gate))).astype(BF16))
    y_b = _dot(jnp.concatenate(parts, axis=1), wob[...])

    merged = _sigmoid(mg0[0]) * y_a + _sigmoid(mg1[0]) * y_b
    o_ref[0] = merged.astype(BF16)


def _post1_call(sc, u, cols, lp):
    y0, y1, b0, b1 = sc['y0'], sc['y1'], sc['bo0'], sc['bo1']
    b, t, _ = y0.shape
    d = lp['w_out_a'].shape[1]
    tm = min(256, t)
    nt = t // tm
    hb = tm // POOL_HALO
    nh = t // POOL_HALO
    blk = lambda c: pl.BlockSpec((1, tm, W_A), lambda bi, i: (bi, i, c))
    wide = lambda c: pl.BlockSpec((1, tm, d), lambda bi, i: (bi, i, c))
    const2 = lambda shape: pl.BlockSpec(shape, lambda bi, i: (0, 0))
    pool_col = cols['pool']
    in_specs = [blk(0), blk(0), blk(0), blk(0),
                blk(cols['gate_a']), blk(cols['gate_b']), blk(pool_col),
                pl.BlockSpec((1, POOL_HALO, W_B), lambda bi, i: (bi, jnp.maximum(i * hb - 1, 0), pool_col)),
                pl.BlockSpec((1, POOL_HALO, W_B), lambda bi, i: (bi, jnp.minimum((i + 1) * hb, nh - 1), pool_col)),
                wide(cols['merge0']), wide(cols['merge1']),
                const2((1, W_A)), const2((1, W_A)),
                pl.BlockSpec((len(POOL_WINDOWS), POOL_GROUP_W, POOL_GROUP_W), lambda bi, i: (0, 0, 0)),
                const2((1, W_B)), const2((W_A, d)), const2((W_B, d)), const2((GROUP_W, GROUP_W))]
    args = [y0, y1, b0, b1, u, u, u, u, u, u, u,
            lp['gn_w'], lp['gn_b'], lp['pool_w'], lp['pool_scale'], lp['w_out_a'], lp['w_out_b'], lp['head_mask']]
    return pl.pallas_call(
        functools.partial(_post1_kernel, tm=tm, t_total=t),
        out_shape=jax.ShapeDtypeStruct((b, t, d), BF16),
        grid=(b, nt),
        in_specs=in_specs,
        out_specs=pl.BlockSpec((1, tm, d), lambda bi, i: (bi, i, 0)),
        compiler_params=_cparams(("arbitrary", "arbitrary")),
    )(*args)


def _post2_kernel(m_ref, h_ref, gate_ref, w_ref, fg_ref, o_ref, *, final_norm):
    out = _dot(m_ref[0], w_ref[...])
    h = h_ref[0] + gate_ref[0] * out
    if final_norm:
        ms = jnp.mean(h * h, axis=-1, keepdims=True)
        h = h * lax.rsqrt(ms + NORM_EPS) * fg_ref[...]
    o_ref[0] = h


def _post2_call(merged, h, gate, w_out, final_g, final_norm):
    bm, t, d = h.shape
    tm = min(512, t)
    return pl.pallas_call(
        functools.partial(_post2_kernel, final_norm=final_norm),
        out_shape=jax.ShapeDtypeStruct((bm, t, d), F32),
        grid=(bm, t // tm),
        in_specs=[
            pl.BlockSpec((1, tm, d), lambda b, m: (b, m, 0)),
            pl.BlockSpec((1, tm, d), lambda b, m: (b, m, 0)),
            pl.BlockSpec((1, 1, d), lambda b, m: (b, 0, 0)),
            pl.BlockSpec((d, d), lambda b, m: (0, 0)),
            pl.BlockSpec((1, d), lambda b, m: (0, 0)),
        ],
        out_specs=pl.BlockSpec((1, tm, d), lambda b, m: (b, m, 0)),
        compiler_params=_cparams(("arbitrary", "arbitrary")),
    )(merged, h, gate, w_out, final_g)


def _pad_rows(w, rows):
    return jnp.pad(w, ((0, 0),) * (w.ndim - 2) + ((0, rows - w.shape[-2]), (0, 0)))


def kernel(x, c, ctx, c_ctx, ada_w, ada_b, norm_g, w_in, tok_mu, decay_w0, decay_lora_b, iclr_a0, iclr_lora_b, key_k, key_a, bonus_rk, gn_w, gn_b, vres_v0, vres_lora_a, vres_lora_b, pool_w, pool_scale, w_out_a, w_out_b, w_out, final_g):
    depth = w_in.shape[0]
    batch, seq, d = x.shape
    ctx_len = ctx.shape[1]
    assert seq % GRID_W == 0 and seq % CHUNK == 0 and ctx_len % CHUNK == 0
    assert d % LANES == 0

    off_merge = 3 * W_A + W_B + W_A + W_B
    off_dl = off_merge + 2 * d
    off_vres = off_dl + 2 * R_DECAY + 2 * R_ICLR
    nu = -(-(off_vres + LANES) // INPROJ_TN) * INPROJ_TN
    cols = dict(pool=REF_OFF_POOL // W_B, gate_a=REF_OFF_GATE_A // W_A, gate_b=REF_OFF_GATE_B // W_A,
                merge0=off_merge // d, merge1=off_merge // d + 1)
    assert off_merge % d == 0 and off_dl % GROUP_W == 0 and off_vres % LANES == 0

    head_mask = (jnp.arange(GROUP_W)[:, None] // HEAD_DIM == jnp.arange(GROUP_W)[None, :] // HEAD_DIM).astype(BF16)

    rows = -(-(batch + 1) // 8) * 8
    c_all = jnp.zeros((rows, d), F32).at[:batch].set(c).at[batch].set(c_ctx)
    mod = _ada_call(c_all, ada_w, ada_b)

    h_lat, h_ctx = x, ctx
    vfirst = {}
    for l in range(depth):
        last = l == depth - 1
        w_l = w_in[l]
        pieces = [w_l[:, :REF_OFF_DECAY], w_l[:, REF_OFF_MERGE:REF_OFF_MERGE + 2 * d],
                  w_l[:, REF_OFF_DECAY:REF_OFF_MERGE]]
        if l > 0:
            pieces.append(vres_lora_a[l - 1])
        w_perm = jnp.concatenate(pieces, axis=1)
        w_perm = jnp.pad(w_perm, ((0, 0), (0, nu - w_perm.shape[1]))).astype(BF16)

        dlw = decay_lora_b[l]
        dlw = jnp.stack([jnp.pad(dlw[0], ((0, LANES - R_DECAY), (0, 0))),
                         jnp.pad(dlw[1], ((R_DECAY, LANES - 2 * R_DECAY), (0, 0)))]).astype(BF16)
        ilw = iclr_lora_b[l]
        ilw = jnp.stack([jnp.pad(ilw[0], ((0, LANES - R_ICLR), (0, 0))),
                         jnp.pad(ilw[1], ((R_ICLR, LANES - 2 * R_ICLR), (0, 0)))]).astype(BF16)
        lp = dict(tok_mu=tok_mu[l], decay_w0=decay_w0[l], decay_lora=dlw, iclr_a0=iclr_a0[l], iclr_lora=ilw,
                  key_k=key_k[l].reshape(1, W_A), key_a=key_a[l], bonus_rk=bonus_rk[l].reshape(2, W_A),
                  head_mask=head_mask, gn_w=gn_w[l].reshape(1, W_A), gn_b=gn_b[l].reshape(1, W_A),
                  pool_w=pool_w[l].astype(BF16), pool_scale=pool_scale[l].reshape(1, W_B),
                  w_out_a=w_out_a[l].astype(BF16), w_out_b=w_out_b[l].astype(BF16))
        w_out_l = w_out[l].astype(BF16)
        g_l = norm_g[l].reshape(1, d)

        m = mod[l]
        shift_l, scale_l, gate_l = (m[:batch, j * d:(j + 1) * d].reshape(batch, 1, d) for j in range(3))
        shift_c, scale_c, gate_c = (m[batch:batch + 1, j * d:(j + 1) * d].reshape(1, 1, d) for j in range(3))

        u_c = _inproj_call(h_ctx.reshape(1, batch * ctx_len, d), shift_c, scale_c, g_l, w_perm)
        u_c = u_c.reshape(batch, ctx_len, nu)
        u_l = _inproj_call(h_lat, shift_l, scale_l, g_l, w_perm)

        vres_c = vres_l = None
        if l > 0:
            vb = _pad_rows(vres_lora_b[l - 1], LANES).astype(BF16)
            v0 = vres_v0[l - 1].reshape(1, W_A)
            vres_c = (vfirst['c'], off_vres // LANES, vb, v0)
            vres_l = (vfirst['l'], off_vres // LANES, vb, v0)
        keep_v = l == 0 and depth > 1
        sc_c = _scan_call(u_c, off_dl // GROUP_W, None, lp, False, not last, vres_c, keep_v)
        sc_l = _scan_call(u_l, off_dl // GROUP_W, sc_c['sfin'], lp, True, True, vres_l, keep_v)
        if keep_v:
            vfirst = dict(c=sc_c['vmix'], l=sc_l['vmix'])

        merged_l = _post1_call(sc_l, u_l, cols, lp)
        h_lat = _post2_call(merged_l, h_lat, gate_l, w_out_l, final_g.reshape(1, d), last)
        if not last:
            merged_c = _post1_call(sc_c, u_c, cols, lp)
            h_ctx = _post2_call(merged_c.reshape(1, batch * ctx_len, d), h_ctx.reshape(1, batch * ctx_len, d),
                                gate_c, w_out_l, final_g.reshape(1, d), False).reshape(batch, ctx_len, d)
    return h_lat
```

```python
import functools
import math

import jax
import jax.numpy as jnp
from jax import lax
from jax.experimental import pallas as pl
from jax.experimental.pallas import tpu as pltpu

F32 = jnp.float32
BF16 = jnp.bfloat16

GRID_W = 64
W_A = 1024
W_B = 1024
HEAD_DIM = 64
POOL_WINDOWS = (2, 4, 8, 16)
POOL_GROUP_W = W_B // len(POOL_WINDOWS)
R_DECAY = 64
R_ICLR = 64
R_VRES = 32
NORM_EPS = 1e-6
GN_EPS = HEAD_DIM * 1e-5
REF_OFF_POOL = 3 * W_A
REF_OFF_GATE_A = REF_OFF_POOL + W_B
REF_OFF_GATE_B = REF_OFF_GATE_A + W_A
REF_OFF_DECAY = REF_OFF_GATE_B + W_B
REF_OFF_ICLR = REF_OFF_DECAY + 2 * R_DECAY
REF_OFF_MERGE = REF_OFF_ICLR + 2 * R_ICLR

LANES = 128
SUBLANES = 8
MXU_DIM = 256
VMEM_LIMIT_BYTES = 56 * 1024 * 1024

GROUP_W = MXU_DIM
GROUP_HEADS = GROUP_W // HEAD_DIM
N_GROUPS = W_A // GROUP_W
SCAN_GROUPS = 2
SCAN_TB = 256
CHUNK = 64
SCAN_CHUNK_SKEW = 2
N_OPERANDS = 7
HALO = 64
POOL_HALO = 8
INPROJ_TN = 1536
LOG_DECAY_SCALE = math.exp(-0.5)


def _cparams(sem):
    return pltpu.CompilerParams(dimension_semantics=sem, vmem_limit_bytes=VMEM_LIMIT_BYTES)


def _sigmoid(x):
    return 1.0 / (1.0 + jnp.exp(-x))


def _dot(a, b):
    return jnp.dot(a, b, preferred_element_type=F32)


def _dot_nt(a, b):
    return lax.dot_general(a, b, (((1,), (1,)), ((), ())), preferred_element_type=F32)


def _dot_tn(a, b):
    return lax.dot_general(a, b, (((0,), (0,)), ((), ())), preferred_element_type=F32)


def _headsums(xs, bm):
    rows = xs[0].shape[0]
    parts = []
    for x in xs:
        hi = x.astype(BF16)
        parts += [hi, (x - hi.astype(F32)).astype(BF16)]
    s = _dot(jnp.concatenate(parts, axis=0), bm)
    return [s[2 * n * rows:(2 * n + 1) * rows] + s[(2 * n + 1) * rows:(2 * n + 2) * rows] for n in range(len(xs))]


def _ada_kernel(c_ref, w_ref, b_ref, o_ref):
    c = c_ref[...]
    s = (c * _sigmoid(c)).astype(BF16)
    o_ref[0] = _dot(s, w_ref[0].astype(BF16)) + b_ref[0]


def _ada_call(c_all, ada_w, ada_b):
    n_layers, d, d3 = ada_w.shape
    rows = c_all.shape[0]
    tn = 768
    return pl.pallas_call(
        _ada_kernel,
        out_shape=jax.ShapeDtypeStruct((n_layers, rows, d3), F32),
        grid=(n_layers, d3 // tn),
        in_specs=[
            pl.BlockSpec((rows, d), lambda l, n: (0, 0)),
            pl.BlockSpec((1, d, tn), lambda l, n: (l, 0, n)),
            pl.BlockSpec((1, 1, tn), lambda l, n: (l, 0, n)),
        ],
        out_specs=pl.BlockSpec((1, rows, tn), lambda l, n: (l, 0, n)),
        compiler_params=_cparams(("arbitrary", "arbitrary")),
    )(c_all, ada_w, ada_b.reshape(n_layers, 1, d3))


def _inproj_kernel(h_ref, shift_ref, scale_ref, g_ref, w_ref, o_ref, xn_ref):
    @pl.when(pl.program_id(2) == 0)
    def _():
        x = h_ref[0]
        ms = jnp.mean(x * x, axis=-1, keepdims=True)
        y = x * lax.rsqrt(ms + NORM_EPS) * g_ref[...]
        xn_ref[...] = (y * (1.0 + scale_ref[0]) + shift_ref[0]).astype(BF16)

    o_ref[0] = _dot(xn_ref[...], w_ref[...])


def _inproj_call(h, shift, scale, g, w):
    bm, t, d = h.shape
    nu = w.shape[1]
    tm = min(1024, t)
    tn = INPROJ_TN
    assert nu % tn == 0
    return pl.pallas_call(
        _inproj_kernel,
        out_shape=jax.ShapeDtypeStruct((bm, t, nu), F32),
        grid=(bm, t // tm, nu // tn),
        in_specs=[
            pl.BlockSpec((1, tm, d), lambda b, m, n: (b, m, 0)),
            pl.BlockSpec((1, 1, d), lambda b, m, n: (b, 0, 0)),
            pl.BlockSpec((1, 1, d), lambda b, m, n: (b, 0, 0)),
            pl.BlockSpec((1, d), lambda b, m, n: (0, 0)),
            pl.BlockSpec((d, tn), lambda b, m, n: (0, n)),
        ],
        out_specs=pl.BlockSpec((1, tm, tn), lambda b, m, n: (b, m, n)),
        scratch_shapes=[pltpu.VMEM((tm, d), BF16)],
        compiler_params=_cparams(("arbitrary", "arbitrary", "arbitrary")),
    )(h, shift, scale, g, w)


def _shift_plan(grid_mode, hg):
    if grid_mode:
        return ((-1, 'first'), (1, 'last'), (-GRID_W, None), (GRID_W, None))[hg]
    return ((-1, None), (-1, None), (1, None), (1, None))[hg]


def _prep_kernel(*refs, names, tb, grid_mode, has_vres, emit_out, emit_v):
    R = dict(zip(names, refs))
    i = pl.program_id(1)
    nt = pl.num_programs(1)
    ps = jnp.where(i > 0, 1.0, 0.0).astype(F32)
    ns = jnp.where(i < nt - 1, 1.0, 0.0).astype(F32)
    n_ext = tb + 2 * HALO
    trow1 = lax.broadcasted_iota(jnp.int32, (tb, 1), 0)
    gcol = jnp.bitwise_and(trow1, GRID_W - 1)
    col_mask = dict(first=jnp.where(gcol != 0, 1.0, 0.0).astype(F32),
                    last=jnp.where(gcol != GRID_W - 1, 1.0, 0.0).astype(F32))
    bm = R['head_mask'][...]
    brow = lax.broadcasted_iota(jnp.int32, (tb, tb), 0)
    bcol = lax.broadcasted_iota(jnp.int32, (tb, tb), 1)
    same_chunk = (brow >> 6) == (bcol >> 6)
    tris = (jnp.where(jnp.logical_and(same_chunk, bcol <= brow), 1.0, 0.0).astype(BF16),
            jnp.where(jnp.logical_and(same_chunk, bcol >= brow), 1.0, 0.0).astype(BF16))
    n_chunk = tb // CHUNK
    R['ptot'][...] = jnp.zeros_like(R['ptot'])

    def mixed(hg, m):
        off, cmask = _shift_plan(grid_mode, hg)
        cs = slice(m * W_A + hg * GROUP_W, m * W_A + (hg + 1) * GROUP_W)
        x = R['cur'][0, :, cs]
        if off == -GRID_W:
            sh = jnp.concatenate([R['prv'][0, :, cs] * ps, x[:tb - HALO]], axis=0)
        elif off == GRID_W:
            sh = jnp.concatenate([x[HALO:], R['nxt'][0, :, cs] * ns], axis=0)
        else:
            ext = jnp.concatenate([R['prv'][0, :, cs] * ps, x, R['nxt'][0, :, cs] * ns], axis=0)
            sh = pltpu.roll(ext, 1 if off == -1 else n_ext - 1, axis=0)[HALO:HALO + tb]
        if cmask is not None:
            sh = sh * col_mask[cmask]
        return x + R['tok_mu'][m:m + 1, hg * GROUP_W:(hg + 1) * GROUP_W] * (sh - x)

    dl = R['dl'][0]
    dlo = jnp.tanh(dl[:, :LANES]).astype(BF16)
    alo = dl[:, LANES:].astype(BF16)
    for hg in range(N_GROUPS):
        cs = slice(hg * GROUP_W, (hg + 1) * GROUP_W)
        r = mixed(hg, 0)
        k = mixed(hg, 1)
        v = mixed(hg, 2)
        if has_vres:
            gate = _sigmoid(R['vres_v0'][:, cs] + _dot(R['lo'][0].astype(BF16), R['vres_b'][:, cs]))
            v = v + (R['vf'][0, :, cs] - v) * gate
        if emit_v:
            R['vmix'][0, :, cs] = v
        kk = k * R['key_k'][:, cs]
        a_d, kd_d, sums_in = [], [], [kk * kk]
        for d in (0, 1):
            a = _sigmoid(R['iclr_a0'][d:d + 1, cs] + _dot(alo, R['iclr_lora'][d, :, cs]))
            kd = k * (1.0 + (a - 1.0) * R['key_a'][d:d + 1, cs])
            a_d.append(a)
            kd_d.append(kd)
            if emit_out:
                sums_in.append(r * kd * R['bonus_rk'][d:d + 1, cs])
        sums = _headsums(sums_in, bm)
        kk = kk / jnp.maximum(jnp.sqrt(sums[0]), 1e-12)
        if emit_out:
            R['bonus'][0, :, cs] = (sums[1] + sums[2]) * v
        vb = v.astype(BF16)
        for d in (0, 1):
            z = R['decay_w0'][d:d + 1, cs] + _dot(dlo, R['decay_lora'][d, :, cs])
            lg = -LOG_DECAY_SCALE * _sigmoid(z)
            g1 = lg.astype(BF16)
            g2 = (lg - g1.astype(F32)).astype(BF16)
            g3 = (lg - g1.astype(F32) - g2.astype(F32)).astype(BF16)
            cum = _dot(tris[d], g1) + _dot(tris[d], g2) + _dot(tris[d], g3)
            bb = kk * a_d[d]
            kd = kd_d[d]
            for c in range(n_chunk):
                sl = slice(c * CHUNK, (c + 1) * CHUNK)
                lc = cum[sl]
                ltot = lc[CHUNK - 1:CHUNK] if d == 0 else lc[0:1]
                e_neg = jnp.exp(-lc)
                e_tail = jnp.exp(ltot - lc)
                R['ptot'][d, 0, 0, c:c + 1, cs] = jnp.exp(ltot)
                operands = (None, r[sl] * jnp.exp(lc), kk[sl] * jnp.exp(lc - lg[sl]),
                            kd[sl] * e_neg, bb[sl] * e_neg, kd[sl] * e_tail, bb[sl] * e_tail)
                R['ops'][d, 0, 0, sl, cs] = vb[sl]
                for m in range(1, N_OPERANDS):
                    R['ops'][d, m, 0, sl, cs] = operands[m].astype(BF16)


def _prep_call(u, dl_col, lp, grid_mode, emit_out, vres=None, emit_v=False):
    b, t, _ = u.shape
    tb = min(SCAN_TB, t)
    nt = t // tb
    hb = tb // HALO
    nh = t // HALO
    w3 = 3 * W_A
    has_vres = vres is not None
    assert GROUP_W == W_A // 4 and tb % HALO == 0 and tb % CHUNK == 0 and tb // CHUNK <= SUBLANES
    names, in_specs, args = [], [], []

    def add(name, spec, arr):
        names.append(name)
        in_specs.append(spec)
        args.append(arr)

    add('cur', pl.BlockSpec((1, tb, w3), lambda bi, i: (bi, i, 0)), u)
    add('prv', pl.BlockSpec((1, HALO, w3), lambda bi, i: (bi, jnp.maximum(i * hb - 1, 0), 0)), u)
    add('nxt', pl.BlockSpec((1, HALO, w3), lambda bi, i: (bi, jnp.minimum((i + 1) * hb, nh - 1), 0)), u)
    add('dl', pl.BlockSpec((1, tb, GROUP_W), lambda bi, i: (bi, i, dl_col)), u)
    if has_vres:
        v_first, vres_col, vres_b, vres_v0 = vres
        add('vf', pl.BlockSpec((1, tb, W_A), lambda bi, i: (bi, i, 0)), v_first)
        add('lo', pl.BlockSpec((1, tb, LANES), lambda bi, i: (bi, i, vres_col)), u)
        add('vres_b', pl.BlockSpec((LANES, W_A), lambda bi, i: (0, 0)), vres_b)
        add('vres_v0', pl.BlockSpec((1, W_A), lambda bi, i: (0, 0)), vres_v0)
    vec = lambda rows: pl.BlockSpec((rows, W_A), lambda bi, i: (0, 0))
    lora = pl.BlockSpec((2, LANES, W_A), lambda bi, i: (0, 0, 0))
    add('tok_mu', vec(3), lp['tok_mu'])
    add('decay_w0', vec(2), lp['decay_w0'])
    add('decay_lora', lora, lp['decay_lora'])
    add('iclr_a0', vec(2), lp['iclr_a0'])
    add('iclr_lora', lora, lp['iclr_lora'])
    add('key_k', vec(1), lp['key_k'])
    add('key_a', vec(2), lp['key_a'])
    add('bonus_rk', vec(2), lp['bonus_rk'])
    add('head_mask', pl.BlockSpec((GROUP_W, GROUP_W), lambda bi, i: (0, 0)), lp['head_mask'])

    out_names = ['ops', 'ptot']
    out_shape = [jax.ShapeDtypeStruct((2, N_OPERANDS, b, t, W_A), BF16),
                 jax.ShapeDtypeStruct((2, b, nt, SUBLANES, W_A), F32)]
    out_specs = [pl.BlockSpec((2, N_OPERANDS, 1, tb, W_A), lambda bi, i: (0, 0, bi, i, 0)),
                 pl.BlockSpec((2, 1, 1, SUBLANES, W_A), lambda bi, i: (0, bi, i, 0, 0))]
    for flag, nm in ((emit_out, 'bonus'), (emit_v, 'vmix')):
        if flag:
            out_names.append(nm)
            out_shape.append(jax.ShapeDtypeStruct((b, t, W_A), F32))
            out_specs.append(pl.BlockSpec((1, tb, W_A), lambda bi, i: (bi, i, 0)))

    outs = pl.pallas_call(
        functools.partial(_prep_kernel, names=tuple(names + out_names), tb=tb, grid_mode=grid_mode,
                          has_vres=has_vres, emit_out=emit_out, emit_v=emit_v),
        out_shape=out_shape,
        grid=(b, nt),
        in_specs=in_specs,
        out_specs=out_specs,
        compiler_params=_cparams(("arbitrary", "arbitrary")),
    )(*args)
    return dict(zip(out_names, outs))


def _scan_kernel(*refs, names, tb, n_groups, use_s0, emit_out):
    R = dict(zip(names, refs))
    s_ref = R['s_scr']
    i = pl.program_id(2)
    nt = pl.num_programs(2)

    @pl.when(i == 0)
    def _():
        if use_s0:
            s_ref[...] = R['s0'][:, 0]
        else:
            s_ref[...] = jnp.zeros_like(s_ref)

    lane1 = lax.broadcasted_iota(jnp.int32, (1, LANES), 1)
    half_f = (jnp.where(lane1 < HEAD_DIM, 1.0, 0.0).astype(F32), jnp.where(lane1 >= HEAD_DIM, 1.0, 0.0).astype(F32))
    half_b = (half_f[0].astype(BF16), half_f[1].astype(BF16))
    zero_tile = jnp.zeros((HEAD_DIM, LANES), BF16)
    n_tiles = GROUP_W // LANES

    def bd(x):
        xb = x.astype(BF16)
        rows = []
        for j in range(GROUP_HEADS):
            t = (j * HEAD_DIM) // LANES
            piece = xb[:, t * LANES:(t + 1) * LANES] * half_b[(j * HEAD_DIM % LANES) // HEAD_DIM]
            rows.append(jnp.concatenate([piece if tt == t else zero_tile for tt in range(n_tiles)], axis=1))
        return jnp.concatenate(rows, axis=0)

    def head_blocks(full):
        tiles = []
        for t in range(n_tiles):
            acc = None
            for j in range(GROUP_HEADS):
                if (j * HEAD_DIM) // LANES == t:
                    part = (full[j * HEAD_DIM:(j + 1) * HEAD_DIM, t * LANES:(t + 1) * LANES]
                            * half_f[(j * HEAD_DIM % LANES) // HEAD_DIM])
                    acc = part if acc is None else acc + part
            tiles.append(acc)
        return jnp.concatenate(tiles, axis=1)

    def prod(a, b):
        return _dot(a.astype(BF16), bd(b))

    trow = lax.broadcasted_iota(jnp.int32, (CHUNK, GROUP_W), 0)
    scol = jnp.bitwise_and(lax.broadcasted_iota(jnp.int32, (CHUNK, GROUP_W), 1), CHUNK - 1)
    eye = jnp.where(trow == scol, 1.0, 0.0).astype(F32)
    same16 = (trow >> 4) == (scol >> 4)
    same32 = (trow >> 5) == (scol >> 5)
    tri_masks = []
    for before, upto in ((scol < trow, scol <= trow), (scol > trow, scol >= trow)):
        tri_masks.append(dict(
            neg_d16=jnp.where(jnp.logical_and(before, same16), -1.0, 0.0).astype(F32),
            o32=jnp.where(jnp.logical_and(before, jnp.logical_and(same32, jnp.logical_not(same16))), 1.0, 0.0).astype(BF16),
            o64=jnp.where(jnp.logical_and(before, jnp.logical_not(same32)), 1.0, 0.0).astype(BF16),
            strict=jnp.where(before, 1.0, 0.0).astype(F32),
            incl=jnp.where(upto, 1.0, 0.0).astype(F32)))

    n_chunk = tb // CHUNK
    s_cur = {(d, g): s_ref[d, g] for d in (0, 1) for g in range(n_groups)}

    def unit_stages(d, g, c):
        sl = slice(c * CHUNK, (c + 1) * CHUNK)
        cs = slice(g * GROUP_W, (g + 1) * GROUP_W)
        vc, r_h, kap_h, k_h, b_h, k_t, b_t = (R[f'ops{d}'][0, m, 0, sl, cs] for m in range(N_OPERANDS))
        p_tot = R[f'ptot{d}'][0, 0, 0, c:c + 1, cs]
        mk = tri_masks[d]
        x2 = jnp.concatenate([kap_h, r_h], axis=0)
        gb = _dot_nt(x2, bd(b_h))
        yield
        gk = _dot_nt(x2, bd(k_h))
        yield
        ab = gb[:CHUNK]
        abb = ab.astype(BF16)
        n1 = ab * mk['neg_d16']
        o32 = abb * mk['o32']
        o64 = abb * mk['o64']
        a_rb = (gb[CHUNK:] * mk['incl']).astype(BF16)
        a_kk = jnp.concatenate([gk[:CHUNK] * mk['strict'], gk[CHUNK:] * mk['incl']], axis=0).astype(BF16)
        m1y = _dot(a_kk, bd(vc))
        yield
        n1b = n1.astype(BF16)
        nk = prod(n1b, n1b).astype(BF16)
        x = eye + n1
        yield
        for _ in range(2):
            both = _dot(jnp.concatenate([nk, x.astype(BF16)], axis=0), bd(nk))
            nk = both[:CHUNK].astype(BF16)
            x = x + both[CHUNK:]
            yield
        x = x + prod(x, nk)
        yield
        for off in (o32, o64):
            xb = x.astype(BF16)
            xo = prod(xb, off).astype(BF16)
            yield
            x = x - prod(xo, xb)
            yield
        t16 = x.astype(BF16)
        w = _dot(t16, bd(kap_h)).astype(BF16)
        yield
        ut = _dot(t16, bd(m1y[:CHUNK]))
        yield
        xs = bd(head_blocks(_dot_tn(w, b_t)))
        yield
        ns = head_blocks(_dot_tn(jnp.concatenate([vc, (-ut).astype(BF16)], axis=0),
                                 jnp.concatenate([k_t, b_t], axis=0)))
        yield
        s_in = s_cur[d, g]
        sb = s_in.astype(BF16)
        s_cur[d, g] = s_in * p_tot - _dot(sb, xs) + ns
        yield
        if emit_out:
            wr = _dot_nt(jnp.concatenate([w, r_h], axis=0), bd(sb))
            yield
            u = -(wr[:CHUNK] + ut)
            R[f'y{d}'][0, sl, cs] = wr[CHUNK:] + prod(a_rb, u) + m1y[CHUNK:]

    jobs = [(pos * SCAN_CHUNK_SKEW, unit_stages(d, g, c)) for d in (0, 1) for g in range(n_groups)
            for pos, c in enumerate(range(n_chunk) if d == 0 else range(n_chunk - 1, -1, -1))]
    tick = 0
    while jobs:
        for entry in list(jobs):
            if tick >= entry[0]:
                try:
                    next(entry[1])
                except StopIteration:
                    jobs.remove(entry)
        tick += 1
    for (d, g), s_val in s_cur.items():
        s_ref[d, g] = s_val

    @pl.when(i == nt - 1)
    def _():
        R['sfin'][:, 0] = s_ref[...]


def _scan_call(prep, s0, emit_out):
    ops, ptot = prep['ops'], prep['ptot']
    _, _, b, t, _ = ops.shape
    tb = min(SCAN_TB, t)
    nt = t // tb
    ng = SCAN_GROUPS
    gw = ng * GROUP_W
    npair = N_GROUPS // ng
    use_s0 = s0 is not None
    tblk = (lambda i: i, lambda i: nt - 1 - i)
    names, in_specs, args = [], [], []

    def add(name, spec, arr):
        names.append(name)
        in_specs.append(spec)
        args.append(arr)

    for d in (0, 1):
        add(f'ops{d}', pl.BlockSpec((1, N_OPERANDS, 1, tb, gw), lambda bi, p, i, d=d: (d, 0, bi, tblk[d](i), p)), ops)
        add(f'ptot{d}', pl.BlockSpec((1, 1, 1, SUBLANES, gw), lambda bi, p, i, d=d: (d, bi, tblk[d](i), 0, p)), ptot)
    state_spec = pl.BlockSpec((2, 1, ng, HEAD_DIM, GROUP_W), lambda bi, p, i: (0, bi, p, 0, 0))
    if use_s0:
        add('s0', state_spec, s0)

    out_names, out_shape, out_specs = [], [], []
    if emit_out:
        for d in (0, 1):
            out_names.append(f'y{d}')
            out_shape.append(jax.ShapeDtypeStruct((b, t, W_A), F32))
            out_specs.append(pl.BlockSpec((1, tb, gw), lambda bi, p, i, d=d: (bi, tblk[d](i), p)))
    out_names.append('sfin')
    out_shape.append(jax.ShapeDtypeStruct((2, b, N_GROUPS, HEAD_DIM, GROUP_W), F32))
    out_specs.append(state_spec)

    outs = pl.pallas_call(
        functools.partial(_scan_kernel, names=tuple(names + out_names + ['s_scr']), tb=tb, n_groups=ng,
                          use_s0=use_s0, emit_out=emit_out),
        out_shape=out_shape,
        grid=(b, npair, nt),
        in_specs=in_specs,
        out_specs=out_specs,
        scratch_shapes=[pltpu.VMEM((2, ng, HEAD_DIM, GROUP_W), F32)],
        compiler_params=_cparams(("arbitrary", "arbitrary", "arbitrary")),
    )(*args)
    return dict(zip(out_names, outs))


def _post1_kernel(y0, y1, bo, ga, gb, pc, pp, pn, mg0, mg1, gnw, gnb, pw, ps, woa, wob, bm_ref, o_ref, *, tm, t_total):
    i = pl.program_id(1)
    nt = pl.num_programs(1)
    bm = bm_ref[...]

    parts = []
    for hg in range(N_GROUPS):
        cs = slice(hg * GROUP_W, (hg + 1) * GROUP_W)
        y = y0[0, :, cs] + y1[0, :, cs]
        mu = _headsums([y], bm)[0] * (1.0 / HEAD_DIM)
        yc = y - mu
        var = _headsums([yc * yc], bm)[0] * (1.0 / HEAD_DIM)
        yn = yc * lax.rsqrt(var + GN_EPS) * gnw[:, cs] + gnb[:, cs]
        yn = yn + bo[0, :, cs]
        gate = ga[0, :, cs]
        parts.append((yn * (gate * _sigmoid(gate))).astype(BF16))
    y_a = _dot(jnp.concatenate(parts, axis=1), woa[...])

    pscale = jnp.where(i > 0, 1.0, 0.0).astype(F32)
    nscale = jnp.where(i < nt - 1, 1.0, 0.0).astype(F32)
    n_ext = tm + 2 * POOL_HALO
    tpos = (lax.broadcasted_iota(jnp.int32, (tm, 1), 0) + i * tm).astype(F32)
    parts = []
    for gi, win in enumerate(POOL_WINDOWS):
        left = win // 2
        right = win - 1 - left
        cs = slice(gi * POOL_GROUP_W, (gi + 1) * POOL_GROUP_W)
        p = pc[0, :, cs]
        ext = jnp.concatenate([pp[0, :, cs] * pscale, p, pn[0, :, cs] * nscale], axis=0)
        s = ext
        step = 1
        while step < win:
            s = s + pltpu.roll(s, step, axis=0)
            step *= 2
        if right > 0:
            s = pltpu.roll(s, n_ext - right, axis=0)
        s = s[POOL_HALO:POOL_HALO + tm]
        cnt = jnp.minimum(tpos + (right + 1), float(t_total)) - jnp.maximum(tpos - left, 0.0)
        pooled = s / cnt - p
        mixed = _dot(pooled.astype(BF16), pw[gi]) * ps[:, cs]
        gate = gb[0, :, cs]
        parts.append((mixed * (gate * _sigmoid(gate))).astype(BF16))
    y_b = _dot(jnp.concatenate(parts, axis=1), wob[...])

    merged = _sigmoid(mg0[0]) * y_a + _sigmoid(mg1[0]) * y_b
    o_ref[0] = merged.astype(BF16)


def _post1_call(sc, bonus, u, cols, lp):
    y0, y1 = sc['y0'], sc['y1']
    b, t, _ = y0.shape
    d = lp['w_out_a'].shape[1]
    tm = min(256, t)
    nt = t // tm
    hb = tm // POOL_HALO
    nh = t // POOL_HALO
    blk = lambda c: pl.BlockSpec((1, tm, W_A), lambda bi, i: (bi, i, c))
    wide = lambda c: pl.BlockSpec((1, tm, d), lambda bi, i: (bi, i, c))
    const2 = lambda shape: pl.BlockSpec(shape, lambda bi, i: (0, 0))
    pool_col = cols['pool']
    in_specs = [blk(0), blk(0), blk(0),
                blk(cols['gate_a']), blk(cols['gate_b']), blk(pool_col),
                pl.BlockSpec((1, POOL_HALO, W_B), lambda bi, i: (bi, jnp.maximum(i * hb - 1, 0), pool_col)),
                pl.BlockSpec((1, POOL_HALO, W_B), lambda bi, i: (bi, jnp.minimum((i + 1) * hb, nh - 1), pool_col)),
                wide(cols['merge0']), wide(cols['merge1']),
                const2((1, W_A)), const2((1, W_A)),
                pl.BlockSpec((len(POOL_WINDOWS), POOL_GROUP_W, POOL_GROUP_W), lambda bi, i: (0, 0, 0)),
                const2((1, W_B)), const2((W_A, d)), const2((W_B, d)), const2((GROUP_W, GROUP_W))]
    args = [y0, y1, bonus, u, u, u, u, u, u, u,
            lp['gn_w'], lp['gn_b'], lp['pool_w'], lp['pool_scale'], lp['w_out_a'], lp['w_out_b'], lp['head_mask']]
    return pl.pallas_call(
        functools.partial(_post1_kernel, tm=tm, t_total=t),
        out_shape=jax.ShapeDtypeStruct((b, t, d), BF16),
        grid=(b, nt),
        in_specs=in_specs,
        out_specs=pl.BlockSpec((1, tm, d), lambda bi, i: (bi, i, 0)),
        compiler_params=_cparams(("arbitrary", "arbitrary")),
    )(*args)


def _post2_kernel(m_ref, h_ref, gate_ref, w_ref, fg_ref, o_ref, *, final_norm):
    out = _dot(m_ref[0], w_ref[...])
    h = h_ref[0] + gate_ref[0] * out
    if final_norm:
        ms = jnp.mean(h * h, axis=-1, keepdims=True)
        h = h * lax.rsqrt(ms + NORM_EPS) * fg_ref[...]
    o_ref[0] = h


def _post2_call(merged, h, gate, w_out, final_g, final_norm):
    bm, t, d = h.shape
    tm = min(512, t)
    return pl.pallas_call(
        functools.partial(_post2_kernel, final_norm=final_norm),
        out_shape=jax.ShapeDtypeStruct((bm, t, d), F32),
        grid=(bm, t // tm),
        in_specs=[
            pl.BlockSpec((1, tm, d), lambda b, m: (b, m, 0)),
            pl.BlockSpec((1, tm, d), lambda b, m: (b, m, 0)),
            pl.BlockSpec((1, 1, d), lambda b, m: (b, 0, 0)),
            pl.BlockSpec((d, d), lambda b, m: (0, 0)),
            pl.BlockSpec((1, d), lambda b, m: (0, 0)),
        ],
        out_specs=pl.BlockSpec((1, tm, d), lambda b, m: (b, m, 0)),
        compiler_params=_cparams(("arbitrary", "arbitrary")),
    )(merged, h, gate, w_out, final_g)


def _pad_rows(w, rows):
    return jnp.pad(w, ((0, 0),) * (w.ndim - 2) + ((0, rows - w.shape[-2]), (0, 0)))


def kernel(x, c, ctx, c_ctx, ada_w, ada_b, norm_g, w_in, tok_mu, decay_w0, decay_lora_b, iclr_a0, iclr_lora_b, key_k, key_a, bonus_rk, gn_w, gn_b, vres_v0, vres_lora_a, vres_lora_b, pool_w, pool_scale, w_out_a, w_out_b, w_out, final_g):
    depth = w_in.shape[0]
    batch, seq, d = x.shape
    ctx_len = ctx.shape[1]
    assert seq % GRID_W == 0 and seq % CHUNK == 0 and ctx_len % CHUNK == 0
    assert d % LANES == 0

    off_merge = 3 * W_A + W_B + W_A + W_B
    off_dl = off_merge + 2 * d
    off_vres = off_dl + 2 * R_DECAY + 2 * R_ICLR
    nu = -(-(off_vres + LANES) // INPROJ_TN) * INPROJ_TN
    cols = dict(pool=REF_OFF_POOL // W_B, gate_a=REF_OFF_GATE_A // W_A, gate_b=REF_OFF_GATE_B // W_A,
                merge0=off_merge // d, merge1=off_merge // d + 1)
    assert off_merge % d == 0 and off_dl % GROUP_W == 0 and off_vres % LANES == 0

    head_mask = (jnp.arange(GROUP_W)[:, None] // HEAD_DIM == jnp.arange(GROUP_W)[None, :] // HEAD_DIM).astype(BF16)

    rows = -(-(batch + 1) // SUBLANES) * SUBLANES
    c_all = jnp.zeros((rows, d), F32).at[:batch].set(c).at[batch].set(c_ctx)
    mod = _ada_call(c_all, ada_w, ada_b)

    h_lat, h_ctx = x, ctx
    vfirst = {}
    for l in range(depth):
        last = l == depth - 1
        w_l = w_in[l]
        pieces = [w_l[:, :REF_OFF_DECAY], w_l[:, REF_OFF_MERGE:REF_OFF_MERGE + 2 * d],
                  w_l[:, REF_OFF_DECAY:REF_OFF_MERGE]]
        if l > 0:
            pieces.append(vres_lora_a[l - 1])
        w_perm = jnp.concatenate(pieces, axis=1)
        w_perm = jnp.pad(w_perm, ((0, 0), (0, nu - w_perm.shape[1]))).astype(BF16)

        dlw = decay_lora_b[l]
        dlw = jnp.stack([jnp.pad(dlw[0], ((0, LANES - R_DECAY), (0, 0))),
                         jnp.pad(dlw[1], ((R_DECAY, LANES - 2 * R_DECAY), (0, 0)))]).astype(BF16)
        ilw = iclr_lora_b[l]
        ilw = jnp.stack([jnp.pad(ilw[0], ((0, LANES - R_ICLR), (0, 0))),
                         jnp.pad(ilw[1], ((R_ICLR, LANES - 2 * R_ICLR), (0, 0)))]).astype(BF16)
        lp = dict(tok_mu=tok_mu[l], decay_w0=decay_w0[l], decay_lora=dlw, iclr_a0=iclr_a0[l], iclr_lora=ilw,
                  key_k=key_k[l].reshape(1, W_A), key_a=key_a[l], bonus_rk=bonus_rk[l].reshape(2, W_A),
                  head_mask=head_mask, gn_w=gn_w[l].reshape(1, W_A), gn_b=gn_b[l].reshape(1, W_A),
                  pool_w=pool_w[l].astype(BF16), pool_scale=pool_scale[l].reshape(1, W_B),
                  w_out_a=w_out_a[l].astype(BF16), w_out_b=w_out_b[l].astype(BF16))
        w_out_l = w_out[l].astype(BF16)
        g_l = norm_g[l].reshape(1, d)

        m = mod[l]
        shift_l, scale_l, gate_l = (m[:batch, j * d:(j + 1) * d].reshape(batch, 1, d) for j in range(3))
        shift_c, scale_c, gate_c = (m[batch:batch + 1, j * d:(j + 1) * d].reshape(1, 1, d) for j in range(3))

        u_c = _inproj_call(h_ctx.reshape(1, batch * ctx_len, d), shift_c, scale_c, g_l, w_perm)
        u_c = u_c.reshape(batch, ctx_len, nu)
        u_l = _inproj_call(h_lat, shift_l, scale_l, g_l, w_perm)

        vres_c = vres_l = None
        if l > 0:
            vb = _pad_rows(vres_lora_b[l - 1], LANES).astype(BF16)
            v0 = vres_v0[l - 1].reshape(1, W_A)
            vres_c = (vfirst['c'], off_vres // LANES, vb, v0)
            vres_l = (vfirst['l'], off_vres // LANES, vb, v0)
        keep_v = l == 0 and depth > 1
        prep_c = _prep_call(u_c, off_dl // GROUP_W, lp, False, not last, vres_c, keep_v)
        prep_l = _prep_call(u_l, off_dl // GROUP_W, lp, True, True, vres_l, keep_v)
        if keep_v:
            vfirst = dict(c=prep_c['vmix'], l=prep_l['vmix'])
        sc_c = _scan_call(prep_c, None, not last)
        sc_l = _scan_call(prep_l, sc_c['sfin'], True)

        merged_l = _post1_call(sc_l, prep_l['bonus'], u_l, cols, lp)
        h_lat = _post2_call(merged_l, h_lat, gate_l, w_out_l, final_g.reshape(1, d), last)
        if not last:
            merged_c = _post1_call(sc_c, prep_c['bonus'], u_c, cols, lp)
            h_ctx = _post2_call(merged_c.reshape(1, batch * ctx_len, d), h_ctx.reshape(1, batch * ctx_len, d),
                                gate_c, w_out_l, final_g.reshape(1, d), False).reshape(batch, ctx_len, d)
    return h_lat
```

```python
import functools
import math

import jax
import jax.numpy as jnp
from jax import lax
from jax.experimental import pallas as pl
from jax.experimental.pallas import tpu as pltpu

F32 = jnp.float32
BF16 = jnp.bfloat16

GRID_W = 64
W_A = 1024
W_B = 1024
HEAD_DIM = 64
POOL_WINDOWS = (2, 4, 8, 16)
POOL_GROUP_W = W_B // len(POOL_WINDOWS)
R_DECAY = 64
R_ICLR = 64
R_VRES = 32
NORM_EPS = 1e-6
GN_EPS = HEAD_DIM * 1e-5
REF_OFF_POOL = 3 * W_A
REF_OFF_GATE_A = REF_OFF_POOL + W_B
REF_OFF_GATE_B = REF_OFF_GATE_A + W_A
REF_OFF_DECAY = REF_OFF_GATE_B + W_B
REF_OFF_ICLR = REF_OFF_DECAY + 2 * R_DECAY
REF_OFF_MERGE = REF_OFF_ICLR + 2 * R_ICLR

LANES = 128
SUBLANES = 8
MXU_DIM = 256
VMEM_LIMIT_BYTES = 56 * 1024 * 1024

GROUP_W = MXU_DIM
GROUP_HEADS = GROUP_W // HEAD_DIM
N_GROUPS = W_A // GROUP_W
SCAN_GROUPS = 2
SCAN_TB = 256
CHUNK = 64
SCAN_CHUNK_SKEW = 2
N_OPERANDS = 7
HALO = 64
POOL_HALO = 8
INPROJ_TN = 1536
LOG_DECAY_SCALE = math.exp(-0.5)


def _cparams(sem):
    return pltpu.CompilerParams(dimension_semantics=sem, vmem_limit_bytes=VMEM_LIMIT_BYTES)


def _sigmoid(x):
    return 1.0 / (1.0 + jnp.exp(-x))


def _dot(a, b):
    return jnp.dot(a, b, preferred_element_type=F32)


def _dot_nt(a, b):
    return lax.dot_general(a, b, (((1,), (1,)), ((), ())), preferred_element_type=F32)


def _dot_tn(a, b):
    return lax.dot_general(a, b, (((0,), (0,)), ((), ())), preferred_element_type=F32)


def _headsums(xs, bm):
    rows = xs[0].shape[0]
    parts = []
    for x in xs:
        hi = x.astype(BF16)
        parts += [hi, (x - hi.astype(F32)).astype(BF16)]
    s = _dot(jnp.concatenate(parts, axis=0), bm)
    return [s[2 * n * rows:(2 * n + 1) * rows] + s[(2 * n + 1) * rows:(2 * n + 2) * rows] for n in range(len(xs))]


def _ada_kernel(c_ref, w_ref, b_ref, o_ref):
    c = c_ref[...]
    s = (c * _sigmoid(c)).astype(BF16)
    o_ref[0] = _dot(s, w_ref[0].astype(BF16)) + b_ref[0]


def _ada_call(c_all, ada_w, ada_b):
    n_layers, d, d3 = ada_w.shape
    rows = c_all.shape[0]
    tn = 768
    return pl.pallas_call(
        _ada_kernel,
        out_shape=jax.ShapeDtypeStruct((n_layers, rows, d3), F32),
        grid=(n_layers, d3 // tn),
        in_specs=[
            pl.BlockSpec((rows, d), lambda l, n: (0, 0)),
            pl.BlockSpec((1, d, tn), lambda l, n: (l, 0, n)),
            pl.BlockSpec((1, 1, tn), lambda l, n: (l, 0, n)),
        ],
        out_specs=pl.BlockSpec((1, rows, tn), lambda l, n: (l, 0, n)),
        compiler_params=_cparams(("arbitrary", "arbitrary")),
    )(c_all, ada_w, ada_b.reshape(n_layers, 1, d3))


def _inproj_kernel(h_ref, shift_ref, scale_ref, g_ref, w_ref, o_ref, xn_ref):
    @pl.when(pl.program_id(2) == 0)
    def _():
        x = h_ref[0]
        ms = jnp.mean(x * x, axis=-1, keepdims=True)
        y = x * lax.rsqrt(ms + NORM_EPS) * g_ref[...]
        xn_ref[...] = (y * (1.0 + scale_ref[0]) + shift_ref[0]).astype(BF16)

    o_ref[0] = _dot(xn_ref[...], w_ref[...])


def _inproj_call(h, shift, scale, g, w):
    bm, t, d = h.shape
    nu = w.shape[1]
    tm = min(1024, t)
    tn = INPROJ_TN
    assert nu % tn == 0
    return pl.pallas_call(
        _inproj_kernel,
        out_shape=jax.ShapeDtypeStruct((bm, t, nu), F32),
        grid=(bm, t // tm, nu // tn),
        in_specs=[
            pl.BlockSpec((1, tm, d), lambda b, m, n: (b, m, 0)),
            pl.BlockSpec((1, 1, d), lambda b, m, n: (b, 0, 0)),
            pl.BlockSpec((1, 1, d), lambda b, m, n: (b, 0, 0)),
            pl.BlockSpec((1, d), lambda b, m, n: (0, 0)),
            pl.BlockSpec((d, tn), lambda b, m, n: (0, n)),
        ],
        out_specs=pl.BlockSpec((1, tm, tn), lambda b, m, n: (b, m, n)),
        scratch_shapes=[pltpu.VMEM((tm, d), BF16)],
        compiler_params=_cparams(("arbitrary", "arbitrary", "arbitrary")),
    )(h, shift, scale, g, w)


def _shift_plan(grid_mode, hg):
    if grid_mode:
        return ((-1, 'first'), (1, 'last'), (-GRID_W, None), (GRID_W, None))[hg]
    return ((-1, None), (-1, None), (1, None), (1, None))[hg]


def _prep_kernel(*refs, names, tb, grid_mode, has_vres, emit_out, emit_v):
    R = dict(zip(names, refs))
    i = pl.program_id(1)
    nt = pl.num_programs(1)
    ps = jnp.where(i > 0, 1.0, 0.0).astype(F32)
    ns = jnp.where(i < nt - 1, 1.0, 0.0).astype(F32)
    n_ext = tb + 2 * HALO
    trow1 = lax.broadcasted_iota(jnp.int32, (tb, 1), 0)
    gcol = jnp.bitwise_and(trow1, GRID_W - 1)
    col_mask = dict(first=jnp.where(gcol != 0, 1.0, 0.0).astype(F32),
                    last=jnp.where(gcol != GRID_W - 1, 1.0, 0.0).astype(F32))
    bm = R['head_mask'][...]
    brow = lax.broadcasted_iota(jnp.int32, (tb, tb), 0)
    bcol = lax.broadcasted_iota(jnp.int32, (tb, tb), 1)
    same_chunk = (brow >> 6) == (bcol >> 6)
    tris = (jnp.where(jnp.logical_and(same_chunk, bcol <= brow), 1.0, 0.0).astype(BF16),
            jnp.where(jnp.logical_and(same_chunk, bcol >= brow), 1.0, 0.0).astype(BF16))
    n_chunk = tb // CHUNK
    R['ptot'][...] = jnp.zeros_like(R['ptot'])

    def mixed(hg, m):
        off, cmask = _shift_plan(grid_mode, hg)
        cs = slice(m * W_A + hg * GROUP_W, m * W_A + (hg + 1) * GROUP_W)
        x = R['cur'][0, :, cs]
        if off == -GRID_W:
            sh = jnp.concatenate([R['prv'][0, :, cs] * ps, x[:tb - HALO]], axis=0)
        elif off == GRID_W:
            sh = jnp.concatenate([x[HALO:], R['nxt'][0, :, cs] * ns], axis=0)
        else:
            ext = jnp.concatenate([R['prv'][0, :, cs] * ps, x, R['nxt'][0, :, cs] * ns], axis=0)
            sh = pltpu.roll(ext, 1 if off == -1 else n_ext - 1, axis=0)[HALO:HALO + tb]
        if cmask is not None:
            sh = sh * col_mask[cmask]
        return x + R['tok_mu'][m:m + 1, hg * GROUP_W:(hg + 1) * GROUP_W] * (sh - x)

    dl = R['dl'][0]
    dlo = jnp.tanh(dl[:, :LANES]).astype(BF16)
    alo = dl[:, LANES:].astype(BF16)
    for hg in range(N_GROUPS):
        cs = slice(hg * GROUP_W, (hg + 1) * GROUP_W)
        r = mixed(hg, 0)
        k = mixed(hg, 1)
        v = mixed(hg, 2)
        if has_vres:
            gate = _sigmoid(R['vres_v0'][:, cs] + _dot(R['lo'][0].astype(BF16), R['vres_b'][:, cs]))
            v = v + (R['vf'][0, :, cs] - v) * gate
        if emit_v:
            R['vmix'][0, :, cs] = v
        kk = k * R['key_k'][:, cs]
        a_d, kd_d, sums_in = [], [], [kk * kk]
        for d in (0, 1):
            a = _sigmoid(R['iclr_a0'][d:d + 1, cs] + _dot(alo, R['iclr_lora'][d, :, cs]))
            kd = k * (1.0 + (a - 1.0) * R['key_a'][d:d + 1, cs])
            a_d.append(a)
            kd_d.append(kd)
            if emit_out:
                sums_in.append(r * kd * R['bonus_rk'][d:d + 1, cs])
        sums = _headsums(sums_in, bm)
        kk = kk / jnp.maximum(jnp.sqrt(sums[0]), 1e-12)
        if emit_out:
            R['bonus'][0, :, cs] = (sums[1] + sums[2]) * v
        vb = v.astype(BF16)
        for d in (0, 1):
            z = R['decay_w0'][d:d + 1, cs] + _dot(dlo, R['decay_lora'][d, :, cs])
            lg = -LOG_DECAY_SCALE * _sigmoid(z)
            g1 = lg.astype(BF16)
            g2 = (lg - g1.astype(F32)).astype(BF16)
            g3 = (lg - g1.astype(F32) - g2.astype(F32)).astype(BF16)
            cum = _dot(tris[d], g1) + _dot(tris[d], g2) + _dot(tris[d], g3)
            bb = kk * a_d[d]
            kd = kd_d[d]
            for c in range(n_chunk):
                sl = slice(c * CHUNK, (c + 1) * CHUNK)
                lc = cum[sl]
                ltot = lc[CHUNK - 1:CHUNK] if d == 0 else lc[0:1]
                e_neg = jnp.exp(-lc)
                e_tail = jnp.exp(ltot - lc)
                R['ptot'][d, 0, 0, c:c + 1, cs] = jnp.exp(ltot)
                operands = (None, r[sl] * jnp.exp(lc), kk[sl] * jnp.exp(lc - lg[sl]),
                            kd[sl] * e_neg, bb[sl] * e_neg, kd[sl] * e_tail, bb[sl] * e_tail)
                R['ops'][d, 0, 0, sl, cs] = vb[sl]
                for m in range(1, N_OPERANDS):
                    R['ops'][d, m, 0, sl, cs] = operands[m].astype(BF16)


def _prep_call(u, dl_col, lp, grid_mode, emit_out, vres=None, emit_v=False):
    b, t, _ = u.shape
    tb = min(SCAN_TB, t)
    nt = t // tb
    hb = tb // HALO
    nh = t // HALO
    w3 = 3 * W_A
    has_vres = vres is not None
    assert GROUP_W == W_A // 4 and tb % HALO == 0 and tb % CHUNK == 0 and tb // CHUNK <= SUBLANES
    names, in_specs, args = [], [], []

    def add(name, spec, arr):
        names.append(name)
        in_specs.append(spec)
        args.append(arr)

    add('cur', pl.BlockSpec((1, tb, w3), lambda bi, i: (bi, i, 0)), u)
    add('prv', pl.BlockSpec((1, HALO, w3), lambda bi, i: (bi, jnp.maximum(i * hb - 1, 0), 0)), u)
    add('nxt', pl.BlockSpec((1, HALO, w3), lambda bi, i: (bi, jnp.minimum((i + 1) * hb, nh - 1), 0)), u)
    add('dl', pl.BlockSpec((1, tb, GROUP_W), lambda bi, i: (bi, i, dl_col)), u)
    if has_vres:
        v_first, vres_col, vres_b, vres_v0 = vres
        add('vf', pl.BlockSpec((1, tb, W_A), lambda bi, i: (bi, i, 0)), v_first)
        add('lo', pl.BlockSpec((1, tb, LANES), lambda bi, i: (bi, i, vres_col)), u)
        add('vres_b', pl.BlockSpec((LANES, W_A), lambda bi, i: (0, 0)), vres_b)
        add('vres_v0', pl.BlockSpec((1, W_A), lambda bi, i: (0, 0)), vres_v0)
    vec = lambda rows: pl.BlockSpec((rows, W_A), lambda bi, i: (0, 0))
    lora = pl.BlockSpec((2, LANES, W_A), lambda bi, i: (0, 0, 0))
    add('tok_mu', vec(3), lp['tok_mu'])
    add('decay_w0', vec(2), lp['decay_w0'])
    add('decay_lora', lora, lp['decay_lora'])
    add('iclr_a0', vec(2), lp['iclr_a0'])
    add('iclr_lora', lora, lp['iclr_lora'])
    add('key_k', vec(1), lp['key_k'])
    add('key_a', vec(2), lp['key_a'])
    add('bonus_rk', vec(2), lp['bonus_rk'])
    add('head_mask', pl.BlockSpec((GROUP_W, GROUP_W), lambda bi, i: (0, 0)), lp['head_mask'])

    out_names = ['ops', 'ptot']
    out_shape = [jax.ShapeDtypeStruct((2, N_OPERANDS, b, t, W_A), BF16),
                 jax.ShapeDtypeStruct((2, b, nt, SUBLANES, W_A), F32)]
    out_specs = [pl.BlockSpec((2, N_OPERANDS, 1, tb, W_A), lambda bi, i: (0, 0, bi, i, 0)),
                 pl.BlockSpec((2, 1, 1, SUBLANES, W_A), lambda bi, i: (0, bi, i, 0, 0))]
    for flag, nm in ((emit_out, 'bonus'), (emit_v, 'vmix')):
        if flag:
            out_names.append(nm)
            out_shape.append(jax.ShapeDtypeStruct((b, t, W_A), F32))
            out_specs.append(pl.BlockSpec((1, tb, W_A), lambda bi, i: (bi, i, 0)))

    outs = pl.pallas_call(
        functools.partial(_prep_kernel, names=tuple(names + out_names), tb=tb, grid_mode=grid_mode,
                          has_vres=has_vres, emit_out=emit_out, emit_v=emit_v),
        out_shape=out_shape,
        grid=(b, nt),
        in_specs=in_specs,
        out_specs=out_specs,
        compiler_params=_cparams(("arbitrary", "arbitrary")),
    )(*args)
    return dict(zip(out_names, outs))


def _scan_kernel(*refs, names, tb, n_groups, use_s0, emit_out):
    R = dict(zip(names, refs))
    s_ref = R['s_scr']
    i = pl.program_id(2)
    nt = pl.num_programs(2)

    @pl.when(i == 0)
    def _():
        if use_s0:
            s_ref[...] = R['s0'][:, 0]
        else:
            s_ref[...] = jnp.zeros_like(s_ref)

    lane1 = lax.broadcasted_iota(jnp.int32, (1, LANES), 1)
    half_f = (jnp.where(lane1 < HEAD_DIM, 1.0, 0.0).astype(F32), jnp.where(lane1 >= HEAD_DIM, 1.0, 0.0).astype(F32))
    half_b = (half_f[0].astype(BF16), half_f[1].astype(BF16))
    zero_tile = jnp.zeros((HEAD_DIM, LANES), BF16)
    n_tiles = GROUP_W // LANES

    def bd(x):
        xb = x.astype(BF16)
        rows = []
        for j in range(GROUP_HEADS):
            t = (j * HEAD_DIM) // LANES
            piece = xb[:, t * LANES:(t + 1) * LANES] * half_b[(j * HEAD_DIM % LANES) // HEAD_DIM]
            rows.append(jnp.concatenate([piece if tt == t else zero_tile for tt in range(n_tiles)], axis=1))
        return jnp.concatenate(rows, axis=0)

    def head_blocks(full):
        tiles = []
        for t in range(n_tiles):
            acc = None
            for j in range(GROUP_HEADS):
                if (j * HEAD_DIM) // LANES == t:
                    part = (full[j * HEAD_DIM:(j + 1) * HEAD_DIM, t * LANES:(t + 1) * LANES]
                            * half_f[(j * HEAD_DIM % LANES) // HEAD_DIM])
                    acc = part if acc is None else acc + part
            tiles.append(acc)
        return jnp.concatenate(tiles, axis=1)

    def prod(a, b):
        return _dot(a.astype(BF16), bd(b))

    trow = lax.broadcasted_iota(jnp.int32, (CHUNK, GROUP_W), 0)
    scol = jnp.bitwise_and(lax.broadcasted_iota(jnp.int32, (CHUNK, GROUP_W), 1), CHUNK - 1)
    eye = jnp.where(trow == scol, 1.0, 0.0).astype(F32)
    same16 = (trow >> 4) == (scol >> 4)
    same32 = (trow >> 5) == (scol >> 5)
    tri_masks = []
    for before, upto in ((scol < trow, scol <= trow), (scol > trow, scol >= trow)):
        tri_masks.append(dict(
            neg_d16=jnp.where(jnp.logical_and(before, same16), -1.0, 0.0).astype(F32),
            o32=jnp.where(jnp.logical_and(before, jnp.logical_and(same32, jnp.logical_not(same16))), 1.0, 0.0).astype(BF16),
            o64=jnp.where(jnp.logical_and(before, jnp.logical_not(same32)), 1.0, 0.0).astype(BF16),
            strict=jnp.where(before, 1.0, 0.0).astype(F32),
            incl=jnp.where(upto, 1.0, 0.0).astype(F32)))

    n_chunk = tb // CHUNK
    s_cur = {(d, g): s_ref[d, g] for d in (0, 1) for g in range(n_groups)}

    def unit_stages(d, g, c):
        sl = slice(c * CHUNK, (c + 1) * CHUNK)
        cs = slice(g * GROUP_W, (g + 1) * GROUP_W)
        vc, r_h, kap_h, k_h, b_h, k_t, b_t = (R[f'ops{d}'][0, m, 0, sl, cs] for m in range(N_OPERANDS))
        p_tot = R[f'ptot{d}'][0, 0, 0, c:c + 1, cs]
        mk = tri_masks[d]
        x2 = jnp.concatenate([kap_h, r_h], axis=0)
        gb = _dot_nt(x2, bd(b_h))
        yield
        gk = _dot_nt(x2, bd(k_h))
        yield
        ab = gb[:CHUNK]
        abb = ab.astype(BF16)
        n1 = ab * mk['neg_d16']
        o32 = abb * mk['o32']
        o64 = abb * mk['o64']
        a_rb = (gb[CHUNK:] * mk['incl']).astype(BF16)
        a_kk = jnp.concatenate([gk[:CHUNK] * mk['strict'], gk[CHUNK:] * mk['incl']], axis=0).astype(BF16)
        m1y = _dot(a_kk, bd(vc))
        yield
        n1b = n1.astype(BF16)
        nk = prod(n1b, n1b).astype(BF16)
        x = eye + n1
        yield
        for _ in range(2):
            both = _dot(jnp.concatenate([nk, x.astype(BF16)], axis=0), bd(nk))
            nk = both[:CHUNK].astype(BF16)
            x = x + both[CHUNK:]
            yield
        x = x + prod(x, nk)
        yield
        for off in (o32, o64):
            xb = x.astype(BF16)
            xo = prod(xb, off).astype(BF16)
            yield
            x = x - prod(xo, xb)
            yield
        t16 = x.astype(BF16)
        w = _dot(t16, bd(kap_h)).astype(BF16)
        yield
        ut = _dot(t16, bd(m1y[:CHUNK]))
        yield
        s_in = s_cur[d, g]
        wr = _dot_nt(jnp.concatenate([w, r_h], axis=0) if emit_out else w, bd(s_in))
        yield
        u = (-(wr[:CHUNK] + ut)).astype(BF16)
        s_cur[d, g] = s_in * p_tot + head_blocks(_dot_tn(jnp.concatenate([u, vc], axis=0),
                                                         jnp.concatenate([b_t, k_t], axis=0)))
        yield
        if emit_out:
            R[f'y{d}'][0, sl, cs] = wr[CHUNK:] + prod(a_rb, u) + m1y[CHUNK:]

    jobs = [(pos * SCAN_CHUNK_SKEW, unit_stages(d, g, c)) for d in (0, 1) for g in range(n_groups)
            for pos, c in enumerate(range(n_chunk) if d == 0 else range(n_chunk - 1, -1, -1))]
    tick = 0
    while jobs:
        for entry in list(jobs):
            if tick >= entry[0]:
                try:
                    next(entry[1])
                except StopIteration:
                    jobs.remove(entry)
        tick += 1
    for (d, g), s_val in s_cur.items():
        s_ref[d, g] = s_val

    @pl.when(i == nt - 1)
    def _():
        R['sfin'][:, 0] = s_ref[...]


def _scan_call(prep, s0, emit_out):
    ops, ptot = prep['ops'], prep['ptot']
    _, _, b, t, _ = ops.shape
    tb = min(SCAN_TB, t)
    nt = t // tb
    ng = SCAN_GROUPS
    gw = ng * GROUP_W
    npair = N_GROUPS // ng
    use_s0 = s0 is not None
    tblk = (lambda i: i, lambda i: nt - 1 - i)
    names, in_specs, args = [], [], []

    def add(name, spec, arr):
        names.append(name)
        in_specs.append(spec)
        args.append(arr)

    for d in (0, 1):
        add(f'ops{d}', pl.BlockSpec((1, N_OPERANDS, 1, tb, gw), lambda bi, p, i, d=d: (d, 0, bi, tblk[d](i), p)), ops)
        add(f'ptot{d}', pl.BlockSpec((1, 1, 1, SUBLANES, gw), lambda bi, p, i, d=d: (d, bi, tblk[d](i), 0, p)), ptot)
    state_spec = pl.BlockSpec((2, 1, ng, HEAD_DIM, GROUP_W), lambda bi, p, i: (0, bi, p, 0, 0))
    if use_s0:
        add('s0', state_spec, s0)

    out_names, out_shape, out_specs = [], [], []
    if emit_out:
        for d in (0, 1):
            out_names.append(f'y{d}')
            out_shape.append(jax.ShapeDtypeStruct((b, t, W_A), F32))
            out_specs.append(pl.BlockSpec((1, tb, gw), lambda bi, p, i, d=d: (bi, tblk[d](i), p)))
    out_names.append('sfin')
    out_shape.append(jax.ShapeDtypeStruct((2, b, N_GROUPS, HEAD_DIM, GROUP_W), F32))
    out_specs.append(state_spec)

    outs = pl.pallas_call(
        functools.partial(_scan_kernel, names=tuple(names + out_names + ['s_scr']), tb=tb, n_groups=ng,
                          use_s0=use_s0, emit_out=emit_out),
        out_shape=out_shape,
        grid=(b, npair, nt),
        in_specs=in_specs,
        out_specs=out_specs,
        scratch_shapes=[pltpu.VMEM((2, ng, HEAD_DIM, GROUP_W), F32)],
        compiler_params=_cparams(("arbitrary", "arbitrary", "arbitrary")),
    )(*args)
    return dict(zip(out_names, outs))


def _post1_kernel(y0, y1, bo, ga, gb, pc, pp, pn, mg0, mg1, gnw, gnb, pw, ps, woa, wob, bm_ref, o_ref, *, tm, t_total):
    i = pl.program_id(1)
    nt = pl.num_programs(1)
    bm = bm_ref[...]

    parts = []
    for hg in range(N_GROUPS):
        cs = slice(hg * GROUP_W, (hg + 1) * GROUP_W)
        y = y0[0, :, cs] + y1[0, :, cs]
        mu = _headsums([y], bm)[0] * (1.0 / HEAD_DIM)
        yc = y - mu
        var = _headsums([yc * yc], bm)[0] * (1.0 / HEAD_DIM)
        yn = yc * lax.rsqrt(var + GN_EPS) * gnw[:, cs] + gnb[:, cs]
        yn = yn + bo[0, :, cs]
        gate = ga[0, :, cs]
        parts.append((yn * (gate * _sigmoid(gate))).astype(BF16))
    y_a = _dot(jnp.concatenate(parts, axis=1), woa[...])

    pscale = jnp.where(i > 0, 1.0, 0.0).astype(F32)
    nscale = jnp.where(i < nt - 1, 1.0, 0.0).astype(F32)
    n_ext = tm + 2 * POOL_HALO
    tpos = (lax.broadcasted_iota(jnp.int32, (tm, 1), 0) + i * tm).astype(F32)
    parts = []
    for gi, win in enumerate(POOL_WINDOWS):
        left = win // 2
        right = win - 1 - left
        cs = slice(gi * POOL_GROUP_W, (gi + 1) * POOL_GROUP_W)
        p = pc[0, :, cs]
        ext = jnp.concatenate([pp[0, :, cs] * pscale, p, pn[0, :, cs] * nscale], axis=0)
        s = ext
        step = 1
        while step < win:
            s = s + pltpu.roll(s, step, axis=0)
            step *= 2
        if right > 0:
            s = pltpu.roll(s, n_ext - right, axis=0)
        s = s[POOL_HALO:POOL_HALO + tm]
        cnt = jnp.minimum(tpos + (right + 1), float(t_total)) - jnp.maximum(tpos - left, 0.0)
        pooled = s / cnt - p
        mixed = _dot(pooled.astype(BF16), pw[gi]) * ps[:, cs]
        gate = gb[0, :, cs]
        parts.append((mixed * (gate * _sigmoid(gate))).astype(BF16))
    y_b = _dot(jnp.concatenate(parts, axis=1), wob[...])

    merged = _sigmoid(mg0[0]) * y_a + _sigmoid(mg1[0]) * y_b
    o_ref[0] = merged.astype(BF16)


def _post1_call(sc, bonus, u, cols, lp):
    y0, y1 = sc['y0'], sc['y1']
    b, t, _ = y0.shape
    d = lp['w_out_a'].shape[1]
    tm = min(256, t)
    nt = t // tm
    hb = tm // POOL_HALO
    nh = t // POOL_HALO
    blk = lambda c: pl.BlockSpec((1, tm, W_A), lambda bi, i: (bi, i, c))
    wide = lambda c: pl.BlockSpec((1, tm, d), lambda bi, i: (bi, i, c))
    const2 = lambda shape: pl.BlockSpec(shape, lambda bi, i: (0, 0))
    pool_col = cols['pool']
    in_specs = [blk(0), blk(0), blk(0),
                blk(cols['gate_a']), blk(cols['gate_b']), blk(pool_col),
                pl.BlockSpec((1, POOL_HALO, W_B), lambda bi, i: (bi, jnp.maximum(i * hb - 1, 0), pool_col)),
                pl.BlockSpec((1, POOL_HALO, W_B), lambda bi, i: (bi, jnp.minimum((i + 1) * hb, nh - 1), pool_col)),
                wide(cols['merge0']), wide(cols['merge1']),
                const2((1, W_A)), const2((1, W_A)),
                pl.BlockSpec((len(POOL_WINDOWS), POOL_GROUP_W, POOL_GROUP_W), lambda bi, i: (0, 0, 0)),
                const2((1, W_B)), const2((W_A, d)), const2((W_B, d)), const2((GROUP_W, GROUP_W))]
    args = [y0, y1, bonus, u, u, u, u, u, u, u,
            lp['gn_w'], lp['gn_b'], lp['pool_w'], lp['pool_scale'], lp['w_out_a'], lp['w_out_b'], lp['head_mask']]
    return pl.pallas_call(
        functools.partial(_post1_kernel, tm=tm, t_total=t),
        out_shape=jax.ShapeDtypeStruct((b, t, d), BF16),
        grid=(b, nt),
        in_specs=in_specs,
        out_specs=pl.BlockSpec((1, tm, d), lambda bi, i: (bi, i, 0)),
        compiler_params=_cparams(("arbitrary", "arbitrary")),
    )(*args)


def _post2_kernel(m_ref, h_ref, gate_ref, w_ref, fg_ref, o_ref, *, final_norm):
    out = _dot(m_ref[0], w_ref[...])
    h = h_ref[0] + gate_ref[0] * out
    if final_norm:
        ms = jnp.mean(h * h, axis=-1, keepdims=True)
        h = h * lax.rsqrt(ms + NORM_EPS) * fg_ref[...]
    o_ref[0] = h


def _post2_call(merged, h, gate, w_out, final_g, final_norm):
    bm, t, d = h.shape
    tm = min(512, t)
    return pl.pallas_call(
        functools.partial(_post2_kernel, final_norm=final_norm),
        out_shape=jax.ShapeDtypeStruct((bm, t, d), F32),
        grid=(bm, t // tm),
        in_specs=[
            pl.BlockSpec((1, tm, d), lambda b, m: (b, m, 0)),
            pl.BlockSpec((1, tm, d), lambda b, m: (b, m, 0)),
            pl.BlockSpec((1, 1, d), lambda b, m: (b, 0, 0)),
            pl.BlockSpec((d, d), lambda b, m: (0, 0)),
            pl.BlockSpec((1, d), lambda b, m: (0, 0)),
        ],
        out_specs=pl.BlockSpec((1, tm, d), lambda b, m: (b, m, 0)),
        compiler_params=_cparams(("arbitrary", "arbitrary")),
    )(merged, h, gate, w_out, final_g)


def _pad_rows(w, rows):
    return jnp.pad(w, ((0, 0),) * (w.ndim - 2) + ((0, rows - w.shape[-2]), (0, 0)))


def kernel(x, c, ctx, c_ctx, ada_w, ada_b, norm_g, w_in, tok_mu, decay_w0, decay_lora_b, iclr_a0, iclr_lora_b, key_k, key_a, bonus_rk, gn_w, gn_b, vres_v0, vres_lora_a, vres_lora_b, pool_w, pool_scale, w_out_a, w_out_b, w_out, final_g):
    depth = w_in.shape[0]
    batch, seq, d = x.shape
    ctx_len = ctx.shape[1]
    assert seq % GRID_W == 0 and seq % CHUNK == 0 and ctx_len % CHUNK == 0
    assert d % LANES == 0

    off_merge = 3 * W_A + W_B + W_A + W_B
    off_dl = off_merge + 2 * d
    off_vres = off_dl + 2 * R_DECAY + 2 * R_ICLR
    nu = -(-(off_vres + LANES) // INPROJ_TN) * INPROJ_TN
    cols = dict(pool=REF_OFF_POOL // W_B, gate_a=REF_OFF_GATE_A // W_A, gate_b=REF_OFF_GATE_B // W_A,
                merge0=off_merge // d, merge1=off_merge // d + 1)
    assert off_merge % d == 0 and off_dl % GROUP_W == 0 and off_vres % LANES == 0

    head_mask = (jnp.arange(GROUP_W)[:, None] // HEAD_DIM == jnp.arange(GROUP_W)[None, :] // HEAD_DIM).astype(BF16)

    rows = -(-(batch + 1) // SUBLANES) * SUBLANES
    c_all = jnp.zeros((rows, d), F32).at[:batch].set(c).at[batch].set(c_ctx)
    mod = _ada_call(c_all, ada_w, ada_b)

    h_lat, h_ctx = x, ctx
    vfirst = {}
    for l in range(depth):
        last = l == depth - 1
        w_l = w_in[l]
        pieces = [w_l[:, :REF_OFF_DECAY], w_l[:, REF_OFF_MERGE:REF_OFF_MERGE + 2 * d],
                  w_l[:, REF_OFF_DECAY:REF_OFF_MERGE]]
        if l > 0:
            pieces.append(vres_lora_a[l - 1])
        w_perm = jnp.concatenate(pieces, axis=1)
        w_perm = jnp.pad(w_perm, ((0, 0), (0, nu - w_perm.shape[1]))).astype(BF16)

        dlw = decay_lora_b[l]
        dlw = jnp.stack([jnp.pad(dlw[0], ((0, LANES - R_DECAY), (0, 0))),
                         jnp.pad(dlw[1], ((R_DECAY, LANES - 2 * R_DECAY), (0, 0)))]).astype(BF16)
        ilw = iclr_lora_b[l]
        ilw = jnp.stack([jnp.pad(ilw[0], ((0, LANES - R_ICLR), (0, 0))),
                         jnp.pad(ilw[1], ((R_ICLR, LANES - 2 * R_ICLR), (0, 0)))]).astype(BF16)
        lp = dict(tok_mu=tok_mu[l], decay_w0=decay_w0[l], decay_lora=dlw, iclr_a0=iclr_a0[l], iclr_lora=ilw,
                  key_k=key_k[l].reshape(1, W_A), key_a=key_a[l], bonus_rk=bonus_rk[l].reshape(2, W_A),
                  head_mask=head_mask, gn_w=gn_w[l].reshape(1, W_A), gn_b=gn_b[l].reshape(1, W_A),
                  pool_w=pool_w[l].astype(BF16), pool_scale=pool_scale[l].reshape(1, W_B),
                  w_out_a=w_out_a[l].astype(BF16), w_out_b=w_out_b[l].astype(BF16))
        w_out_l = w_out[l].astype(BF16)
        g_l = norm_g[l].reshape(1, d)

        m = mod[l]
        shift_l, scale_l, gate_l = (m[:batch, j * d:(j + 1) * d].reshape(batch, 1, d) for j in range(3))
        shift_c, scale_c, gate_c = (m[batch:batch + 1, j * d:(j + 1) * d].reshape(1, 1, d) for j in range(3))

        u_c = _inproj_call(h_ctx.reshape(1, batch * ctx_len, d), shift_c, scale_c, g_l, w_perm)
        u_c = u_c.reshape(batch, ctx_len, nu)
        u_l = _inproj_call(h_lat, shift_l, scale_l, g_l, w_perm)

        vres_c = vres_l = None
        if l > 0:
            vb = _pad_rows(vres_lora_b[l - 1], LANES).astype(BF16)
            v0 = vres_v0[l - 1].reshape(1, W_A)
            vres_c = (vfirst['c'], off_vres // LANES, vb, v0)
            vres_l = (vfirst['l'], off_vres // LANES, vb, v0)
        keep_v = l == 0 and depth > 1
        prep_c = _prep_call(u_c, off_dl // GROUP_W, lp, False, not last, vres_c, keep_v)
        prep_l = _prep_call(u_l, off_dl // GROUP_W, lp, True, True, vres_l, keep_v)
        if keep_v:
            vfirst = dict(c=prep_c['vmix'], l=prep_l['vmix'])
        sc_c = _scan_call(prep_c, None, not last)
        sc_l = _scan_call(prep_l, sc_c['sfin'], True)

        merged_l = _post1_call(sc_l, prep_l['bonus'], u_l, cols, lp)
        h_lat = _post2_call(merged_l, h_lat, gate_l, w_out_l, final_g.reshape(1, d), last)
        if not last:
            merged_c = _post1_call(sc_c, prep_c['bonus'], u_c, cols, lp)
            h_ctx = _post2_call(merged_c.reshape(1, batch * ctx_len, d), h_ctx.reshape(1, batch * ctx_len, d),
                                gate_c, w_out_l, final_g.reshape(1, d), False).reshape(batch, ctx_len, d)
    return h_lat
```

```python
import functools
import math

import jax
import jax.numpy as jnp
from jax import lax
from jax.experimental import pallas as pl
from jax.experimental.pallas import tpu as pltpu

F32 = jnp.float32
BF16 = jnp.bfloat16

GRID_W = 64
W_A = 1024
W_B = 1024
HEAD_DIM = 64
POOL_WINDOWS = (2, 4, 8, 16)
POOL_GROUP_W = W_B // len(POOL_WINDOWS)
R_DECAY = 64
R_ICLR = 64
R_VRES = 32
NORM_EPS = 1e-6
GN_EPS = HEAD_DIM * 1e-5
REF_OFF_POOL = 3 * W_A
REF_OFF_GATE_A = REF_OFF_POOL + W_B
REF_OFF_GATE_B = REF_OFF_GATE_A + W_A
REF_OFF_DECAY = REF_OFF_GATE_B + W_B
REF_OFF_ICLR = REF_OFF_DECAY + 2 * R_DECAY
REF_OFF_MERGE = REF_OFF_ICLR + 2 * R_ICLR

LANES = 128
SUBLANES = 8
MXU_DIM = 256
VMEM_LIMIT_BYTES = 56 * 1024 * 1024

GROUP_W = MXU_DIM
GROUP_HEADS = GROUP_W // HEAD_DIM
N_GROUPS = W_A // GROUP_W
SCAN_GROUPS = 2
SCAN_TB = 256
CHUNK = 64
SCAN_CHUNK_SKEW = 2
N_OPERANDS = 7
HALO = 64
POOL_HALO = 8
INPROJ_TN = 1536
LOG_DECAY_SCALE = math.exp(-0.5)


def _cparams(sem):
    return pltpu.CompilerParams(dimension_semantics=sem, vmem_limit_bytes=VMEM_LIMIT_BYTES)


def _sigmoid(x):
    return 1.0 / (1.0 + jnp.exp(-x))


def _dot(a, b):
    return jnp.dot(a, b, preferred_element_type=F32)


def _dot_nt(a, b):
    return lax.dot_general(a, b, (((1,), (1,)), ((), ())), preferred_element_type=F32)


def _dot_tn(a, b):
    return lax.dot_general(a, b, (((0,), (0,)), ((), ())), preferred_element_type=F32)


def _headsums(xs, bm):
    rows = xs[0].shape[0]
    s = _dot(jnp.concatenate([x.astype(BF16) for x in xs], axis=0), bm)
    return [s[n * rows:(n + 1) * rows] for n in range(len(xs))]


def _ada_kernel(c_ref, w_ref, b_ref, o_ref):
    c = c_ref[...]
    s = (c * _sigmoid(c)).astype(BF16)
    o_ref[0] = _dot(s, w_ref[0].astype(BF16)) + b_ref[0]


def _ada_call(c_all, ada_w, ada_b):
    n_layers, d, d3 = ada_w.shape
    rows = c_all.shape[0]
    tn = 768
    return pl.pallas_call(
        _ada_kernel,
        out_shape=jax.ShapeDtypeStruct((n_layers, rows, d3), F32),
        grid=(n_layers, d3 // tn),
        in_specs=[
            pl.BlockSpec((rows, d), lambda l, n: (0, 0)),
            pl.BlockSpec((1, d, tn), lambda l, n: (l, 0, n)),
            pl.BlockSpec((1, 1, tn), lambda l, n: (l, 0, n)),
        ],
        out_specs=pl.BlockSpec((1, rows, tn), lambda l, n: (l, 0, n)),
        compiler_params=_cparams(("arbitrary", "arbitrary")),
    )(c_all, ada_w, ada_b.reshape(n_layers, 1, d3))


def _inproj_kernel(h_ref, shift_ref, scale_ref, g_ref, w_ref, o_ref, xn_ref):
    @pl.when(pl.program_id(2) == 0)
    def _():
        x = h_ref[0]
        ms = jnp.mean(x * x, axis=-1, keepdims=True)
        y = x * lax.rsqrt(ms + NORM_EPS) * g_ref[...]
        xn_ref[...] = (y * (1.0 + scale_ref[0]) + shift_ref[0]).astype(BF16)

    o_ref[0] = _dot(xn_ref[...], w_ref[...])


def _inproj_call(h, shift, scale, g, w):
    bm, t, d = h.shape
    nu = w.shape[1]
    tm = min(1024, t)
    tn = INPROJ_TN
    assert nu % tn == 0
    return pl.pallas_call(
        _inproj_kernel,
        out_shape=jax.ShapeDtypeStruct((bm, t, nu), F32),
        grid=(bm, t // tm, nu // tn),
        in_specs=[
            pl.BlockSpec((1, tm, d), lambda b, m, n: (b, m, 0)),
            pl.BlockSpec((1, 1, d), lambda b, m, n: (b, 0, 0)),
            pl.BlockSpec((1, 1, d), lambda b, m, n: (b, 0, 0)),
            pl.BlockSpec((1, d), lambda b, m, n: (0, 0)),
            pl.BlockSpec((d, tn), lambda b, m, n: (0, n)),
        ],
        out_specs=pl.BlockSpec((1, tm, tn), lambda b, m, n: (b, m, n)),
        scratch_shapes=[pltpu.VMEM((tm, d), BF16)],
        compiler_params=_cparams(("arbitrary", "arbitrary", "arbitrary")),
    )(h, shift, scale, g, w)


def _shift_plan(grid_mode, hg):
    if grid_mode:
        return ((-1, 'first'), (1, 'last'), (-GRID_W, None), (GRID_W, None))[hg]
    return ((-1, None), (-1, None), (1, None), (1, None))[hg]


def _prep_kernel(*refs, names, tb, grid_mode, has_vres, emit_out, emit_v):
    R = dict(zip(names, refs))
    i = pl.program_id(1)
    nt = pl.num_programs(1)
    ps = jnp.where(i > 0, 1.0, 0.0).astype(F32)
    ns = jnp.where(i < nt - 1, 1.0, 0.0).astype(F32)
    n_ext = tb + 2 * HALO
    trow1 = lax.broadcasted_iota(jnp.int32, (tb, 1), 0)
    gcol = jnp.bitwise_and(trow1, GRID_W - 1)
    col_mask = dict(first=jnp.where(gcol != 0, 1.0, 0.0).astype(F32),
                    last=jnp.where(gcol != GRID_W - 1, 1.0, 0.0).astype(F32))
    bm = R['head_mask'][...]
    brow = lax.broadcasted_iota(jnp.int32, (tb, tb), 0)
    bcol = lax.broadcasted_iota(jnp.int32, (tb, tb), 1)
    same_chunk = (brow >> 6) == (bcol >> 6)
    tris = (jnp.where(jnp.logical_and(same_chunk, bcol <= brow), 1.0, 0.0).astype(BF16),
            jnp.where(jnp.logical_and(same_chunk, bcol >= brow), 1.0, 0.0).astype(BF16))
    n_chunk = tb // CHUNK
    R['ptot'][...] = jnp.zeros_like(R['ptot'])

    def mixed(hg, m):
        off, cmask = _shift_plan(grid_mode, hg)
        cs = slice(m * W_A + hg * GROUP_W, m * W_A + (hg + 1) * GROUP_W)
        x = R['cur'][0, :, cs]
        if off == -GRID_W:
            sh = jnp.concatenate([R['prv'][0, :, cs] * ps, x[:tb - HALO]], axis=0)
        elif off == GRID_W:
            sh = jnp.concatenate([x[HALO:], R['nxt'][0, :, cs] * ns], axis=0)
        else:
            ext = jnp.concatenate([R['prv'][0, :, cs] * ps, x, R['nxt'][0, :, cs] * ns], axis=0)
            sh = pltpu.roll(ext, 1 if off == -1 else n_ext - 1, axis=0)[HALO:HALO + tb]
        if cmask is not None:
            sh = sh * col_mask[cmask]
        return x + R['tok_mu'][m:m + 1, hg * GROUP_W:(hg + 1) * GROUP_W] * (sh - x)

    dl = R['dl'][0]
    dlo = jnp.tanh(dl[:, :LANES]).astype(BF16)
    alo = dl[:, LANES:].astype(BF16)
    for hg in range(N_GROUPS):
        cs = slice(hg * GROUP_W, (hg + 1) * GROUP_W)
        r = mixed(hg, 0)
        k = mixed(hg, 1)
        v = mixed(hg, 2)
        if has_vres:
            gate = _sigmoid(R['vres_v0'][:, cs] + _dot(R['lo'][0].astype(BF16), R['vres_b'][:, cs]))
            v = v + (R['vf'][0, :, cs] - v) * gate
        if emit_v:
            R['vmix'][0, :, cs] = v
        kk = k * R['key_k'][:, cs]
        a_d, kd_d, sums_in = [], [], [kk * kk]
        for d in (0, 1):
            a = _sigmoid(R['iclr_a0'][d:d + 1, cs] + _dot(alo, R['iclr_lora'][d, :, cs]))
            kd = k * (1.0 + (a - 1.0) * R['key_a'][d:d + 1, cs])
            a_d.append(a)
            kd_d.append(kd)
            if emit_out:
                sums_in.append(r * kd * R['bonus_rk'][d:d + 1, cs])
        sums = _headsums(sums_in, bm)
        kk = kk / jnp.maximum(jnp.sqrt(sums[0]), 1e-12)
        if emit_out:
            R['bonus'][0, :, cs] = (sums[1] + sums[2]) * v
        vb = v.astype(BF16)
        for d in (0, 1):
            z = R['decay_w0'][d:d + 1, cs] + _dot(dlo, R['decay_lora'][d, :, cs])
            lg = -LOG_DECAY_SCALE * _sigmoid(z)
            g1 = lg.astype(BF16)
            g2 = (lg - g1.astype(F32)).astype(BF16)
            cum = _dot(tris[d], g1) + _dot(tris[d], g2)
            bb = kk * a_d[d]
            kd = kd_d[d]
            for c in range(n_chunk):
                sl = slice(c * CHUNK, (c + 1) * CHUNK)
                lc = cum[sl]
                ltot = lc[CHUNK - 1:CHUNK] if d == 0 else lc[0:1]
                e_pos = jnp.exp(lc)
                e_neg = 1.0 / e_pos
                p_tot = jnp.exp(ltot)
                e_tail = p_tot * e_neg
                R['ptot'][d, 0, 0, c:c + 1, cs] = p_tot
                operands = (None, r[sl] * e_pos, kk[sl] * jnp.exp(lc - lg[sl]),
                            kd[sl] * e_neg, bb[sl] * e_neg, kd[sl] * e_tail, bb[sl] * e_tail)
                R['ops'][d, 0, 0, sl, cs] = vb[sl]
                for m in range(1, N_OPERANDS):
                    R['ops'][d, m, 0, sl, cs] = operands[m].astype(BF16)


def _prep_call(u, dl_col, lp, grid_mode, emit_out, vres=None, emit_v=False):
    b, t, _ = u.shape
    tb = min(SCAN_TB, t)
    nt = t // tb
    hb = tb // HALO
    nh = t // HALO
    w3 = 3 * W_A
    has_vres = vres is not None
    assert GROUP_W == W_A // 4 and tb % HALO == 0 and tb % CHUNK == 0 and tb // CHUNK <= SUBLANES
    names, in_specs, args = [], [], []

    def add(name, spec, arr):
        names.append(name)
        in_specs.append(spec)
        args.append(arr)

    add('cur', pl.BlockSpec((1, tb, w3), lambda bi, i: (bi, i, 0)), u)
    add('prv', pl.BlockSpec((1, HALO, w3), lambda bi, i: (bi, jnp.maximum(i * hb - 1, 0), 0)), u)
    add('nxt', pl.BlockSpec((1, HALO, w3), lambda bi, i: (bi, jnp.minimum((i + 1) * hb, nh - 1), 0)), u)
    add('dl', pl.BlockSpec((1, tb, GROUP_W), lambda bi, i: (bi, i, dl_col)), u)
    if has_vres:
        v_first, vres_col, vres_b, vres_v0 = vres
        add('vf', pl.BlockSpec((1, tb, W_A), lambda bi, i: (bi, i, 0)), v_first)
        add('lo', pl.BlockSpec((1, tb, LANES), lambda bi, i: (bi, i, vres_col)), u)
        add('vres_b', pl.BlockSpec((LANES, W_A), lambda bi, i: (0, 0)), vres_b)
        add('vres_v0', pl.BlockSpec((1, W_A), lambda bi, i: (0, 0)), vres_v0)
    vec = lambda rows: pl.BlockSpec((rows, W_A), lambda bi, i: (0, 0))
    lora = pl.BlockSpec((2, LANES, W_A), lambda bi, i: (0, 0, 0))
    add('tok_mu', vec(3), lp['tok_mu'])
    add('decay_w0', vec(2), lp['decay_w0'])
    add('decay_lora', lora, lp['decay_lora'])
    add('iclr_a0', vec(2), lp['iclr_a0'])
    add('iclr_lora', lora, lp['iclr_lora'])
    add('key_k', vec(1), lp['key_k'])
    add('key_a', vec(2), lp['key_a'])
    add('bonus_rk', vec(2), lp['bonus_rk'])
    add('head_mask', pl.BlockSpec((GROUP_W, GROUP_W), lambda bi, i: (0, 0)), lp['head_mask'])

    out_names = ['ops', 'ptot']
    out_shape = [jax.ShapeDtypeStruct((2, N_OPERANDS, b, t, W_A), BF16),
                 jax.ShapeDtypeStruct((2, b, nt, SUBLANES, W_A), F32)]
    out_specs = [pl.BlockSpec((2, N_OPERANDS, 1, tb, W_A), lambda bi, i: (0, 0, bi, i, 0)),
                 pl.BlockSpec((2, 1, 1, SUBLANES, W_A), lambda bi, i: (0, bi, i, 0, 0))]
    for flag, nm in ((emit_out, 'bonus'), (emit_v, 'vmix')):
        if flag:
            out_names.append(nm)
            out_shape.append(jax.ShapeDtypeStruct((b, t, W_A), F32))
            out_specs.append(pl.BlockSpec((1, tb, W_A), lambda bi, i: (bi, i, 0)))

    outs = pl.pallas_call(
        functools.partial(_prep_kernel, names=tuple(names + out_names), tb=tb, grid_mode=grid_mode,
                          has_vres=has_vres, emit_out=emit_out, emit_v=emit_v),
        out_shape=out_shape,
        grid=(b, nt),
        in_specs=in_specs,
        out_specs=out_specs,
        compiler_params=_cparams(("arbitrary", "arbitrary")),
    )(*args)
    return dict(zip(out_names, outs))


def _scan_kernel(*refs, names, tb, n_groups, use_s0, emit_out):
    R = dict(zip(names, refs))
    s_ref = R['s_scr']
    i = pl.program_id(2)
    nt = pl.num_programs(2)

    @pl.when(i == 0)
    def _():
        if use_s0:
            s_ref[...] = R['s0'][:, 0]
        else:
            s_ref[...] = jnp.zeros_like(s_ref)

    lane1 = lax.broadcasted_iota(jnp.int32, (1, LANES), 1)
    half_f = (jnp.where(lane1 < HEAD_DIM, 1.0, 0.0).astype(F32), jnp.where(lane1 >= HEAD_DIM, 1.0, 0.0).astype(F32))
    half_b = (half_f[0].astype(BF16), half_f[1].astype(BF16))
    zero_tile = jnp.zeros((HEAD_DIM, LANES), BF16)
    n_tiles = GROUP_W // LANES

    def bd(x):
        xb = x.astype(BF16)
        rows = []
        for j in range(GROUP_HEADS):
            t = (j * HEAD_DIM) // LANES
            piece = xb[:, t * LANES:(t + 1) * LANES] * half_b[(j * HEAD_DIM % LANES) // HEAD_DIM]
            rows.append(jnp.concatenate([piece if tt == t else zero_tile for tt in range(n_tiles)], axis=1))
        return jnp.concatenate(rows, axis=0)

    def head_blocks(full):
        tiles = []
        for t in range(n_tiles):
            acc = None
            for j in range(GROUP_HEADS):
                if (j * HEAD_DIM) // LANES == t:
                    part = (full[j * HEAD_DIM:(j + 1) * HEAD_DIM, t * LANES:(t + 1) * LANES]
                            * half_f[(j * HEAD_DIM % LANES) // HEAD_DIM])
                    acc = part if acc is None else acc + part
            tiles.append(acc)
        return jnp.concatenate(tiles, axis=1)

    def prod(a, b):
        return _dot(a.astype(BF16), bd(b))

    trow = lax.broadcasted_iota(jnp.int32, (CHUNK, GROUP_W), 0)
    scol = jnp.bitwise_and(lax.broadcasted_iota(jnp.int32, (CHUNK, GROUP_W), 1), CHUNK - 1)
    eye = jnp.where(trow == scol, 1.0, 0.0).astype(F32)
    same16 = (trow >> 4) == (scol >> 4)
    same32 = (trow >> 5) == (scol >> 5)
    tri_masks = []
    for before, upto in ((scol < trow, scol <= trow), (scol > trow, scol >= trow)):
        tri_masks.append(dict(
            neg_d16=jnp.where(jnp.logical_and(before, same16), -1.0, 0.0).astype(F32),
            o32=jnp.where(jnp.logical_and(before, jnp.logical_and(same32, jnp.logical_not(same16))), 1.0, 0.0).astype(BF16),
            o64=jnp.where(jnp.logical_and(before, jnp.logical_not(same32)), 1.0, 0.0).astype(BF16),
            strict=jnp.where(before, 1.0, 0.0).astype(F32),
            incl=jnp.where(upto, 1.0, 0.0).astype(F32)))

    n_chunk = tb // CHUNK
    s_cur = {(d, g): s_ref[d, g] for d in (0, 1) for g in range(n_groups)}

    def unit_stages(d, g, c):
        sl = slice(c * CHUNK, (c + 1) * CHUNK)
        cs = slice(g * GROUP_W, (g + 1) * GROUP_W)
        vc, r_h, kap_h, k_h, b_h, k_t, b_t = (R[f'ops{d}'][0, m, 0, sl, cs] for m in range(N_OPERANDS))
        p_tot = R[f'ptot{d}'][0, 0, 0, c:c + 1, cs]
        mk = tri_masks[d]
        x2 = jnp.concatenate([kap_h, r_h], axis=0)
        gb = _dot_nt(x2, bd(b_h))
        yield
        gk = _dot_nt(x2, bd(k_h))
        yield
        ab = gb[:CHUNK]
        abb = ab.astype(BF16)
        n1 = ab * mk['neg_d16']
        o32 = abb * mk['o32']
        o64 = abb * mk['o64']
        a_rb = (gb[CHUNK:] * mk['incl']).astype(BF16)
        a_kk = jnp.concatenate([gk[:CHUNK] * mk['strict'], gk[CHUNK:] * mk['incl']], axis=0).astype(BF16)
        m1y = _dot(a_kk, bd(vc))
        yield
        n1b = n1.astype(BF16)
        nk = prod(n1b, n1b).astype(BF16)
        x = eye + n1
        yield
        for _ in range(2):
            both = _dot(jnp.concatenate([nk, x.astype(BF16)], axis=0), bd(nk))
            nk = both[:CHUNK].astype(BF16)
            x = x + both[CHUNK:]
            yield
        x = x + prod(x, nk)
        yield
        for off in (o32, o64):
            xb = x.astype(BF16)
            xo = prod(xb, off).astype(BF16)
            yield
            x = x - prod(xo, xb)
            yield
        t16 = x.astype(BF16)
        w = _dot(t16, bd(kap_h)).astype(BF16)
        yield
        ut = _dot(t16, bd(m1y[:CHUNK]))
        yield
        s_in = s_cur[d, g]
        wr = _dot_nt(jnp.concatenate([w, r_h], axis=0) if emit_out else w, bd(s_in))
        yield
        u = (-(wr[:CHUNK] + ut)).astype(BF16)
        s_cur[d, g] = s_in * p_tot + head_blocks(_dot_tn(jnp.concatenate([u, vc], axis=0),
                                                         jnp.concatenate([b_t, k_t], axis=0)))
        yield
        if emit_out:
            R[f'y{d}'][0, sl, cs] = wr[CHUNK:] + prod(a_rb, u) + m1y[CHUNK:]

    jobs = [(pos * SCAN_CHUNK_SKEW, unit_stages(d, g, c)) for d in (0, 1) for g in range(n_groups)
            for pos, c in enumerate(range(n_chunk) if d == 0 else range(n_chunk - 1, -1, -1))]
    tick = 0
    while jobs:
        for entry in list(jobs):
            if tick >= entry[0]:
                try:
                    next(entry[1])
                except StopIteration:
                    jobs.remove(entry)
        tick += 1
    for (d, g), s_val in s_cur.items():
        s_ref[d, g] = s_val

    @pl.when(i == nt - 1)
    def _():
        R['sfin'][:, 0] = s_ref[...]


def _scan_call(prep, s0, emit_out):
    ops, ptot = prep['ops'], prep['ptot']
    _, _, b, t, _ = ops.shape
    tb = min(SCAN_TB, t)
    nt = t // tb
    ng = SCAN_GROUPS
    gw = ng * GROUP_W
    npair = N_GROUPS // ng
    use_s0 = s0 is not None
    tblk = (lambda i: i, lambda i: nt - 1 - i)
    names, in_specs, args = [], [], []

    def add(name, spec, arr):
        names.append(name)
        in_specs.append(spec)
        args.append(arr)

    for d in (0, 1):
        add(f'ops{d}', pl.BlockSpec((1, N_OPERANDS, 1, tb, gw), lambda bi, p, i, d=d: (d, 0, bi, tblk[d](i), p)), ops)
        add(f'ptot{d}', pl.BlockSpec((1, 1, 1, SUBLANES, gw), lambda bi, p, i, d=d: (d, bi, tblk[d](i), 0, p)), ptot)
    state_spec = pl.BlockSpec((2, 1, ng, HEAD_DIM, GROUP_W), lambda bi, p, i: (0, bi, p, 0, 0))
    if use_s0:
        add('s0', state_spec, s0)

    out_names, out_shape, out_specs = [], [], []
    if emit_out:
        for d in (0, 1):
            out_names.append(f'y{d}')
            out_shape.append(jax.ShapeDtypeStruct((b, t, W_A), F32))
            out_specs.append(pl.BlockSpec((1, tb, gw), lambda bi, p, i, d=d: (bi, tblk[d](i), p)))
    out_names.append('sfin')
    out_shape.append(jax.ShapeDtypeStruct((2, b, N_GROUPS, HEAD_DIM, GROUP_W), F32))
    out_specs.append(state_spec)

    outs = pl.pallas_call(
        functools.partial(_scan_kernel, names=tuple(names + out_names + ['s_scr']), tb=tb, n_groups=ng,
                          use_s0=use_s0, emit_out=emit_out),
        out_shape=out_shape,
        grid=(b, npair, nt),
        in_specs=in_specs,
        out_specs=out_specs,
        scratch_shapes=[pltpu.VMEM((2, ng, HEAD_DIM, GROUP_W), F32)],
        compiler_params=_cparams(("arbitrary", "arbitrary", "arbitrary")),
    )(*args)
    return dict(zip(out_names, outs))


def _post1_kernel(y0, y1, bo, ga, gb, pc, pp, pn, mg0, mg1, gnw, gnb, pw, ps, woa, wob, bm_ref, o_ref, *, tm, t_total):
    i = pl.program_id(1)
    nt = pl.num_programs(1)
    bm = bm_ref[...]

    parts = []
    for hg in range(N_GROUPS):
        cs = slice(hg * GROUP_W, (hg + 1) * GROUP_W)
        y = y0[0, :, cs] + y1[0, :, cs]
        mu = _headsums([y], bm)[0] * (1.0 / HEAD_DIM)
        yc = y - mu
        var = _headsums([yc * yc], bm)[0] * (1.0 / HEAD_DIM)
        yn = yc * lax.rsqrt(var + GN_EPS) * gnw[:, cs] + gnb[:, cs]
        yn = yn + bo[0, :, cs]
        gate = ga[0, :, cs]
        parts.append((yn * (gate * _sigmoid(gate))).astype(BF16))
    y_a = _dot(jnp.concatenate(parts, axis=1), woa[...])

    pscale = jnp.where(i > 0, 1.0, 0.0).astype(F32)
    nscale = jnp.where(i < nt - 1, 1.0, 0.0).astype(F32)
    n_ext = tm + 2 * POOL_HALO
    tpos = (lax.broadcasted_iota(jnp.int32, (tm, 1), 0) + i * tm).astype(F32)
    parts = []
    for gi, win in enumerate(POOL_WINDOWS):
        left = win // 2
        right = win - 1 - left
        cs = slice(gi * POOL_GROUP_W, (gi + 1) * POOL_GROUP_W)
        p = pc[0, :, cs]
        ext = jnp.concatenate([pp[0, :, cs] * pscale, p, pn[0, :, cs] * nscale], axis=0)
        s = ext
        step = 1
        while step < win:
            s = s + pltpu.roll(s, step, axis=0)
            step *= 2
        if right > 0:
            s = pltpu.roll(s, n_ext - right, axis=0)
        s = s[POOL_HALO:POOL_HALO + tm]
        cnt = jnp.minimum(tpos + (right + 1), float(t_total)) - jnp.maximum(tpos - left, 0.0)
        pooled = s / cnt - p
        mixed = _dot(pooled.astype(BF16), pw[gi]) * ps[:, cs]
        gate = gb[0, :, cs]
        parts.append((mixed * (gate * _sigmoid(gate))).astype(BF16))
    y_b = _dot(jnp.concatenate(parts, axis=1), wob[...])

    merged = _sigmoid(mg0[0]) * y_a + _sigmoid(mg1[0]) * y_b
    o_ref[0] = merged.astype(BF16)


def _post1_call(sc, bonus, u, cols, lp):
    y0, y1 = sc['y0'], sc['y1']
    b, t, _ = y0.shape
    d = lp['w_out_a'].shape[1]
    tm = min(256, t)
    nt = t // tm
    hb = tm // POOL_HALO
    nh = t // POOL_HALO
    blk = lambda c: pl.BlockSpec((1, tm, W_A), lambda bi, i: (bi, i, c))
    wide = lambda c: pl.BlockSpec((1, tm, d), lambda bi, i: (bi, i, c))
    const2 = lambda shape: pl.BlockSpec(shape, lambda bi, i: (0, 0))
    pool_col = cols['pool']
    in_specs = [blk(0), blk(0), blk(0),
                blk(cols['gate_a']), blk(cols['gate_b']), blk(pool_col),
                pl.BlockSpec((1, POOL_HALO, W_B), lambda bi, i: (bi, jnp.maximum(i * hb - 1, 0), pool_col)),
                pl.BlockSpec((1, POOL_HALO, W_B), lambda bi, i: (bi, jnp.minimum((i + 1) * hb, nh - 1), pool_col)),
                wide(cols['merge0']), wide(cols['merge1']),
                const2((1, W_A)), const2((1, W_A)),
                pl.BlockSpec((len(POOL_WINDOWS), POOL_GROUP_W, POOL_GROUP_W), lambda bi, i: (0, 0, 0)),
                const2((1, W_B)), const2((W_A, d)), const2((W_B, d)), const2((GROUP_W, GROUP_W))]
    args = [y0, y1, bonus, u, u, u, u, u, u, u,
            lp['gn_w'], lp['gn_b'], lp['pool_w'], lp['pool_scale'], lp['w_out_a'], lp['w_out_b'], lp['head_mask']]
    return pl.pallas_call(
        functools.partial(_post1_kernel, tm=tm, t_total=t),
        out_shape=jax.ShapeDtypeStruct((b, t, d), BF16),
        grid=(b, nt),
        in_specs=in_specs,
        out_specs=pl.BlockSpec((1, tm, d), lambda bi, i: (bi, i, 0)),
        compiler_params=_cparams(("arbitrary", "arbitrary")),
    )(*args)


def _post2_kernel(m_ref, h_ref, gate_ref, w_ref, fg_ref, o_ref, *, final_norm):
    out = _dot(m_ref[0], w_ref[...])
    h = h_ref[0] + gate_ref[0] * out
    if final_norm:
        ms = jnp.mean(h * h, axis=-1, keepdims=True)
        h = h * lax.rsqrt(ms + NORM_EPS) * fg_ref[...]
    o_ref[0] = h


def _post2_call(merged, h, gate, w_out, final_g, final_norm):
    bm, t, d = h.shape
    tm = min(512, t)
    return pl.pallas_call(
        functools.partial(_post2_kernel, final_norm=final_norm),
        out_shape=jax.ShapeDtypeStruct((bm, t, d), F32),
        grid=(bm, t // tm),
        in_specs=[
            pl.BlockSpec((1, tm, d), lambda b, m: (b, m, 0)),
            pl.BlockSpec((1, tm, d), lambda b, m: (b, m, 0)),
            pl.BlockSpec((1, 1, d), lambda b, m: (b, 0, 0)),
            pl.BlockSpec((d, d), lambda b, m: (0, 0)),
            pl.BlockSpec((1, d), lambda b, m: (0, 0)),
        ],
        out_specs=pl.BlockSpec((1, tm, d), lambda b, m: (b, m, 0)),
        compiler_params=_cparams(("arbitrary", "arbitrary")),
    )(merged, h, gate, w_out, final_g)


def _pad_rows(w, rows):
    return jnp.pad(w, ((0, 0),) * (w.ndim - 2) + ((0, rows - w.shape[-2]), (0, 0)))


def kernel(x, c, ctx, c_ctx, ada_w, ada_b, norm_g, w_in, tok_mu, decay_w0, decay_lora_b, iclr_a0, iclr_lora_b, key_k, key_a, bonus_rk, gn_w, gn_b, vres_v0, vres_lora_a, vres_lora_b, pool_w, pool_scale, w_out_a, w_out_b, w_out, final_g):
    depth = w_in.shape[0]
    batch, seq, d = x.shape
    ctx_len = ctx.shape[1]
    assert seq % GRID_W == 0 and seq % CHUNK == 0 and ctx_len % CHUNK == 0
    assert d % LANES == 0

    off_merge = 3 * W_A + W_B + W_A + W_B
    off_dl = off_merge + 2 * d
    off_vres = off_dl + 2 * R_DECAY + 2 * R_ICLR
    nu = -(-(off_vres + LANES) // INPROJ_TN) * INPROJ_TN
    cols = dict(pool=REF_OFF_POOL // W_B, gate_a=REF_OFF_GATE_A // W_A, gate_b=REF_OFF_GATE_B // W_A,
                merge0=off_merge // d, merge1=off_merge // d + 1)
    assert off_merge % d == 0 and off_dl % GROUP_W == 0 and off_vres % LANES == 0

    head_mask = (jnp.arange(GROUP_W)[:, None] // HEAD_DIM == jnp.arange(GROUP_W)[None, :] // HEAD_DIM).astype(BF16)

    rows = -(-(batch + 1) // SUBLANES) * SUBLANES
    c_all = jnp.zeros((rows, d), F32).at[:batch].set(c).at[batch].set(c_ctx)
    mod = _ada_call(c_all, ada_w, ada_b)

    h_lat, h_ctx = x, ctx
    vfirst = {}
    for l in range(depth):
        last = l == depth - 1
        w_l = w_in[l]
        pieces = [w_l[:, :REF_OFF_DECAY], w_l[:, REF_OFF_MERGE:REF_OFF_MERGE + 2 * d],
                  w_l[:, REF_OFF_DECAY:REF_OFF_MERGE]]
        if l > 0:
            pieces.append(vres_lora_a[l - 1])
        w_perm = jnp.concatenate(pieces, axis=1)
        w_perm = jnp.pad(w_perm, ((0, 0), (0, nu - w_perm.shape[1]))).astype(BF16)

        dlw = decay_lora_b[l]
        dlw = jnp.stack([jnp.pad(dlw[0], ((0, LANES - R_DECAY), (0, 0))),
                         jnp.pad(dlw[1], ((R_DECAY, LANES - 2 * R_DECAY), (0, 0)))]).astype(BF16)
        ilw = iclr_lora_b[l]
        ilw = jnp.stack([jnp.pad(ilw[0], ((0, LANES - R_ICLR), (0, 0))),
                         jnp.pad(ilw[1], ((R_ICLR, LANES - 2 * R_ICLR), (0, 0)))]).astype(BF16)
        lp = dict(tok_mu=tok_mu[l], decay_w0=decay_w0[l], decay_lora=dlw, iclr_a0=iclr_a0[l], iclr_lora=ilw,
                  key_k=key_k[l].reshape(1, W_A), key_a=key_a[l], bonus_rk=bonus_rk[l].reshape(2, W_A),
                  head_mask=head_mask, gn_w=gn_w[l].reshape(1, W_A), gn_b=gn_b[l].reshape(1, W_A),
                  pool_w=pool_w[l].astype(BF16), pool_scale=pool_scale[l].reshape(1, W_B),
                  w_out_a=w_out_a[l].astype(BF16), w_out_b=w_out_b[l].astype(BF16))
        w_out_l = w_out[l].astype(BF16)
        g_l = norm_g[l].reshape(1, d)

        m = mod[l]
        shift_l, scale_l, gate_l = (m[:batch, j * d:(j + 1) * d].reshape(batch, 1, d) for j in range(3))
        shift_c, scale_c, gate_c = (m[batch:batch + 1, j * d:(j + 1) * d].reshape(1, 1, d) for j in range(3))

        u_c = _inproj_call(h_ctx.reshape(1, batch * ctx_len, d), shift_c, scale_c, g_l, w_perm)
        u_c = u_c.reshape(batch, ctx_len, nu)
        u_l = _inproj_call(h_lat, shift_l, scale_l, g_l, w_perm)

        vres_c = vres_l = None
        if l > 0:
            vb = _pad_rows(vres_lora_b[l - 1], LANES).astype(BF16)
            v0 = vres_v0[l - 1].reshape(1, W_A)
            vres_c = (vfirst['c'], off_vres // LANES, vb, v0)
            vres_l = (vfirst['l'], off_vres // LANES, vb, v0)
        keep_v = l == 0 and depth > 1
        prep_c = _prep_call(u_c, off_dl // GROUP_W, lp, False, not last, vres_c, keep_v)
        prep_l = _prep_call(u_l, off_dl // GROUP_W, lp, True, True, vres_l, keep_v)
        if keep_v:
            vfirst = dict(c=prep_c['vmix'], l=prep_l['vmix'])
        sc_c = _scan_call(prep_c, None, not last)
        sc_l = _scan_call(prep_l, sc_c['sfin'], True)

        merged_l = _post1_call(sc_l, prep_l['bonus'], u_l, cols, lp)
        h_lat = _post2_call(merged_l, h_lat, gate_l, w_out_l, final_g.reshape(1, d), last)
        if not last:
            merged_c = _post1_call(sc_c, prep_c['bonus'], u_c, cols, lp)
            h_ctx = _post2_call(merged_c.reshape(1, batch * ctx_len, d), h_ctx.reshape(1, batch * ctx_len, d),
                                gate_c, w_out_l, final_g.reshape(1, d), False).reshape(batch, ctx_len, d)
    return h_lat
```

```python
import functools
import math

import jax
import jax.numpy as jnp
from jax import lax
from jax.experimental import pallas as pl
from jax.experimental.pallas import tpu as pltpu

F32 = jnp.float32
BF16 = jnp.bfloat16

GRID_W = 64
W_A = 1024
W_B = 1024
HEAD_DIM = 64
POOL_WINDOWS = (2, 4, 8, 16)
POOL_GROUP_W = W_B // len(POOL_WINDOWS)
R_DECAY = 64
R_ICLR = 64
R_VRES = 32
NORM_EPS = 1e-6
GN_EPS = HEAD_DIM * 1e-5
REF_OFF_POOL = 3 * W_A
REF_OFF_GATE_A = REF_OFF_POOL + W_B
REF_OFF_GATE_B = REF_OFF_GATE_A + W_A
REF_OFF_DECAY = REF_OFF_GATE_B + W_B
REF_OFF_ICLR = REF_OFF_DECAY + 2 * R_DECAY
REF_OFF_MERGE = REF_OFF_ICLR + 2 * R_ICLR

LANES = 128
SUBLANES = 8
MXU_DIM = 256
VMEM_LIMIT_BYTES = 56 * 1024 * 1024

GROUP_W = MXU_DIM
GROUP_HEADS = GROUP_W // HEAD_DIM
N_GROUPS = W_A // GROUP_W
SCAN_GROUPS = 2
SCAN_TB = 256
CHUNK = 64
SCAN_CHUNK_SKEW = 2
N_OPERANDS = 4
HALO = 64
POOL_HALO = 8
INPROJ_TN = 1536
LOG_DECAY_SCALE = math.exp(-0.5)


def _cparams(sem):
    return pltpu.CompilerParams(dimension_semantics=sem, vmem_limit_bytes=VMEM_LIMIT_BYTES)


def _sigmoid(x):
    return 1.0 / (1.0 + jnp.exp(-x))


def _dot(a, b):
    return jnp.dot(a, b, preferred_element_type=F32)


def _dot_nt(a, b):
    return lax.dot_general(a, b, (((1,), (1,)), ((), ())), preferred_element_type=F32)


def _dot_tn(a, b):
    return lax.dot_general(a, b, (((0,), (0,)), ((), ())), preferred_element_type=F32)


def _headsums(xs, bm):
    rows = xs[0].shape[0]
    s = _dot(jnp.concatenate([x.astype(BF16) for x in xs], axis=0), bm)
    return [s[n * rows:(n + 1) * rows] for n in range(len(xs))]


def _ada_kernel(c_ref, w_ref, b_ref, o_ref):
    c = c_ref[...]
    s = (c * _sigmoid(c)).astype(BF16)
    o_ref[0] = _dot(s, w_ref[0].astype(BF16)) + b_ref[0]


def _ada_call(c_all, ada_w, ada_b):
    n_layers, d, d3 = ada_w.shape
    rows = c_all.shape[0]
    tn = 768
    return pl.pallas_call(
        _ada_kernel,
        out_shape=jax.ShapeDtypeStruct((n_layers, rows, d3), F32),
        grid=(n_layers, d3 // tn),
        in_specs=[
            pl.BlockSpec((rows, d), lambda l, n: (0, 0)),
            pl.BlockSpec((1, d, tn), lambda l, n: (l, 0, n)),
            pl.BlockSpec((1, 1, tn), lambda l, n: (l, 0, n)),
        ],
        out_specs=pl.BlockSpec((1, rows, tn), lambda l, n: (l, 0, n)),
        compiler_params=_cparams(("arbitrary", "arbitrary")),
    )(c_all, ada_w, ada_b.reshape(n_layers, 1, d3))


def _wcast_kernel(w_ref, t_ref, o_ref, *, n_main):
    o_ref[0] = jnp.where(pl.program_id(1) >= n_main, t_ref[0], w_ref[0]).astype(BF16)


def _wcast_call(w_in, tail, merge_w):
    n_layers, d, width = w_in.shape
    bw = GROUP_W
    n_keep = REF_OFF_DECAY // bw
    n_lora = (REF_OFF_MERGE - REF_OFF_DECAY) // bw
    n_merge = merge_w // bw
    n_main = n_keep + n_merge + n_lora
    n_tail = tail.shape[2] // bw
    assert n_main * bw == width and n_tail * bw == tail.shape[2] and REF_OFF_DECAY % bw == 0 and REF_OFF_MERGE % bw == 0

    def src(j):
        return jnp.where(j < n_keep, j, jnp.where(j < n_keep + n_merge, j + n_lora, jnp.minimum(j - n_merge, n_main - 1)))

    return pl.pallas_call(
        functools.partial(_wcast_kernel, n_main=n_main),
        out_shape=jax.ShapeDtypeStruct((n_layers, d, (n_main + n_tail) * bw), BF16),
        grid=(n_layers, n_main + n_tail),
        in_specs=[pl.BlockSpec((1, d, bw), lambda l, j: (l, 0, src(j))),
                  pl.BlockSpec((1, d, bw), lambda l, j: (l, 0, jnp.maximum(j - n_main, 0)))],
        out_specs=pl.BlockSpec((1, d, bw), lambda l, j: (l, 0, j)),
        compiler_params=_cparams(("arbitrary", "arbitrary")),
    )(w_in, tail)


def _inproj_kernel(h_ref, shift_ref, scale_ref, g_ref, w_ref, o_ref, xn_ref):
    @pl.when(pl.program_id(2) == 0)
    def _():
        x = h_ref[0]
        ms = jnp.mean(x * x, axis=-1, keepdims=True)
        y = x * lax.rsqrt(ms + NORM_EPS) * g_ref[...]
        xn_ref[...] = (y * (1.0 + scale_ref[0]) + shift_ref[0]).astype(BF16)

    o_ref[0] = _dot(xn_ref[...], w_ref[0])


def _inproj_call(h, shift, scale, g, w, layer):
    bm, t, d = h.shape
    nu = w.shape[2]
    tm = min(1024, t)
    tn = INPROJ_TN
    assert nu % tn == 0
    return pl.pallas_call(
        _inproj_kernel,
        out_shape=jax.ShapeDtypeStruct((bm, t, nu), F32),
        grid=(bm, t // tm, nu // tn),
        in_specs=[
            pl.BlockSpec((1, tm, d), lambda b, m, n: (b, m, 0)),
            pl.BlockSpec((1, 1, d), lambda b, m, n: (b, 0, 0)),
            pl.BlockSpec((1, 1, d), lambda b, m, n: (b, 0, 0)),
            pl.BlockSpec((1, d), lambda b, m, n: (0, 0)),
            pl.BlockSpec((1, d, tn), lambda b, m, n: (layer, 0, n)),
        ],
        out_specs=pl.BlockSpec((1, tm, tn), lambda b, m, n: (b, m, n)),
        scratch_shapes=[pltpu.VMEM((tm, d), BF16)],
        compiler_params=_cparams(("arbitrary", "arbitrary", "arbitrary")),
    )(h, shift, scale, g, w)


def _shift_plan(grid_mode, hg):
    if grid_mode:
        return ((-1, 'first'), (1, 'last'), (-GRID_W, None), (GRID_W, None))[hg]
    return ((-1, None), (-1, None), (1, None), (1, None))[hg]


def _prep_kernel(*refs, names, tb, grid_mode, has_vres, emit_out, emit_v):
    R = dict(zip(names, refs))
    i = pl.program_id(1)
    nt = pl.num_programs(1)
    ps = jnp.where(i > 0, 1.0, 0.0).astype(F32)
    ns = jnp.where(i < nt - 1, 1.0, 0.0).astype(F32)
    n_ext = tb + 2 * HALO
    trow1 = lax.broadcasted_iota(jnp.int32, (tb, 1), 0)
    gcol = jnp.bitwise_and(trow1, GRID_W - 1)
    col_mask = dict(first=jnp.where(gcol != 0, 1.0, 0.0).astype(F32),
                    last=jnp.where(gcol != GRID_W - 1, 1.0, 0.0).astype(F32))
    bm = R['head_mask'][...]
    brow = lax.broadcasted_iota(jnp.int32, (tb, tb), 0)
    bcol = lax.broadcasted_iota(jnp.int32, (tb, tb), 1)
    same_chunk = (brow >> 6) == (bcol >> 6)
    tris = (jnp.where(jnp.logical_and(same_chunk, bcol <= brow), 1.0, 0.0).astype(BF16),
            jnp.where(jnp.logical_and(same_chunk, bcol >= brow), 1.0, 0.0).astype(BF16))
    n_chunk = tb // CHUNK
    R['ptot'][...] = jnp.zeros_like(R['ptot'])

    def mixed(hg, m):
        off, cmask = _shift_plan(grid_mode, hg)
        cs = slice(m * W_A + hg * GROUP_W, m * W_A + (hg + 1) * GROUP_W)
        x = R['cur'][0, :, cs]
        if off == -GRID_W:
            sh = jnp.concatenate([R['prv'][0, :, cs] * ps, x[:tb - HALO]], axis=0)
        elif off == GRID_W:
            sh = jnp.concatenate([x[HALO:], R['nxt'][0, :, cs] * ns], axis=0)
        else:
            ext = jnp.concatenate([R['prv'][0, :, cs] * ps, x, R['nxt'][0, :, cs] * ns], axis=0)
            sh = pltpu.roll(ext, 1 if off == -1 else n_ext - 1, axis=0)[HALO:HALO + tb]
        if cmask is not None:
            sh = sh * col_mask[cmask]
        return x + R['tok_mu'][m:m + 1, hg * GROUP_W:(hg + 1) * GROUP_W] * (sh - x)

    dl = R['dl'][0]
    dlo = jnp.tanh(dl[:, :LANES]).astype(BF16)
    alo = dl[:, LANES:].astype(BF16)
    for hg in range(N_GROUPS):
        cs = slice(hg * GROUP_W, (hg + 1) * GROUP_W)
        r = mixed(hg, 0)
        k = mixed(hg, 1)
        v = mixed(hg, 2)
        if has_vres:
            gate = _sigmoid(R['vres_v0'][:, cs] + _dot(R['lo'][0].astype(BF16), R['vres_b'][:, cs]))
            v = v + (R['vf'][0, :, cs] - v) * gate
        if emit_v:
            R['vmix'][0, :, cs] = v
        kk = k * R['key_k'][:, cs]
        a_d, kd_d, sums_in = [], [], [kk * kk]
        for d in (0, 1):
            a = _sigmoid(R['iclr_a0'][d:d + 1, cs] + _dot(alo, R['iclr_lora'][d, :, cs]))
            kd = k * (1.0 + (a - 1.0) * R['key_a'][d:d + 1, cs])
            a_d.append(a)
            kd_d.append(kd)
            if emit_out:
                sums_in.append(r * kd * R['bonus_rk'][d:d + 1, cs])
        sums = _headsums(sums_in, bm)
        kk = kk / jnp.maximum(jnp.sqrt(sums[0]), 1e-12)
        if emit_out:
            R['bonus'][0, :, cs] = (sums[1] + sums[2]) * v
        R['vop'][0, :, cs] = v.astype(BF16)
        for d in (0, 1):
            z = R['decay_w0'][d:d + 1, cs] + _dot(dlo, R['decay_lora'][d, :, cs])
            lg = -LOG_DECAY_SCALE * _sigmoid(z)
            g1 = lg.astype(BF16)
            g2 = (lg - g1.astype(F32)).astype(BF16)
            cum = _dot(tris[d], g1) + _dot(tris[d], g2)
            bb = kk * a_d[d]
            kd = kd_d[d]
            for c in range(n_chunk):
                sl = slice(c * CHUNK, (c + 1) * CHUNK)
                lc = cum[sl]
                ltot = lc[CHUNK - 1:CHUNK] if d == 0 else lc[0:1]
                e_pos = jnp.exp(lc)
                e_neg = 1.0 / e_pos
                R['ptot'][d, 0, 0, c:c + 1, cs] = jnp.exp(ltot)
                operands = (r[sl] * e_pos, kk[sl] * jnp.exp(lc - lg[sl]), kd[sl] * e_neg, bb[sl] * e_neg)
                for m in range(N_OPERANDS):
                    R['ops'][d, m, 0, sl, cs] = operands[m].astype(BF16)


def _prep_call(u, dl_col, lp, grid_mode, emit_out, vres=None, emit_v=False):
    b, t, _ = u.shape
    tb = min(SCAN_TB, t)
    nt = t // tb
    hb = tb // HALO
    nh = t // HALO
    w3 = 3 * W_A
    has_vres = vres is not None
    assert GROUP_W == W_A // 4 and tb % HALO == 0 and tb % CHUNK == 0 and tb // CHUNK <= SUBLANES
    names, in_specs, args = [], [], []

    def add(name, spec, arr):
        names.append(name)
        in_specs.append(spec)
        args.append(arr)

    add('cur', pl.BlockSpec((1, tb, w3), lambda bi, i: (bi, i, 0)), u)
    add('prv', pl.BlockSpec((1, HALO, w3), lambda bi, i: (bi, jnp.maximum(i * hb - 1, 0), 0)), u)
    add('nxt', pl.BlockSpec((1, HALO, w3), lambda bi, i: (bi, jnp.minimum((i + 1) * hb, nh - 1), 0)), u)
    add('dl', pl.BlockSpec((1, tb, GROUP_W), lambda bi, i: (bi, i, dl_col)), u)
    if has_vres:
        v_first, vres_col, vres_b, vres_v0 = vres
        add('vf', pl.BlockSpec((1, tb, W_A), lambda bi, i: (bi, i, 0)), v_first)
        add('lo', pl.BlockSpec((1, tb, LANES), lambda bi, i: (bi, i, vres_col)), u)
        add('vres_b', pl.BlockSpec((LANES, W_A), lambda bi, i: (0, 0)), vres_b)
        add('vres_v0', pl.BlockSpec((1, W_A), lambda bi, i: (0, 0)), vres_v0)
    vec = lambda rows: pl.BlockSpec((rows, W_A), lambda bi, i: (0, 0))
    lora = pl.BlockSpec((2, LANES, W_A), lambda bi, i: (0, 0, 0))
    add('tok_mu', vec(3), lp['tok_mu'])
    add('decay_w0', vec(2), lp['decay_w0'])
    add('decay_lora', lora, lp['decay_lora'])
    add('iclr_a0', vec(2), lp['iclr_a0'])
    add('iclr_lora', lora, lp['iclr_lora'])
    add('key_k', vec(1), lp['key_k'])
    add('key_a', vec(2), lp['key_a'])
    add('bonus_rk', vec(2), lp['bonus_rk'])
    add('head_mask', pl.BlockSpec((GROUP_W, GROUP_W), lambda bi, i: (0, 0)), lp['head_mask'])

    out_names = ['ops', 'ptot', 'vop']
    out_shape = [jax.ShapeDtypeStruct((2, N_OPERANDS, b, t, W_A), BF16),
                 jax.ShapeDtypeStruct((2, b, nt, SUBLANES, W_A), F32),
                 jax.ShapeDtypeStruct((b, t, W_A), BF16)]
    out_specs = [pl.BlockSpec((2, N_OPERANDS, 1, tb, W_A), lambda bi, i: (0, 0, bi, i, 0)),
                 pl.BlockSpec((2, 1, 1, SUBLANES, W_A), lambda bi, i: (0, bi, i, 0, 0)),
                 pl.BlockSpec((1, tb, W_A), lambda bi, i: (bi, i, 0))]
    for flag, nm in ((emit_out, 'bonus'), (emit_v, 'vmix')):
        if flag:
            out_names.append(nm)
            out_shape.append(jax.ShapeDtypeStruct((b, t, W_A), F32))
            out_specs.append(pl.BlockSpec((1, tb, W_A), lambda bi, i: (bi, i, 0)))

    outs = pl.pallas_call(
        functools.partial(_prep_kernel, names=tuple(names + out_names), tb=tb, grid_mode=grid_mode,
                          has_vres=has_vres, emit_out=emit_out, emit_v=emit_v),
        out_shape=out_shape,
        grid=(b, nt),
        in_specs=in_specs,
        out_specs=out_specs,
        compiler_params=_cparams(("arbitrary", "arbitrary")),
    )(*args)
    return dict(zip(out_names, outs))


def _scan_kernel(*refs, names, tb, n_groups, use_s0, emit_out):
    R = dict(zip(names, refs))
    s_ref = R['s_scr']
    i = pl.program_id(2)
    nt = pl.num_programs(2)

    @pl.when(i == 0)
    def _():
        if use_s0:
            s_ref[...] = R['s0'][:, 0]
        else:
            s_ref[...] = jnp.zeros_like(s_ref)

    lane1 = lax.broadcasted_iota(jnp.int32, (1, LANES), 1)
    half_f = (jnp.where(lane1 < HEAD_DIM, 1.0, 0.0).astype(F32), jnp.where(lane1 >= HEAD_DIM, 1.0, 0.0).astype(F32))
    half_b = (half_f[0].astype(BF16), half_f[1].astype(BF16))
    zero_tile = jnp.zeros((HEAD_DIM, LANES), BF16)
    n_tiles = GROUP_W // LANES

    def bd(x):
        xb = x.astype(BF16)
        rows = []
        for j in range(GROUP_HEADS):
            t = (j * HEAD_DIM) // LANES
            piece = xb[:, t * LANES:(t + 1) * LANES] * half_b[(j * HEAD_DIM % LANES) // HEAD_DIM]
            rows.append(jnp.concatenate([piece if tt == t else zero_tile for tt in range(n_tiles)], axis=1))
        return jnp.concatenate(rows, axis=0)

    def head_blocks(full):
        tiles = []
        for t in range(n_tiles):
            acc = None
            for j in range(GROUP_HEADS):
                if (j * HEAD_DIM) // LANES == t:
                    part = (full[j * HEAD_DIM:(j + 1) * HEAD_DIM, t * LANES:(t + 1) * LANES]
                            * half_f[(j * HEAD_DIM % LANES) // HEAD_DIM])
                    acc = part if acc is None else acc + part
            tiles.append(acc)
        return jnp.concatenate(tiles, axis=1)

    def prod(a, b):
        return _dot(a.astype(BF16), bd(b))

    trow = lax.broadcasted_iota(jnp.int32, (CHUNK, GROUP_W), 0)
    scol = jnp.bitwise_and(lax.broadcasted_iota(jnp.int32, (CHUNK, GROUP_W), 1), CHUNK - 1)
    eye = jnp.where(trow == scol, 1.0, 0.0).astype(F32)
    same16 = (trow >> 4) == (scol >> 4)
    same32 = (trow >> 5) == (scol >> 5)
    tri_masks = []
    for before, upto in ((scol < trow, scol <= trow), (scol > trow, scol >= trow)):
        tri_masks.append(dict(
            neg_d16=jnp.where(jnp.logical_and(before, same16), -1.0, 0.0).astype(F32),
            o32=jnp.where(jnp.logical_and(before, jnp.logical_and(same32, jnp.logical_not(same16))), 1.0, 0.0).astype(BF16),
            o64=jnp.where(jnp.logical_and(before, jnp.logical_not(same32)), 1.0, 0.0).astype(BF16),
            strict=jnp.where(before, 1.0, 0.0).astype(F32),
            incl=jnp.where(upto, 1.0, 0.0).astype(F32)))

    n_chunk = tb // CHUNK
    s_cur = {(d, g): s_ref[d, g] for d in (0, 1) for g in range(n_groups)}

    def unit_stages(d, g, c):
        sl = slice(c * CHUNK, (c + 1) * CHUNK)
        cs = slice(g * GROUP_W, (g + 1) * GROUP_W)
        r_h, kap_h, k_h, b_h = (R[f'ops{d}'][0, m, 0, sl, cs] for m in range(N_OPERANDS))
        vc = R[f'v{d}'][0, sl, cs]
        p_tot = R[f'ptot{d}'][0, 0, 0, c:c + 1, cs]
        p_tot_b = p_tot.astype(BF16)
        k_t = k_h * p_tot_b
        b_t = b_h * p_tot_b
        mk = tri_masks[d]
        x2 = jnp.concatenate([kap_h, r_h], axis=0)
        gb = _dot_nt(x2, bd(b_h))
        yield
        gk = _dot_nt(x2, bd(k_h))
        yield
        ab = gb[:CHUNK]
        abb = ab.astype(BF16)
        n1 = ab * mk['neg_d16']
        o32 = abb * mk['o32']
        o64 = abb * mk['o64']
        a_rb = (gb[CHUNK:] * mk['incl']).astype(BF16)
        a_kk = jnp.concatenate([gk[:CHUNK] * mk['strict'], gk[CHUNK:] * mk['incl']], axis=0).astype(BF16)
        m1y = _dot(a_kk, bd(vc))
        yield
        n1b = n1.astype(BF16)
        nk = prod(n1b, n1b).astype(BF16)
        x = eye + n1
        yield
        for _ in range(2):
            both = _dot(jnp.concatenate([nk, x.astype(BF16)], axis=0), bd(nk))
            nk = both[:CHUNK].astype(BF16)
            x = x + both[CHUNK:]
            yield
        x = x + prod(x, nk)
        yield
        for off in (o32, o64):
            xb = x.astype(BF16)
            xo = prod(xb, off).astype(BF16)
            yield
            x = x - prod(xo, xb)
            yield
        t16 = x.astype(BF16)
        w = _dot(t16, bd(kap_h)).astype(BF16)
        yield
        ut = _dot(t16, bd(m1y[:CHUNK]))
        yield
        s_in = s_cur[d, g]
        wr = _dot_nt(jnp.concatenate([w, r_h], axis=0) if emit_out else w, bd(s_in))
        yield
        u = (-(wr[:CHUNK] + ut)).astype(BF16)
        s_cur[d, g] = s_in * p_tot + head_blocks(_dot_tn(jnp.concatenate([u, vc], axis=0),
                                                         jnp.concatenate([b_t, k_t], axis=0)))
        yield
        if emit_out:
            R[f'y{d}'][0, sl, cs] = wr[CHUNK:] + prod(a_rb, u) + m1y[CHUNK:]

    jobs = [(pos * SCAN_CHUNK_SKEW, unit_stages(d, g, c)) for d in (0, 1) for g in range(n_groups)
            for pos, c in enumerate(range(n_chunk) if d == 0 else range(n_chunk - 1, -1, -1))]
    tick = 0
    while jobs:
        for entry in list(jobs):
            if tick >= entry[0]:
                try:
                    next(entry[1])
                except StopIteration:
                    jobs.remove(entry)
        tick += 1
    for (d, g), s_val in s_cur.items():
        s_ref[d, g] = s_val

    @pl.when(i == nt - 1)
    def _():
        R['sfin'][:, 0] = s_ref[...]


def _scan_call(prep, s0, emit_out):
    ops, ptot = prep['ops'], prep['ptot']
    _, _, b, t, _ = ops.shape
    tb = min(SCAN_TB, t)
    nt = t // tb
    ng = SCAN_GROUPS
    gw = ng * GROUP_W
    npair = N_GROUPS // ng
    use_s0 = s0 is not None
    tblk = (lambda i: i, lambda i: nt - 1 - i)
    names, in_specs, args = [], [], []

    def add(name, spec, arr):
        names.append(name)
        in_specs.append(spec)
        args.append(arr)

    for d in (0, 1):
        add(f'ops{d}', pl.BlockSpec((1, N_OPERANDS, 1, tb, gw), lambda bi, p, i, d=d: (d, 0, bi, tblk[d](i), p)), ops)
        add(f'ptot{d}', pl.BlockSpec((1, 1, 1, SUBLANES, gw), lambda bi, p, i, d=d: (d, bi, tblk[d](i), 0, p)), ptot)
        add(f'v{d}', pl.BlockSpec((1, tb, gw), lambda bi, p, i, d=d: (bi, tblk[d](i), p)), prep['vop'])
    state_spec = pl.BlockSpec((2, 1, ng, HEAD_DIM, GROUP_W), lambda bi, p, i: (0, bi, p, 0, 0))
    if use_s0:
        add('s0', state_spec, s0)

    out_names, out_shape, out_specs = [], [], []
    if emit_out:
        for d in (0, 1):
            out_names.append(f'y{d}')
            out_shape.append(jax.ShapeDtypeStruct((b, t, W_A), F32))
            out_specs.append(pl.BlockSpec((1, tb, gw), lambda bi, p, i, d=d: (bi, tblk[d](i), p)))
    out_names.append('sfin')
    out_shape.append(jax.ShapeDtypeStruct((2, b, N_GROUPS, HEAD_DIM, GROUP_W), F32))
    out_specs.append(state_spec)

    outs = pl.pallas_call(
        functools.partial(_scan_kernel, names=tuple(names + out_names + ['s_scr']), tb=tb, n_groups=ng,
                          use_s0=use_s0, emit_out=emit_out),
        out_shape=out_shape,
        grid=(b, npair, nt),
        in_specs=in_specs,
        out_specs=out_specs,
        scratch_shapes=[pltpu.VMEM((2, ng, HEAD_DIM, GROUP_W), F32)],
        compiler_params=_cparams(("arbitrary", "arbitrary", "arbitrary")),
    )(*args)
    return dict(zip(out_names, outs))


def _post1_kernel(y0, y1, bo, ga, gb, pc, pp, pn, mg0, mg1, gnw, gnb, pw, ps, woa, wob, bm_ref, o_ref, *, tm, t_total):
    i = pl.program_id(1)
    nt = pl.num_programs(1)
    bm = bm_ref[...]

    parts = []
    for hg in range(N_GROUPS):
        cs = slice(hg * GROUP_W, (hg + 1) * GROUP_W)
        y = y0[0, :, cs] + y1[0, :, cs]
        mu = _headsums([y], bm)[0] * (1.0 / HEAD_DIM)
        yc = y - mu
        var = _headsums([yc * yc], bm)[0] * (1.0 / HEAD_DIM)
        yn = yc * lax.rsqrt(var + GN_EPS) * gnw[:, cs] + gnb[:, cs]
        yn = yn + bo[0, :, cs]
        gate = ga[0, :, cs]
        parts.append((yn * (gate * _sigmoid(gate))).astype(BF16))
    y_a = _dot(jnp.concatenate(parts, axis=1), woa[...])

    pscale = jnp.where(i > 0, 1.0, 0.0).astype(F32)
    nscale = jnp.where(i < nt - 1, 1.0, 0.0).astype(F32)
    n_ext = tm + 2 * POOL_HALO
    tpos = (lax.broadcasted_iota(jnp.int32, (tm, 1), 0) + i * tm).astype(F32)
    parts = []
    for gi, win in enumerate(POOL_WINDOWS):
        left = win // 2
        right = win - 1 - left
        cs = slice(gi * POOL_GROUP_W, (gi + 1) * POOL_GROUP_W)
        p = pc[0, :, cs]
        ext = jnp.concatenate([pp[0, :, cs] * pscale, p, pn[0, :, cs] * nscale], axis=0)
        s = ext
        step = 1
        while step < win:
            s = s + pltpu.roll(s, step, axis=0)
            step *= 2
        if right > 0:
            s = pltpu.roll(s, n_ext - right, axis=0)
        s = s[POOL_HALO:POOL_HALO + tm]
        cnt = jnp.minimum(tpos + (right + 1), float(t_total)) - jnp.maximum(tpos - left, 0.0)
        pooled = s / cnt - p
        mixed = _dot(pooled.astype(BF16), pw[gi]) * ps[:, cs]
        gate = gb[0, :, cs]
        parts.append((mixed * (gate * _sigmoid(gate))).astype(BF16))
    y_b = _dot(jnp.concatenate(parts, axis=1), wob[...])

    merged = _sigmoid(mg0[0]) * y_a + _sigmoid(mg1[0]) * y_b
    o_ref[0] = merged.astype(BF16)


def _post1_call(sc, bonus, u, cols, lp):
    y0, y1 = sc['y0'], sc['y1']
    b, t, _ = y0.shape
    d = lp['w_out_a'].shape[1]
    tm = min(256, t)
    nt = t // tm
    hb = tm // POOL_HALO
    nh = t // POOL_HALO
    blk = lambda c: pl.BlockSpec((1, tm, W_A), lambda bi, i: (bi, i, c))
    wide = lambda c: pl.BlockSpec((1, tm, d), lambda bi, i: (bi, i, c))
    const2 = lambda shape: pl.BlockSpec(shape, lambda bi, i: (0, 0))
    pool_col = cols['pool']
    in_specs = [blk(0), blk(0), blk(0),
                blk(cols['gate_a']), blk(cols['gate_b']), blk(pool_col),
                pl.BlockSpec((1, POOL_HALO, W_B), lambda bi, i: (bi, jnp.maximum(i * hb - 1, 0), pool_col)),
                pl.BlockSpec((1, POOL_HALO, W_B), lambda bi, i: (bi, jnp.minimum((i + 1) * hb, nh - 1), pool_col)),
                wide(cols['merge0']), wide(cols['merge1']),
                const2((1, W_A)), const2((1, W_A)),
                pl.BlockSpec((len(POOL_WINDOWS), POOL_GROUP_W, POOL_GROUP_W), lambda bi, i: (0, 0, 0)),
                const2((1, W_B)), const2((W_A, d)), const2((W_B, d)), const2((GROUP_W, GROUP_W))]
    args = [y0, y1, bonus, u, u, u, u, u, u, u,
            lp['gn_w'], lp['gn_b'], lp['pool_w'], lp['pool_scale'], lp['w_out_a'], lp['w_out_b'], lp['head_mask']]
    return pl.pallas_call(
        functools.partial(_post1_kernel, tm=tm, t_total=t),
        out_shape=jax.ShapeDtypeStruct((b, t, d), BF16),
        grid=(b, nt),
        in_specs=in_specs,
        out_specs=pl.BlockSpec((1, tm, d), lambda bi, i: (bi, i, 0)),
        compiler_params=_cparams(("arbitrary", "arbitrary")),
    )(*args)


def _post2_kernel(m_ref, h_ref, gate_ref, w_ref, fg_ref, o_ref, *, final_norm):
    out = _dot(m_ref[0], w_ref[...])
    h = h_ref[0] + gate_ref[0] * out
    if final_norm:
        ms = jnp.mean(h * h, axis=-1, keepdims=True)
        h = h * lax.rsqrt(ms + NORM_EPS) * fg_ref[...]
    o_ref[0] = h


def _post2_call(merged, h, gate, w_out, final_g, final_norm):
    bm, t, d = h.shape
    tm = min(512, t)
    return pl.pallas_call(
        functools.partial(_post2_kernel, final_norm=final_norm),
        out_shape=jax.ShapeDtypeStruct((bm, t, d), F32),
        grid=(bm, t // tm),
        in_specs=[
            pl.BlockSpec((1, tm, d), lambda b, m: (b, m, 0)),
            pl.BlockSpec((1, tm, d), lambda b, m: (b, m, 0)),
            pl.BlockSpec((1, 1, d), lambda b, m: (b, 0, 0)),
            pl.BlockSpec((d, d), lambda b, m: (0, 0)),
            pl.BlockSpec((1, d), lambda b, m: (0, 0)),
        ],
        out_specs=pl.BlockSpec((1, tm, d), lambda b, m: (b, m, 0)),
        compiler_params=_cparams(("arbitrary", "arbitrary")),
    )(merged, h, gate, w_out, final_g)


def _pad_rows(w, rows):
    return jnp.pad(w, ((0, 0),) * (w.ndim - 2) + ((0, rows - w.shape[-2]), (0, 0)))


def kernel(x, c, ctx, c_ctx, ada_w, ada_b, norm_g, w_in, tok_mu, decay_w0, decay_lora_b, iclr_a0, iclr_lora_b, key_k, key_a, bonus_rk, gn_w, gn_b, vres_v0, vres_lora_a, vres_lora_b, pool_w, pool_scale, w_out_a, w_out_b, w_out, final_g):
    depth = w_in.shape[0]
    batch, seq, d = x.shape
    ctx_len = ctx.shape[1]
    assert seq % GRID_W == 0 and seq % CHUNK == 0 and ctx_len % CHUNK == 0
    assert d % LANES == 0

    off_merge = 3 * W_A + W_B + W_A + W_B
    off_dl = off_merge + 2 * d
    off_vres = off_dl + 2 * R_DECAY + 2 * R_ICLR
    nu = -(-(off_vres + LANES) // INPROJ_TN) * INPROJ_TN
    cols = dict(pool=REF_OFF_POOL // W_B, gate_a=REF_OFF_GATE_A // W_A, gate_b=REF_OFF_GATE_B // W_A,
                merge0=off_merge // d, merge1=off_merge // d + 1)
    assert off_merge % d == 0 and off_dl % GROUP_W == 0 and off_vres % LANES == 0

    head_mask = (jnp.arange(GROUP_W)[:, None] // HEAD_DIM == jnp.arange(GROUP_W)[None, :] // HEAD_DIM).astype(BF16)

    rows = -(-(batch + 1) // SUBLANES) * SUBLANES
    c_all = jnp.zeros((rows, d), F32).at[:batch].set(c).at[batch].set(c_ctx)
    mod = _ada_call(c_all, ada_w, ada_b)

    h_lat, h_ctx = x, ctx
    vfirst = {}
    tail = jnp.zeros((depth, d, nu - off_vres), F32)
    if depth > 1:
        tail = tail.at[1:, :, :R_VRES].set(vres_lora_a)
    w_perm = _wcast_call(w_in, tail, 2 * d)
    for l in range(depth):
        last = l == depth - 1

        dlw = decay_lora_b[l]
        dlw = jnp.stack([jnp.pad(dlw[0], ((0, LANES - R_DECAY), (0, 0))),
                         jnp.pad(dlw[1], ((R_DECAY, LANES - 2 * R_DECAY), (0, 0)))]).astype(BF16)
        ilw = iclr_lora_b[l]
        ilw = jnp.stack([jnp.pad(ilw[0], ((0, LANES - R_ICLR), (0, 0))),
                         jnp.pad(ilw[1], ((R_ICLR, LANES - 2 * R_ICLR), (0, 0)))]).astype(BF16)
        lp = dict(tok_mu=tok_mu[l], decay_w0=decay_w0[l], decay_lora=dlw, iclr_a0=iclr_a0[l], iclr_lora=ilw,
                  key_k=key_k[l].reshape(1, W_A), key_a=key_a[l], bonus_rk=bonus_rk[l].reshape(2, W_A),
                  head_mask=head_mask, gn_w=gn_w[l].reshape(1, W_A), gn_b=gn_b[l].reshape(1, W_A),
                  pool_w=pool_w[l].astype(BF16), pool_scale=pool_scale[l].reshape(1, W_B),
                  w_out_a=w_out_a[l].astype(BF16), w_out_b=w_out_b[l].astype(BF16))
        w_out_l = w_out[l].astype(BF16)
        g_l = norm_g[l].reshape(1, d)

        m = mod[l]
        shift_l, scale_l, gate_l = (m[:batch, j * d:(j + 1) * d].reshape(batch, 1, d) for j in range(3))
        shift_c, scale_c, gate_c = (m[batch:batch + 1, j * d:(j + 1) * d].reshape(1, 1, d) for j in range(3))

        u_c = _inproj_call(h_ctx.reshape(1, batch * ctx_len, d), shift_c, scale_c, g_l, w_perm, l)
        u_c = u_c.reshape(batch, ctx_len, nu)
        u_l = _inproj_call(h_lat, shift_l, scale_l, g_l, w_perm, l)

        vres_c = vres_l = None
        if l > 0:
            vb = _pad_rows(vres_lora_b[l - 1], LANES).astype(BF16)
            v0 = vres_v0[l - 1].reshape(1, W_A)
            vres_c = (vfirst['c'], off_vres // LANES, vb, v0)
            vres_l = (vfirst['l'], off_vres // LANES, vb, v0)
        keep_v = l == 0 and depth > 1
        prep_c = _prep_call(u_c, off_dl // GROUP_W, lp, False, not last, vres_c, keep_v)
        prep_l = _prep_call(u_l, off_dl // GROUP_W, lp, True, True, vres_l, keep_v)
        if keep_v:
            vfirst = dict(c=prep_c['vmix'], l=prep_l['vmix'])
        sc_c = _scan_call(prep_c, None, not last)
        sc_l = _scan_call(prep_l, sc_c['sfin'], True)

        merged_l = _post1_call(sc_l, prep_l['bonus'], u_l, cols, lp)
        h_lat = _post2_call(merged_l, h_lat, gate_l, w_out_l, final_g.reshape(1, d), last)
        if not last:
            merged_c = _post1_call(sc_c, prep_c['bonus'], u_c, cols, lp)
            h_ctx = _post2_call(merged_c.reshape(1, batch * ctx_len, d), h_ctx.reshape(1, batch * ctx_len, d),
                                gate_c, w_out_l, final_g.reshape(1, d), False).reshape(batch, ctx_len, d)
    return h_lat
```

```python
import functools
import math

import jax
import jax.numpy as jnp
from jax import lax
from jax.experimental import pallas as pl
from jax.experimental.pallas import tpu as pltpu

F32 = jnp.float32
BF16 = jnp.bfloat16

GRID_W = 64
W_A = 1024
W_B = 1024
HEAD_DIM = 64
POOL_WINDOWS = (2, 4, 8, 16)
POOL_GROUP_W = W_B // len(POOL_WINDOWS)
R_DECAY = 64
R_ICLR = 64
R_VRES = 32
NORM_EPS = 1e-6
GN_EPS = HEAD_DIM * 1e-5
REF_OFF_POOL = 3 * W_A
REF_OFF_GATE_A = REF_OFF_POOL + W_B
REF_OFF_GATE_B = REF_OFF_GATE_A + W_A
REF_OFF_DECAY = REF_OFF_GATE_B + W_B
REF_OFF_ICLR = REF_OFF_DECAY + 2 * R_DECAY
REF_OFF_MERGE = REF_OFF_ICLR + 2 * R_ICLR

LANES = 128
SUBLANES = 8
MXU_DIM = 256
VMEM_LIMIT_BYTES = 56 * 1024 * 1024

GROUP_W = MXU_DIM
GROUP_HEADS = GROUP_W // HEAD_DIM
N_GROUPS = W_A // GROUP_W
SCAN_GROUPS = 2
SCAN_TB = 256
CHUNK = 64
SCAN_CHUNK_SKEW = 2
N_OPERANDS = 4
HALO = 64
POOL_HALO = 8
INPROJ_TN = 1536
INPROJ_NORM_ROWS = 32
LOG_DECAY_SCALE = math.exp(-0.5)


def _cparams(sem):
    return pltpu.CompilerParams(dimension_semantics=sem, vmem_limit_bytes=VMEM_LIMIT_BYTES)


def _sigmoid(x):
    return 1.0 / (1.0 + jnp.exp(-x))


def _dot(a, b):
    return jnp.dot(a, b, preferred_element_type=F32)


def _dot_nt(a, b):
    return lax.dot_general(a, b, (((1,), (1,)), ((), ())), preferred_element_type=F32)


def _dot_tn(a, b):
    return lax.dot_general(a, b, (((0,), (0,)), ((), ())), preferred_element_type=F32)


def _headsums(xs, bm):
    rows = xs[0].shape[0]
    s = _dot(jnp.concatenate([x.astype(BF16) for x in xs], axis=0), bm)
    return [s[n * rows:(n + 1) * rows] for n in range(len(xs))]


def _ada_kernel(c_ref, w_ref, b_ref, o_ref):
    c = c_ref[...]
    s = (c * _sigmoid(c)).astype(BF16)
    o_ref[0] = _dot(s, w_ref[0].astype(BF16)) + b_ref[0]


def _ada_call(c_all, ada_w, ada_b):
    n_layers, d, d3 = ada_w.shape
    rows = c_all.shape[0]
    tn = 768
    return pl.pallas_call(
        _ada_kernel,
        out_shape=jax.ShapeDtypeStruct((n_layers, rows, d3), F32),
        grid=(n_layers, d3 // tn),
        in_specs=[
            pl.BlockSpec((rows, d), lambda l, n: (0, 0)),
            pl.BlockSpec((1, d, tn), lambda l, n: (l, 0, n)),
            pl.BlockSpec((1, 1, tn), lambda l, n: (l, 0, n)),
        ],
        out_specs=pl.BlockSpec((1, rows, tn), lambda l, n: (l, 0, n)),
        compiler_params=_cparams(("arbitrary", "arbitrary")),
    )(c_all, ada_w, ada_b.reshape(n_layers, 1, d3))


def _wcast_kernel(w_ref, t_ref, o_ref, *, n_main):
    o_ref[0] = jnp.where(pl.program_id(1) >= n_main, t_ref[0], w_ref[0]).astype(BF16)


def _wcast_call(w_in, tail, merge_w):
    n_layers, d, width = w_in.shape
    bw = GROUP_W
    n_keep = REF_OFF_DECAY // bw
    n_lora = (REF_OFF_MERGE - REF_OFF_DECAY) // bw
    n_merge = merge_w // bw
    n_main = n_keep + n_merge + n_lora
    n_tail = tail.shape[2] // bw
    assert n_main * bw == width and n_tail * bw == tail.shape[2] and REF_OFF_DECAY % bw == 0 and REF_OFF_MERGE % bw == 0

    def src(j):
        return jnp.where(j < n_keep, j, jnp.where(j < n_keep + n_merge, j + n_lora, jnp.minimum(j - n_merge, n_main - 1)))

    return pl.pallas_call(
        functools.partial(_wcast_kernel, n_main=n_main),
        out_shape=jax.ShapeDtypeStruct((n_layers, d, (n_main + n_tail) * bw), BF16),
        grid=(n_layers, n_main + n_tail),
        in_specs=[pl.BlockSpec((1, d, bw), lambda l, j: (l, 0, src(j))),
                  pl.BlockSpec((1, d, bw), lambda l, j: (l, 0, jnp.maximum(j - n_main, 0)))],
        out_specs=pl.BlockSpec((1, d, bw), lambda l, j: (l, 0, j)),
        compiler_params=_cparams(("arbitrary", "arbitrary")),
    )(w_in, tail)


def _inproj_kernel(h_ref, shift_ref, scale_ref, g_ref, w_ref, o_ref, xn_ref):
    @pl.when(pl.program_id(2) == 0)
    def _():
        def rows(r, carry):
            sl = pl.ds(pl.multiple_of(r * INPROJ_NORM_ROWS, INPROJ_NORM_ROWS), INPROJ_NORM_ROWS)
            x = h_ref[0, sl, :]
            ms = jnp.mean(x * x, axis=-1, keepdims=True)
            y = x * lax.rsqrt(ms + NORM_EPS) * g_ref[...]
            xn_ref[sl, :] = (y * (1.0 + scale_ref[0]) + shift_ref[0]).astype(BF16)
            return carry

        lax.fori_loop(0, h_ref.shape[1] // INPROJ_NORM_ROWS, rows, 0, unroll=8)

    o_ref[0] = _dot(xn_ref[...], w_ref[0])


def _inproj_call(h, shift, scale, g, w, layer):
    bm, t, d = h.shape
    nu = w.shape[2]
    tm = min(1024, t)
    tn = INPROJ_TN
    assert nu % tn == 0
    return pl.pallas_call(
        _inproj_kernel,
        out_shape=jax.ShapeDtypeStruct((bm, t, nu), F32),
        grid=(bm, t // tm, nu // tn),
        in_specs=[
            pl.BlockSpec((1, tm, d), lambda b, m, n: (b, m, 0)),
            pl.BlockSpec((1, 1, d), lambda b, m, n: (b, 0, 0)),
            pl.BlockSpec((1, 1, d), lambda b, m, n: (b, 0, 0)),
            pl.BlockSpec((1, d), lambda b, m, n: (0, 0)),
            pl.BlockSpec((1, d, tn), lambda b, m, n: (layer, 0, n)),
        ],
        out_specs=pl.BlockSpec((1, tm, tn), lambda b, m, n: (b, m, n)),
        scratch_shapes=[pltpu.VMEM((tm, d), BF16)],
        compiler_params=_cparams(("arbitrary", "arbitrary", "arbitrary")),
    )(h, shift, scale, g, w)


def _shift_plan(grid_mode, hg):
    if grid_mode:
        return ((-1, 'first'), (1, 'last'), (-GRID_W, None), (GRID_W, None))[hg]
    return ((-1, None), (-1, None), (1, None), (1, None))[hg]


def _prep_kernel(*refs, names, tb, grid_mode, has_vres, emit_out, emit_v):
    R = dict(zip(names, refs))
    i = pl.program_id(1)
    nt = pl.num_programs(1)
    ps = jnp.where(i > 0, 1.0, 0.0).astype(F32)
    ns = jnp.where(i < nt - 1, 1.0, 0.0).astype(F32)
    n_ext = tb + 2 * HALO
    trow1 = lax.broadcasted_iota(jnp.int32, (tb, 1), 0)
    gcol = jnp.bitwise_and(trow1, GRID_W - 1)
    col_mask = dict(first=jnp.where(gcol != 0, 1.0, 0.0).astype(F32),
                    last=jnp.where(gcol != GRID_W - 1, 1.0, 0.0).astype(F32))
    bm = R['head_mask'][...]
    brow = lax.broadcasted_iota(jnp.int32, (tb, tb), 0)
    bcol = lax.broadcasted_iota(jnp.int32, (tb, tb), 1)
    same_chunk = (brow >> 6) == (bcol >> 6)
    tris = (jnp.where(jnp.logical_and(same_chunk, bcol <= brow), 1.0, 0.0).astype(BF16),
            jnp.where(jnp.logical_and(same_chunk, bcol >= brow), 1.0, 0.0).astype(BF16))
    n_chunk = tb // CHUNK
    R['ptot'][...] = jnp.zeros_like(R['ptot'])

    def mixed(hg, m):
        off, cmask = _shift_plan(grid_mode, hg)
        cs = slice(m * W_A + hg * GROUP_W, m * W_A + (hg + 1) * GROUP_W)
        x = R['cur'][0, :, cs]
        if off == -GRID_W:
            sh = jnp.concatenate([R['prv'][0, :, cs] * ps, x[:tb - HALO]], axis=0)
        elif off == GRID_W:
            sh = jnp.concatenate([x[HALO:], R['nxt'][0, :, cs] * ns], axis=0)
        else:
            ext = jnp.concatenate([R['prv'][0, :, cs] * ps, x, R['nxt'][0, :, cs] * ns], axis=0)
            sh = pltpu.roll(ext, 1 if off == -1 else n_ext - 1, axis=0)[HALO:HALO + tb]
        if cmask is not None:
            sh = sh * col_mask[cmask]
        return x + R['tok_mu'][m:m + 1, hg * GROUP_W:(hg + 1) * GROUP_W] * (sh - x)

    dl = R['dl'][0]
    dlo = jnp.tanh(dl[:, :LANES]).astype(BF16)
    alo = dl[:, LANES:].astype(BF16)
    for hg in range(N_GROUPS):
        cs = slice(hg * GROUP_W, (hg + 1) * GROUP_W)
        r = mixed(hg, 0)
        k = mixed(hg, 1)
        v = mixed(hg, 2)
        if has_vres:
            gate = _sigmoid(R['vres_v0'][:, cs] + _dot(R['lo'][0].astype(BF16), R['vres_b'][:, cs]))
            v = v + (R['vf'][0, :, cs] - v) * gate
        if emit_v:
            R['vmix'][0, :, cs] = v
        kk = k * R['key_k'][:, cs]
        a_d, kd_d, sums_in = [], [], [kk * kk]
        for d in (0, 1):
            a = _sigmoid(R['iclr_a0'][d:d + 1, cs] + _dot(alo, R['iclr_lora'][d, :, cs]))
            ka = R['key_a'][d:d + 1, cs]
            kd = k * ((1.0 - ka) + a * ka)
            a_d.append(a)
            kd_d.append(kd)
            if emit_out:
                sums_in.append(r * kd * R['bonus_rk'][d:d + 1, cs])
        sums = _headsums(sums_in, bm)
        kk = kk / jnp.maximum(jnp.sqrt(sums[0]), 1e-12)
        if emit_out:
            R['bonus'][0, :, cs] = (sums[1] + sums[2]) * v
        R['vop'][0, :, cs] = v.astype(BF16)
        for d in (0, 1):
            z = R['decay_w0'][d:d + 1, cs] + _dot(dlo, R['decay_lora'][d, :, cs])
            lg = -LOG_DECAY_SCALE * _sigmoid(z)
            g1 = lg.astype(BF16)
            g2 = (lg - g1.astype(F32)).astype(BF16)
            cum = _dot(tris[d], g1) + _dot(tris[d], g2)
            bb = kk * a_d[d]
            kd = kd_d[d]
            for c in range(n_chunk):
                sl = slice(c * CHUNK, (c + 1) * CHUNK)
                lc = cum[sl]
                ltot = lc[CHUNK - 1:CHUNK] if d == 0 else lc[0:1]
                e_pos = jnp.exp(lc)
                e_neg = 1.0 / e_pos
                R['ptot'][d, 0, 0, c:c + 1, cs] = jnp.exp(ltot)
                operands = (r[sl] * e_pos, kk[sl] * jnp.exp(lc - lg[sl]), kd[sl] * e_neg, bb[sl] * e_neg)
                for m in range(N_OPERANDS):
                    R['ops'][d, m, 0, sl, cs] = operands[m].astype(BF16)


def _prep_call(u, dl_col, lp, grid_mode, emit_out, vres=None, emit_v=False):
    b, t, _ = u.shape
    tb = min(SCAN_TB, t)
    nt = t // tb
    hb = tb // HALO
    nh = t // HALO
    w3 = 3 * W_A
    has_vres = vres is not None
    assert GROUP_W == W_A // 4 and tb % HALO == 0 and tb % CHUNK == 0 and tb // CHUNK <= SUBLANES
    names, in_specs, args = [], [], []

    def add(name, spec, arr):
        names.append(name)
        in_specs.append(spec)
        args.append(arr)

    add('cur', pl.BlockSpec((1, tb, w3), lambda bi, i: (bi, i, 0)), u)
    add('prv', pl.BlockSpec((1, HALO, w3), lambda bi, i: (bi, jnp.maximum(i * hb - 1, 0), 0)), u)
    add('nxt', pl.BlockSpec((1, HALO, w3), lambda bi, i: (bi, jnp.minimum((i + 1) * hb, nh - 1), 0)), u)
    add('dl', pl.BlockSpec((1, tb, GROUP_W), lambda bi, i: (bi, i, dl_col)), u)
    if has_vres:
        v_first, vres_col, vres_b, vres_v0 = vres
        add('vf', pl.BlockSpec((1, tb, W_A), lambda bi, i: (bi, i, 0)), v_first)
        add('lo', pl.BlockSpec((1, tb, LANES), lambda bi, i: (bi, i, vres_col)), u)
        add('vres_b', pl.BlockSpec((LANES, W_A), lambda bi, i: (0, 0)), vres_b)
        add('vres_v0', pl.BlockSpec((1, W_A), lambda bi, i: (0, 0)), vres_v0)
    vec = lambda rows: pl.BlockSpec((rows, W_A), lambda bi, i: (0, 0))
    lora = pl.BlockSpec((2, LANES, W_A), lambda bi, i: (0, 0, 0))
    add('tok_mu', vec(3), lp['tok_mu'])
    add('decay_w0', vec(2), lp['decay_w0'])
    add('decay_lora', lora, lp['decay_lora'])
    add('iclr_a0', vec(2), lp['iclr_a0'])
    add('iclr_lora', lora, lp['iclr_lora'])
    add('key_k', vec(1), lp['key_k'])
    add('key_a', vec(2), lp['key_a'])
    add('bonus_rk', vec(2), lp['bonus_rk'])
    add('head_mask', pl.BlockSpec((GROUP_W, GROUP_W), lambda bi, i: (0, 0)), lp['head_mask'])

    out_names = ['ops', 'ptot', 'vop']
    out_shape = [jax.ShapeDtypeStruct((2, N_OPERANDS, b, t, W_A), BF16),
                 jax.ShapeDtypeStruct((2, b, nt, SUBLANES, W_A), F32),
                 jax.ShapeDtypeStruct((b, t, W_A), BF16)]
    out_specs = [pl.BlockSpec((2, N_OPERANDS, 1, tb, W_A), lambda bi, i: (0, 0, bi, i, 0)),
                 pl.BlockSpec((2, 1, 1, SUBLANES, W_A), lambda bi, i: (0, bi, i, 0, 0)),
                 pl.BlockSpec((1, tb, W_A), lambda bi, i: (bi, i, 0))]
    for flag, nm in ((emit_out, 'bonus'), (emit_v, 'vmix')):
        if flag:
            out_names.append(nm)
            out_shape.append(jax.ShapeDtypeStruct((b, t, W_A), F32))
            out_specs.append(pl.BlockSpec((1, tb, W_A), lambda bi, i: (bi, i, 0)))

    outs = pl.pallas_call(
        functools.partial(_prep_kernel, names=tuple(names + out_names), tb=tb, grid_mode=grid_mode,
                          has_vres=has_vres, emit_out=emit_out, emit_v=emit_v),
        out_shape=out_shape,
        grid=(b, nt),
        in_specs=in_specs,
        out_specs=out_specs,
        compiler_params=_cparams(("arbitrary", "arbitrary")),
    )(*args)
    return dict(zip(out_names, outs))


def _scan_kernel(*refs, names, tb, n_groups, use_s0, emit_out):
    R = dict(zip(names, refs))
    s_ref = R['s_scr']
    i = pl.program_id(2)
    nt = pl.num_programs(2)

    @pl.when(i == 0)
    def _():
        if use_s0:
            s_ref[...] = R['s0'][:, 0]
        else:
            s_ref[...] = jnp.zeros_like(s_ref)

    lane1 = lax.broadcasted_iota(jnp.int32, (1, LANES), 1)
    half_f = (jnp.where(lane1 < HEAD_DIM, 1.0, 0.0).astype(F32), jnp.where(lane1 >= HEAD_DIM, 1.0, 0.0).astype(F32))
    half_b = (half_f[0].astype(BF16), half_f[1].astype(BF16))
    zero_tile = jnp.zeros((HEAD_DIM, LANES), BF16)
    n_tiles = GROUP_W // LANES

    def bd(x):
        xb = x.astype(BF16)
        rows = []
        for j in range(GROUP_HEADS):
            t = (j * HEAD_DIM) // LANES
            piece = xb[:, t * LANES:(t + 1) * LANES] * half_b[(j * HEAD_DIM % LANES) // HEAD_DIM]
            rows.append(jnp.concatenate([piece if tt == t else zero_tile for tt in range(n_tiles)], axis=1))
        return jnp.concatenate(rows, axis=0)

    def head_blocks(full):
        tiles = []
        for t in range(n_tiles):
            acc = None
            for j in range(GROUP_HEADS):
                if (j * HEAD_DIM) // LANES == t:
                    part = (full[j * HEAD_DIM:(j + 1) * HEAD_DIM, t * LANES:(t + 1) * LANES]
                            * half_f[(j * HEAD_DIM % LANES) // HEAD_DIM])
                    acc = part if acc is None else acc + part
            tiles.append(acc)
        return jnp.concatenate(tiles, axis=1)

    def prod(a, b):
        return _dot(a.astype(BF16), bd(b))

    trow = lax.broadcasted_iota(jnp.int32, (CHUNK, GROUP_W), 0)
    scol = jnp.bitwise_and(lax.broadcasted_iota(jnp.int32, (CHUNK, GROUP_W), 1), CHUNK - 1)
    eye = jnp.where(trow == scol, 1.0, 0.0).astype(F32)
    same16 = (trow >> 4) == (scol >> 4)
    same32 = (trow >> 5) == (scol >> 5)
    tri_masks = []
    for before, upto in ((scol < trow, scol <= trow), (scol > trow, scol >= trow)):
        tri_masks.append(dict(
            neg_d16=jnp.where(jnp.logical_and(before, same16), -1.0, 0.0).astype(F32),
            o32=jnp.where(jnp.logical_and(before, jnp.logical_and(same32, jnp.logical_not(same16))), 1.0, 0.0).astype(BF16),
            o64=jnp.where(jnp.logical_and(before, jnp.logical_not(same32)), 1.0, 0.0).astype(BF16),
            strict=jnp.where(before, 1.0, 0.0).astype(F32),
            incl=jnp.where(upto, 1.0, 0.0).astype(F32)))

    n_chunk = tb // CHUNK
    s_cur = {(d, g): s_ref[d, g] for d in (0, 1) for g in range(n_groups)}

    def unit_stages(d, g, c):
        sl = slice(c * CHUNK, (c + 1) * CHUNK)
        cs = slice(g * GROUP_W, (g + 1) * GROUP_W)
        r_h, kap_h, k_h, b_h = (R[f'ops{d}'][0, m, 0, sl, cs] for m in range(N_OPERANDS))
        vc = R[f'v{d}'][0, sl, cs]
        p_tot = R[f'ptot{d}'][0, 0, 0, c:c + 1, cs]
        p_tot_b = p_tot.astype(BF16)
        k_t = k_h * p_tot_b
        b_t = b_h * p_tot_b
        mk = tri_masks[d]
        x2 = jnp.concatenate([kap_h, r_h], axis=0)
        gb = _dot_nt(x2, bd(b_h))
        yield
        gk = _dot_nt(x2, bd(k_h))
        yield
        ab = gb[:CHUNK]
        abb = ab.astype(BF16)
        n1 = ab * mk['neg_d16']
        o32 = abb * mk['o32']
        o64 = abb * mk['o64']
        a_rb = (gb[CHUNK:] * mk['incl']).astype(BF16)
        a_kk = jnp.concatenate([gk[:CHUNK] * mk['strict'], gk[CHUNK:] * mk['incl']], axis=0).astype(BF16)
        m1y = _dot(a_kk, bd(vc))
        yield
        n1b = n1.astype(BF16)
        nk = prod(n1b, n1b).astype(BF16)
        x = eye + n1
        yield
        for _ in range(2):
            both = _dot(jnp.concatenate([nk, x.astype(BF16)], axis=0), bd(nk))
            nk = both[:CHUNK].astype(BF16)
            x = x + both[CHUNK:]
            yield
        x = x + prod(x, nk)
        yield
        for off in (o32, o64):
            xb = x.astype(BF16)
            xo = prod(xb, off).astype(BF16)
            yield
            x = x - prod(xo, xb)
            yield
        t16 = x.astype(BF16)
        w = _dot(t16, bd(kap_h)).astype(BF16)
        yield
        ut = _dot(t16, bd(m1y[:CHUNK]))
        yield
        s_in = s_cur[d, g]
        wr = _dot_nt(jnp.concatenate([w, r_h], axis=0) if emit_out else w, bd(s_in))
        yield
        u = (-(wr[:CHUNK] + ut)).astype(BF16)
        s_cur[d, g] = s_in * p_tot + head_blocks(_dot_tn(jnp.concatenate([u, vc], axis=0),
                                                         jnp.concatenate([b_t, k_t], axis=0)))
        yield
        if emit_out:
            R[f'y{d}'][0, sl, cs] = wr[CHUNK:] + prod(a_rb, u) + m1y[CHUNK:]

    jobs = [(pos * SCAN_CHUNK_SKEW, unit_stages(d, g, c)) for d in (0, 1) for g in range(n_groups)
            for pos, c in enumerate(range(n_chunk) if d == 0 else range(n_chunk - 1, -1, -1))]
    tick = 0
    while jobs:
        for entry in list(jobs):
            if tick >= entry[0]:
                try:
                    next(entry[1])
                except StopIteration:
                    jobs.remove(entry)
        tick += 1
    for (d, g), s_val in s_cur.items():
        s_ref[d, g] = s_val

    @pl.when(i == nt - 1)
    def _():
        R['sfin'][:, 0] = s_ref[...]


def _scan_call(prep, s0, emit_out):
    ops, ptot = prep['ops'], prep['ptot']
    _, _, b, t, _ = ops.shape
    tb = min(SCAN_TB, t)
    nt = t // tb
    ng = SCAN_GROUPS
    gw = ng * GROUP_W
    npair = N_GROUPS // ng
    use_s0 = s0 is not None
    tblk = (lambda i: i, lambda i: nt - 1 - i)
    names, in_specs, args = [], [], []

    def add(name, spec, arr):
        names.append(name)
        in_specs.append(spec)
        args.append(arr)

    for d in (0, 1):
        add(f'ops{d}', pl.BlockSpec((1, N_OPERANDS, 1, tb, gw), lambda bi, p, i, d=d: (d, 0, bi, tblk[d](i), p)), ops)
        add(f'ptot{d}', pl.BlockSpec((1, 1, 1, SUBLANES, gw), lambda bi, p, i, d=d: (d, bi, tblk[d](i), 0, p)), ptot)
        add(f'v{d}', pl.BlockSpec((1, tb, gw), lambda bi, p, i, d=d: (bi, tblk[d](i), p)), prep['vop'])
    state_spec = pl.BlockSpec((2, 1, ng, HEAD_DIM, GROUP_W), lambda bi, p, i: (0, bi, p, 0, 0))
    if use_s0:
        add('s0', state_spec, s0)

    out_names, out_shape, out_specs = [], [], []
    if emit_out:
        for d in (0, 1):
            out_names.append(f'y{d}')
            out_shape.append(jax.ShapeDtypeStruct((b, t, W_A), F32))
            out_specs.append(pl.BlockSpec((1, tb, gw), lambda bi, p, i, d=d: (bi, tblk[d](i), p)))
    out_names.append('sfin')
    out_shape.append(jax.ShapeDtypeStruct((2, b, N_GROUPS, HEAD_DIM, GROUP_W), F32))
    out_specs.append(state_spec)

    outs = pl.pallas_call(
        functools.partial(_scan_kernel, names=tuple(names + out_names + ['s_scr']), tb=tb, n_groups=ng,
                          use_s0=use_s0, emit_out=emit_out),
        out_shape=out_shape,
        grid=(b, npair, nt),
        in_specs=in_specs,
        out_specs=out_specs,
        scratch_shapes=[pltpu.VMEM((2, ng, HEAD_DIM, GROUP_W), F32)],
        compiler_params=_cparams(("arbitrary", "arbitrary", "arbitrary")),
    )(*args)
    return dict(zip(out_names, outs))


def _post1_kernel(y0, y1, bo, ga, gb, pc, pp, pn, mg0, mg1, gnw, gnb, pw, ps, woa, wob, bm_ref, o_ref, *, tm, t_total):
    i = pl.program_id(1)
    nt = pl.num_programs(1)
    bm = bm_ref[...]

    parts = []
    for hg in range(N_GROUPS):
        cs = slice(hg * GROUP_W, (hg + 1) * GROUP_W)
        y = y0[0, :, cs] + y1[0, :, cs]
        mu = _headsums([y], bm)[0] * (1.0 / HEAD_DIM)
        yc = y - mu
        var = _headsums([yc * yc], bm)[0] * (1.0 / HEAD_DIM)
        yn = yc * lax.rsqrt(var + GN_EPS) * gnw[:, cs] + gnb[:, cs]
        yn = yn + bo[0, :, cs]
        gate = ga[0, :, cs]
        parts.append((yn * (gate * _sigmoid(gate))).astype(BF16))
    y_a = _dot(jnp.concatenate(parts, axis=1), woa[...])

    pscale = jnp.where(i > 0, 1.0, 0.0).astype(F32)
    nscale = jnp.where(i < nt - 1, 1.0, 0.0).astype(F32)
    n_ext = tm + 2 * POOL_HALO
    tpos = (lax.broadcasted_iota(jnp.int32, (tm, 1), 0) + i * tm).astype(F32)
    parts = []
    for gi, win in enumerate(POOL_WINDOWS):
        left = win // 2
        right = win - 1 - left
        cs = slice(gi * POOL_GROUP_W, (gi + 1) * POOL_GROUP_W)
        p = pc[0, :, cs]
        ext = jnp.concatenate([pp[0, :, cs] * pscale, p, pn[0, :, cs] * nscale], axis=0)
        s = ext
        step = 1
        while step < win:
            s = s + pltpu.roll(s, step, axis=0)
            step *= 2
        if right > 0:
            s = pltpu.roll(s, n_ext - right, axis=0)
        s = s[POOL_HALO:POOL_HALO + tm]
        cnt = jnp.minimum(tpos + (right + 1), float(t_total)) - jnp.maximum(tpos - left, 0.0)
        pooled = s / cnt - p
        mixed = _dot(pooled.astype(BF16), pw[gi]) * ps[:, cs]
        gate = gb[0, :, cs]
        parts.append((mixed * (gate * _sigmoid(gate))).astype(BF16))
    y_b = _dot(jnp.concatenate(parts, axis=1), wob[...])

    merged = _sigmoid(mg0[0]) * y_a + _sigmoid(mg1[0]) * y_b
    o_ref[0] = merged.astype(BF16)


def _post1_call(sc, bonus, u, cols, lp):
    y0, y1 = sc['y0'], sc['y1']
    b, t, _ = y0.shape
    d = lp['w_out_a'].shape[1]
    tm = min(256, t)
    nt = t // tm
    hb = tm // POOL_HALO
    nh = t // POOL_HALO
    blk = lambda c: pl.BlockSpec((1, tm, W_A), lambda bi, i: (bi, i, c))
    wide = lambda c: pl.BlockSpec((1, tm, d), lambda bi, i: (bi, i, c))
    const2 = lambda shape: pl.BlockSpec(shape, lambda bi, i: (0, 0))
    pool_col = cols['pool']
    in_specs = [blk(0), blk(0), blk(0),
                blk(cols['gate_a']), blk(cols['gate_b']), blk(pool_col),
                pl.BlockSpec((1, POOL_HALO, W_B), lambda bi, i: (bi, jnp.maximum(i * hb - 1, 0), pool_col)),
                pl.BlockSpec((1, POOL_HALO, W_B), lambda bi, i: (bi, jnp.minimum((i + 1) * hb, nh - 1), pool_col)),
                wide(cols['merge0']), wide(cols['merge1']),
                const2((1, W_A)), const2((1, W_A)),
                pl.BlockSpec((len(POOL_WINDOWS), POOL_GROUP_W, POOL_GROUP_W), lambda bi, i: (0, 0, 0)),
                const2((1, W_B)), const2((W_A, d)), const2((W_B, d)), const2((GROUP_W, GROUP_W))]
    args = [y0, y1, bonus, u, u, u, u, u, u, u,
            lp['gn_w'], lp['gn_b'], lp['pool_w'], lp['pool_scale'], lp['w_out_a'], lp['w_out_b'], lp['head_mask']]
    return pl.pallas_call(
        functools.partial(_post1_kernel, tm=tm, t_total=t),
        out_shape=jax.ShapeDtypeStruct((b, t, d), BF16),
        grid=(b, nt),
        in_specs=in_specs,
        out_specs=pl.BlockSpec((1, tm, d), lambda bi, i: (bi, i, 0)),
        compiler_params=_cparams(("arbitrary", "arbitrary")),
    )(*args)


def _post2_kernel(m_ref, h_ref, gate_ref, w_ref, fg_ref, o_ref, *, final_norm):
    out = _dot(m_ref[0], w_ref[...])
    h = h_ref[0] + gate_ref[0] * out
    if final_norm:
        ms = jnp.mean(h * h, axis=-1, keepdims=True)
        h = h * lax.rsqrt(ms + NORM_EPS) * fg_ref[...]
    o_ref[0] = h


def _post2_call(merged, h, gate, w_out, final_g, final_norm):
    bm, t, d = h.shape
    tm = min(512, t)
    return pl.pallas_call(
        functools.partial(_post2_kernel, final_norm=final_norm),
        out_shape=jax.ShapeDtypeStruct((bm, t, d), F32),
        grid=(bm, t // tm),
        in_specs=[
            pl.BlockSpec((1, tm, d), lambda b, m: (b, m, 0)),
            pl.BlockSpec((1, tm, d), lambda b, m: (b, m, 0)),
            pl.BlockSpec((1, 1, d), lambda b, m: (b, 0, 0)),
            pl.BlockSpec((d, d), lambda b, m: (0, 0)),
            pl.BlockSpec((1, d), lambda b, m: (0, 0)),
        ],
        out_specs=pl.BlockSpec((1, tm, d), lambda b, m: (b, m, 0)),
        compiler_params=_cparams(("arbitrary", "arbitrary")),
    )(merged, h, gate, w_out, final_g)


def _pad_rows(w, rows):
    return jnp.pad(w, ((0, 0),) * (w.ndim - 2) + ((0, rows - w.shape[-2]), (0, 0)))


def kernel(x, c, ctx, c_ctx, ada_w, ada_b, norm_g, w_in, tok_mu, decay_w0, decay_lora_b, iclr_a0, iclr_lora_b, key_k, key_a, bonus_rk, gn_w, gn_b, vres_v0, vres_lora_a, vres_lora_b, pool_w, pool_scale, w_out_a, w_out_b, w_out, final_g):
    depth = w_in.shape[0]
    batch, seq, d = x.shape
    ctx_len = ctx.shape[1]
    assert seq % GRID_W == 0 and seq % CHUNK == 0 and ctx_len % CHUNK == 0
    assert d % LANES == 0

    off_merge = 3 * W_A + W_B + W_A + W_B
    off_dl = off_merge + 2 * d
    off_vres = off_dl + 2 * R_DECAY + 2 * R_ICLR
    nu = -(-(off_vres + LANES) // INPROJ_TN) * INPROJ_TN
    cols = dict(pool=REF_OFF_POOL // W_B, gate_a=REF_OFF_GATE_A // W_A, gate_b=REF_OFF_GATE_B // W_A,
                merge0=off_merge // d, merge1=off_merge // d + 1)
    assert off_merge % d == 0 and off_dl % GROUP_W == 0 and off_vres % LANES == 0

    head_mask = (jnp.arange(GROUP_W)[:, None] // HEAD_DIM == jnp.arange(GROUP_W)[None, :] // HEAD_DIM).astype(BF16)

    rows = -(-(batch + 1) // SUBLANES) * SUBLANES
    c_all = jnp.zeros((rows, d), F32).at[:batch].set(c).at[batch].set(c_ctx)
    mod = _ada_call(c_all, ada_w, ada_b)

    h_lat, h_ctx = x, ctx
    vfirst = {}
    tail = jnp.zeros((depth, d, nu - off_vres), F32)
    if depth > 1:
        tail = tail.at[1:, :, :R_VRES].set(vres_lora_a)
    w_perm = _wcast_call(w_in, tail, 2 * d)
    for l in range(depth):
        last = l == depth - 1

        dlw = decay_lora_b[l]
        dlw = jnp.stack([jnp.pad(dlw[0], ((0, LANES - R_DECAY), (0, 0))),
                         jnp.pad(dlw[1], ((R_DECAY, LANES - 2 * R_DECAY), (0, 0)))]).astype(BF16)
        ilw = iclr_lora_b[l]
        ilw = jnp.stack([jnp.pad(ilw[0], ((0, LANES - R_ICLR), (0, 0))),
                         jnp.pad(ilw[1], ((R_ICLR, LANES - 2 * R_ICLR), (0, 0)))]).astype(BF16)
        lp = dict(tok_mu=tok_mu[l], decay_w0=decay_w0[l], decay_lora=dlw, iclr_a0=iclr_a0[l], iclr_lora=ilw,
                  key_k=key_k[l].reshape(1, W_A), key_a=key_a[l], bonus_rk=bonus_rk[l].reshape(2, W_A),
                  head_mask=head_mask, gn_w=gn_w[l].reshape(1, W_A), gn_b=gn_b[l].reshape(1, W_A),
                  pool_w=pool_w[l].astype(BF16), pool_scale=pool_scale[l].reshape(1, W_B),
                  w_out_a=w_out_a[l].astype(BF16), w_out_b=w_out_b[l].astype(BF16))
        w_out_l = w_out[l].astype(BF16)
        g_l = norm_g[l].reshape(1, d)

        m = mod[l]
        shift_l, scale_l, gate_l = (m[:batch, j * d:(j + 1) * d].reshape(batch, 1, d) for j in range(3))
        shift_c, scale_c, gate_c = (m[batch:batch + 1, j * d:(j + 1) * d].reshape(1, 1, d) for j in range(3))

        u_c = _inproj_call(h_ctx.reshape(1, batch * ctx_len, d), shift_c, scale_c, g_l, w_perm, l)
        u_c = u_c.reshape(batch, ctx_len, nu)
        u_l = _inproj_call(h_lat, shift_l, scale_l, g_l, w_perm, l)

        vres_c = vres_l = None
        if l > 0:
            vb = _pad_rows(vres_lora_b[l - 1], LANES).astype(BF16)
            v0 = vres_v0[l - 1].reshape(1, W_A)
            vres_c = (vfirst['c'], off_vres // LANES, vb, v0)
            vres_l = (vfirst['l'], off_vres // LANES, vb, v0)
        keep_v = l == 0 and depth > 1
        prep_c = _prep_call(u_c, off_dl // GROUP_W, lp, False, not last, vres_c, keep_v)
        prep_l = _prep_call(u_l, off_dl // GROUP_W, lp, True, True, vres_l, keep_v)
        if keep_v:
            vfirst = dict(c=prep_c['vmix'], l=prep_l['vmix'])
        sc_c = _scan_call(prep_c, None, not last)
        sc_l = _scan_call(prep_l, sc_c['sfin'], True)

        merged_l = _post1_call(sc_l, prep_l['bonus'], u_l, cols, lp)
        h_lat = _post2_call(merged_l, h_lat, gate_l, w_out_l, final_g.reshape(1, d), last)
        if not last:
            merged_c = _post1_call(sc_c, prep_c['bonus'], u_c, cols, lp)
            h_ctx = _post2_call(merged_c.reshape(1, batch * ctx_len, d), h_ctx.reshape(1, batch * ctx_len, d),
                                gate_c, w_out_l, final_g.reshape(1, d), False).reshape(batch, ctx_len, d)
    return h_lat
```

```python
import functools
import math

import jax
import jax.numpy as jnp
from jax import lax
from jax.experimental import pallas as pl
from jax.experimental.pallas import tpu as pltpu

F32 = jnp.float32
BF16 = jnp.bfloat16

GRID_W = 64
W_A = 1024
W_B = 1024
HEAD_DIM = 64
POOL_WINDOWS = (2, 4, 8, 16)
POOL_GROUP_W = W_B // len(POOL_WINDOWS)
R_DECAY = 64
R_ICLR = 64
R_VRES = 32
NORM_EPS = 1e-6
GN_EPS = HEAD_DIM * 1e-5
REF_OFF_POOL = 3 * W_A
REF_OFF_GATE_A = REF_OFF_POOL + W_B
REF_OFF_GATE_B = REF_OFF_GATE_A + W_A
REF_OFF_DECAY = REF_OFF_GATE_B + W_B
REF_OFF_ICLR = REF_OFF_DECAY + 2 * R_DECAY
REF_OFF_MERGE = REF_OFF_ICLR + 2 * R_ICLR

LANES = 128
SUBLANES = 8
MXU_DIM = 256
VMEM_LIMIT_BYTES = 56 * 1024 * 1024

GROUP_W = MXU_DIM
GROUP_HEADS = GROUP_W // HEAD_DIM
N_GROUPS = W_A // GROUP_W
SCAN_GROUPS = 4
SCAN_TB = 256
CHUNK = 64
SCAN_CHUNK_SKEW = 2
N_OPERANDS = 4
HALO = 64
POOL_HALO = 8
INPROJ_TN = 1536
INPROJ_NORM_ROWS = 32
LOG_DECAY_SCALE = math.exp(-0.5)


def _cparams(sem):
    return pltpu.CompilerParams(dimension_semantics=sem, vmem_limit_bytes=VMEM_LIMIT_BYTES)


def _sigmoid(x):
    return 1.0 / (1.0 + jnp.exp(-x))


def _dot(a, b):
    return jnp.dot(a, b, preferred_element_type=F32)


def _dot_nt(a, b):
    return lax.dot_general(a, b, (((1,), (1,)), ((), ())), preferred_element_type=F32)


def _dot_tn(a, b):
    return lax.dot_general(a, b, (((0,), (0,)), ((), ())), preferred_element_type=F32)


def _headsums(xs, bm):
    rows = xs[0].shape[0]
    s = _dot(jnp.concatenate([x.astype(BF16) for x in xs], axis=0), bm)
    return [s[n * rows:(n + 1) * rows] for n in range(len(xs))]


def _ada_kernel(c_ref, w_ref, b_ref, o_ref):
    c = c_ref[...]
    s = (c * _sigmoid(c)).astype(BF16)
    o_ref[0] = _dot(s, w_ref[0].astype(BF16)) + b_ref[0]


def _ada_call(c_all, ada_w, ada_b):
    n_layers, d, d3 = ada_w.shape
    rows = c_all.shape[0]
    tn = 768
    return pl.pallas_call(
        _ada_kernel,
        out_shape=jax.ShapeDtypeStruct((n_layers, rows, d3), F32),
        grid=(n_layers, d3 // tn),
        in_specs=[
            pl.BlockSpec((rows, d), lambda l, n: (0, 0)),
            pl.BlockSpec((1, d, tn), lambda l, n: (l, 0, n)),
            pl.BlockSpec((1, 1, tn), lambda l, n: (l, 0, n)),
        ],
        out_specs=pl.BlockSpec((1, rows, tn), lambda l, n: (l, 0, n)),
        compiler_params=_cparams(("arbitrary", "arbitrary")),
    )(c_all, ada_w, ada_b.reshape(n_layers, 1, d3))


def _wcast_kernel(w_ref, t_ref, o_ref, *, n_main):
    o_ref[0] = jnp.where(pl.program_id(1) >= n_main, t_ref[0], w_ref[0]).astype(BF16)


def _wcast_call(w_in, tail, merge_w):
    n_layers, d, width = w_in.shape
    bw = GROUP_W
    n_keep = REF_OFF_DECAY // bw
    n_lora = (REF_OFF_MERGE - REF_OFF_DECAY) // bw
    n_merge = merge_w // bw
    n_main = n_keep + n_merge + n_lora
    n_tail = tail.shape[2] // bw
    assert n_main * bw == width and n_tail * bw == tail.shape[2] and REF_OFF_DECAY % bw == 0 and REF_OFF_MERGE % bw == 0

    def src(j):
        return jnp.where(j < n_keep, j, jnp.where(j < n_keep + n_merge, j + n_lora, jnp.minimum(j - n_merge, n_main - 1)))

    return pl.pallas_call(
        functools.partial(_wcast_kernel, n_main=n_main),
        out_shape=jax.ShapeDtypeStruct((n_layers, d, (n_main + n_tail) * bw), BF16),
        grid=(n_layers, n_main + n_tail),
        in_specs=[pl.BlockSpec((1, d, bw), lambda l, j: (l, 0, src(j))),
                  pl.BlockSpec((1, d, bw), lambda l, j: (l, 0, jnp.maximum(j - n_main, 0)))],
        out_specs=pl.BlockSpec((1, d, bw), lambda l, j: (l, 0, j)),
        compiler_params=_cparams(("arbitrary", "arbitrary")),
    )(w_in, tail)


def _inproj_kernel(h_ref, shift_ref, scale_ref, g_ref, w_ref, o_ref, xn_ref):
    @pl.when(pl.program_id(2) == 0)
    def _():
        def rows(r, carry):
            sl = pl.ds(pl.multiple_of(r * INPROJ_NORM_ROWS, INPROJ_NORM_ROWS), INPROJ_NORM_ROWS)
            x = h_ref[0, sl, :]
            ms = jnp.mean(x * x, axis=-1, keepdims=True)
            y = x * lax.rsqrt(ms + NORM_EPS) * g_ref[...]
            xn_ref[sl, :] = (y * (1.0 + scale_ref[0]) + shift_ref[0]).astype(BF16)
            return carry

        lax.fori_loop(0, h_ref.shape[1] // INPROJ_NORM_ROWS, rows, 0, unroll=8)

    o_ref[0] = _dot(xn_ref[...], w_ref[0])


def _inproj_call(h, shift, scale, g, w, layer):
    bm, t, d = h.shape
    nu = w.shape[2]
    tm = min(1024, t)
    tn = INPROJ_TN
    assert nu % tn == 0
    return pl.pallas_call(
        _inproj_kernel,
        out_shape=jax.ShapeDtypeStruct((bm, t, nu), F32),
        grid=(bm, t // tm, nu // tn),
        in_specs=[
            pl.BlockSpec((1, tm, d), lambda b, m, n: (b, m, 0)),
            pl.BlockSpec((1, 1, d), lambda b, m, n: (b, 0, 0)),
            pl.BlockSpec((1, 1, d), lambda b, m, n: (b, 0, 0)),
            pl.BlockSpec((1, d), lambda b, m, n: (0, 0)),
            pl.BlockSpec((1, d, tn), lambda b, m, n: (layer, 0, n)),
        ],
        out_specs=pl.BlockSpec((1, tm, tn), lambda b, m, n: (b, m, n)),
        scratch_shapes=[pltpu.VMEM((tm, d), BF16)],
        compiler_params=_cparams(("arbitrary", "arbitrary", "arbitrary")),
    )(h, shift, scale, g, w)


def _shift_plan(grid_mode, hg):
    if grid_mode:
        return ((-1, 'first'), (1, 'last'), (-GRID_W, None), (GRID_W, None))[hg]
    return ((-1, None), (-1, None), (1, None), (1, None))[hg]


def _prep_kernel(*refs, names, tb, grid_mode, has_vres, emit_out, emit_v):
    R = dict(zip(names, refs))
    i = pl.program_id(1)
    nt = pl.num_programs(1)
    ps = jnp.where(i > 0, 1.0, 0.0).astype(F32)
    ns = jnp.where(i < nt - 1, 1.0, 0.0).astype(F32)
    n_ext = tb + 2 * HALO
    trow1 = lax.broadcasted_iota(jnp.int32, (tb, 1), 0)
    gcol = jnp.bitwise_and(trow1, GRID_W - 1)
    col_mask = dict(first=jnp.where(gcol != 0, 1.0, 0.0).astype(F32),
                    last=jnp.where(gcol != GRID_W - 1, 1.0, 0.0).astype(F32))
    bm = R['head_mask'][...]
    brow = lax.broadcasted_iota(jnp.int32, (tb, tb), 0)
    bcol = lax.broadcasted_iota(jnp.int32, (tb, tb), 1)
    same_chunk = (brow >> 6) == (bcol >> 6)
    tris = (jnp.where(jnp.logical_and(same_chunk, bcol <= brow), 1.0, 0.0).astype(BF16),
            jnp.where(jnp.logical_and(same_chunk, bcol >= brow), 1.0, 0.0).astype(BF16))
    n_chunk = tb // CHUNK
    R['ptot'][...] = jnp.zeros_like(R['ptot'])

    def mixed(hg, m):
        off, cmask = _shift_plan(grid_mode, hg)
        cs = slice(m * W_A + hg * GROUP_W, m * W_A + (hg + 1) * GROUP_W)
        x = R['cur'][0, :, cs]
        if off == -GRID_W:
            sh = jnp.concatenate([R['prv'][0, :, cs] * ps, x[:tb - HALO]], axis=0)
        elif off == GRID_W:
            sh = jnp.concatenate([x[HALO:], R['nxt'][0, :, cs] * ns], axis=0)
        else:
            ext = jnp.concatenate([R['prv'][0, :, cs] * ps, x, R['nxt'][0, :, cs] * ns], axis=0)
            sh = pltpu.roll(ext, 1 if off == -1 else n_ext - 1, axis=0)[HALO:HALO + tb]
        if cmask is not None:
            sh = sh * col_mask[cmask]
        return x + R['tok_mu'][m:m + 1, hg * GROUP_W:(hg + 1) * GROUP_W] * (sh - x)

    dl = R['dl'][0]
    dlo = jnp.tanh(dl[:, :LANES]).astype(BF16)
    alo = dl[:, LANES:].astype(BF16)
    for hg in range(N_GROUPS):
        cs = slice(hg * GROUP_W, (hg + 1) * GROUP_W)
        r = mixed(hg, 0)
        k = mixed(hg, 1)
        v = mixed(hg, 2)
        if has_vres:
            gate = _sigmoid(R['vres_v0'][:, cs] + _dot(R['lo'][0].astype(BF16), R['vres_b'][:, cs]))
            v = v + (R['vf'][0, :, cs] - v) * gate
        if emit_v:
            R['vmix'][0, :, cs] = v
        kk = k * R['key_k'][:, cs]
        a_d, kd_d, sums_in = [], [], [kk * kk]
        for d in (0, 1):
            a = _sigmoid(R['iclr_a0'][d:d + 1, cs] + _dot(alo, R['iclr_lora'][d, :, cs]))
            ka = R['key_a'][d:d + 1, cs]
            kd = k * ((1.0 - ka) + a * ka)
            a_d.append(a)
            kd_d.append(kd)
            if emit_out:
                sums_in.append(r * kd * R['bonus_rk'][d:d + 1, cs])
        sums = _headsums(sums_in, bm)
        kk = kk / jnp.maximum(jnp.sqrt(sums[0]), 1e-12)
        if emit_out:
            R['bonus'][0, :, cs] = (sums[1] + sums[2]) * v
        R['vop'][0, :, cs] = v.astype(BF16)
        for d in (0, 1):
            z = R['decay_w0'][d:d + 1, cs] + _dot(dlo, R['decay_lora'][d, :, cs])
            lg = -LOG_DECAY_SCALE * _sigmoid(z)
            g1 = lg.astype(BF16)
            g2 = (lg - g1.astype(F32)).astype(BF16)
            cum = _dot(tris[d], g1) + _dot(tris[d], g2)
            bb = kk * a_d[d]
            kd = kd_d[d]
            for c in range(n_chunk):
                sl = slice(c * CHUNK, (c + 1) * CHUNK)
                lc = cum[sl]
                ltot = lc[CHUNK - 1:CHUNK] if d == 0 else lc[0:1]
                e_pos = jnp.exp(lc)
                e_neg = 1.0 / e_pos
                R['ptot'][d, 0, 0, c:c + 1, cs] = jnp.exp(ltot)
                operands = (r[sl] * e_pos, kk[sl] * jnp.exp(lc - lg[sl]), kd[sl] * e_neg, bb[sl] * e_neg)
                for m in range(N_OPERANDS):
                    R['ops'][d, m, 0, sl, cs] = operands[m].astype(BF16)


def _prep_call(u, dl_col, lp, grid_mode, emit_out, vres=None, emit_v=False):
    b, t, _ = u.shape
    tb = min(SCAN_TB, t)
    nt = t // tb
    hb = tb // HALO
    nh = t // HALO
    w3 = 3 * W_A
    has_vres = vres is not None
    assert GROUP_W == W_A // 4 and tb % HALO == 0 and tb % CHUNK == 0 and tb // CHUNK <= SUBLANES
    names, in_specs, args = [], [], []

    def add(name, spec, arr):
        names.append(name)
        in_specs.append(spec)
        args.append(arr)

    add('cur', pl.BlockSpec((1, tb, w3), lambda bi, i: (bi, i, 0)), u)
    add('prv', pl.BlockSpec((1, HALO, w3), lambda bi, i: (bi, jnp.maximum(i * hb - 1, 0), 0)), u)
    add('nxt', pl.BlockSpec((1, HALO, w3), lambda bi, i: (bi, jnp.minimum((i + 1) * hb, nh - 1), 0)), u)
    add('dl', pl.BlockSpec((1, tb, GROUP_W), lambda bi, i: (bi, i, dl_col)), u)
    if has_vres:
        v_first, vres_col, vres_b, vres_v0 = vres
        add('vf', pl.BlockSpec((1, tb, W_A), lambda bi, i: (bi, i, 0)), v_first)
        add('lo', pl.BlockSpec((1, tb, LANES), lambda bi, i: (bi, i, vres_col)), u)
        add('vres_b', pl.BlockSpec((LANES, W_A), lambda bi, i: (0, 0)), vres_b)
        add('vres_v0', pl.BlockSpec((1, W_A), lambda bi, i: (0, 0)), vres_v0)
    vec = lambda rows: pl.BlockSpec((rows, W_A), lambda bi, i: (0, 0))
    lora = pl.BlockSpec((2, LANES, W_A), lambda bi, i: (0, 0, 0))
    add('tok_mu', vec(3), lp['tok_mu'])
    add('decay_w0', vec(2), lp['decay_w0'])
    add('decay_lora', lora, lp['decay_lora'])
    add('iclr_a0', vec(2), lp['iclr_a0'])
    add('iclr_lora', lora, lp['iclr_lora'])
    add('key_k', vec(1), lp['key_k'])
    add('key_a', vec(2), lp['key_a'])
    add('bonus_rk', vec(2), lp['bonus_rk'])
    add('head_mask', pl.BlockSpec((GROUP_W, GROUP_W), lambda bi, i: (0, 0)), lp['head_mask'])

    out_names = ['ops', 'ptot', 'vop']
    out_shape = [jax.ShapeDtypeStruct((2, N_OPERANDS, b, t, W_A), BF16),
                 jax.ShapeDtypeStruct((2, b, nt, SUBLANES, W_A), F32),
                 jax.ShapeDtypeStruct((b, t, W_A), BF16)]
    out_specs = [pl.BlockSpec((2, N_OPERANDS, 1, tb, W_A), lambda bi, i: (0, 0, bi, i, 0)),
                 pl.BlockSpec((2, 1, 1, SUBLANES, W_A), lambda bi, i: (0, bi, i, 0, 0)),
                 pl.BlockSpec((1, tb, W_A), lambda bi, i: (bi, i, 0))]
    for flag, nm in ((emit_out, 'bonus'), (emit_v, 'vmix')):
        if flag:
            out_names.append(nm)
            out_shape.append(jax.ShapeDtypeStruct((b, t, W_A), F32))
            out_specs.append(pl.BlockSpec((1, tb, W_A), lambda bi, i: (bi, i, 0)))

    outs = pl.pallas_call(
        functools.partial(_prep_kernel, names=tuple(names + out_names), tb=tb, grid_mode=grid_mode,
                          has_vres=has_vres, emit_out=emit_out, emit_v=emit_v),
        out_shape=out_shape,
        grid=(b, nt),
        in_specs=in_specs,
        out_specs=out_specs,
        compiler_params=_cparams(("arbitrary", "arbitrary")),
    )(*args)
    return dict(zip(out_names, outs))


def _scan_kernel(*refs, names, tb, n_groups, use_s0, emit_out):
    R = dict(zip(names, refs))
    s_ref = R['s_scr']
    i = pl.program_id(2)
    nt = pl.num_programs(2)

    @pl.when(i == 0)
    def _():
        if use_s0:
            s_ref[...] = R['s0'][:, 0]
        else:
            s_ref[...] = jnp.zeros_like(s_ref)

    lane1 = lax.broadcasted_iota(jnp.int32, (1, LANES), 1)
    half_f = (jnp.where(lane1 < HEAD_DIM, 1.0, 0.0).astype(F32), jnp.where(lane1 >= HEAD_DIM, 1.0, 0.0).astype(F32))
    half_b = (half_f[0].astype(BF16), half_f[1].astype(BF16))
    zero_tile = jnp.zeros((HEAD_DIM, LANES), BF16)
    n_tiles = GROUP_W // LANES

    def bd(x):
        xb = x.astype(BF16)
        rows = []
        for j in range(GROUP_HEADS):
            t = (j * HEAD_DIM) // LANES
            piece = xb[:, t * LANES:(t + 1) * LANES] * half_b[(j * HEAD_DIM % LANES) // HEAD_DIM]
            rows.append(jnp.concatenate([piece if tt == t else zero_tile for tt in range(n_tiles)], axis=1))
        return jnp.concatenate(rows, axis=0)

    def head_blocks(full):
        tiles = []
        for t in range(n_tiles):
            acc = None
            for j in range(GROUP_HEADS):
                if (j * HEAD_DIM) // LANES == t:
                    part = (full[j * HEAD_DIM:(j + 1) * HEAD_DIM, t * LANES:(t + 1) * LANES]
                            * half_f[(j * HEAD_DIM % LANES) // HEAD_DIM])
                    acc = part if acc is None else acc + part
            tiles.append(acc)
        return jnp.concatenate(tiles, axis=1)

    def prod(a, b):
        return _dot(a.astype(BF16), bd(b))

    trow = lax.broadcasted_iota(jnp.int32, (CHUNK, GROUP_W), 0)
    scol = jnp.bitwise_and(lax.broadcasted_iota(jnp.int32, (CHUNK, GROUP_W), 1), CHUNK - 1)
    eye = jnp.where(trow == scol, 1.0, 0.0).astype(F32)
    same16 = (trow >> 4) == (scol >> 4)
    same32 = (trow >> 5) == (scol >> 5)
    tri_masks = []
    for before, upto in ((scol < trow, scol <= trow), (scol > trow, scol >= trow)):
        tri_masks.append(dict(
            neg_d16=jnp.where(jnp.logical_and(before, same16), -1.0, 0.0).astype(F32),
            o32=jnp.where(jnp.logical_and(before, jnp.logical_and(same32, jnp.logical_not(same16))), 1.0, 0.0).astype(BF16),
            o64=jnp.where(jnp.logical_and(before, jnp.logical_not(same32)), 1.0, 0.0).astype(BF16),
            strict=jnp.where(before, 1.0, 0.0).astype(F32),
            incl=jnp.where(upto, 1.0, 0.0).astype(F32)))

    n_chunk = tb // CHUNK
    s_cur = {(d, g): s_ref[d, g] for d in (0, 1) for g in range(n_groups)}

    def unit_stages(d, g, c):
        sl = slice(c * CHUNK, (c + 1) * CHUNK)
        cs = slice(g * GROUP_W, (g + 1) * GROUP_W)
        r_h, kap_h, k_h, b_h = (R[f'ops{d}'][0, m, 0, sl, cs] for m in range(N_OPERANDS))
        vc = R[f'v{d}'][0, sl, cs]
        p_tot = R[f'ptot{d}'][0, 0, 0, c:c + 1, cs]
        p_tot_b = p_tot.astype(BF16)
        k_t = k_h * p_tot_b
        b_t = b_h * p_tot_b
        mk = tri_masks[d]
        x2 = jnp.concatenate([kap_h, r_h], axis=0)
        gb = _dot_nt(x2, bd(b_h))
        yield
        gk = _dot_nt(x2, bd(k_h))
        yield
        ab = gb[:CHUNK]
        abb = ab.astype(BF16)
        n1 = ab * mk['neg_d16']
        o32 = abb * mk['o32']
        o64 = abb * mk['o64']
        a_rb = (gb[CHUNK:] * mk['incl']).astype(BF16)
        a_kk = jnp.concatenate([gk[:CHUNK] * mk['strict'], gk[CHUNK:] * mk['incl']], axis=0).astype(BF16)
        m1y = _dot(a_kk, bd(vc))
        yield
        n1b = n1.astype(BF16)
        nk = prod(n1b, n1b).astype(BF16)
        x = eye + n1
        yield
        for _ in range(2):
            both = _dot(jnp.concatenate([nk, x.astype(BF16)], axis=0), bd(nk))
            nk = both[:CHUNK].astype(BF16)
            x = x + both[CHUNK:]
            yield
        x = x + prod(x, nk)
        yield
        for off in (o32, o64):
            xb = x.astype(BF16)
            xo = prod(xb, off).astype(BF16)
            yield
            x = x - prod(xo, xb)
            yield
        t16 = x.astype(BF16)
        w = _dot(t16, bd(kap_h)).astype(BF16)
        yield
        ut = _dot(t16, bd(m1y[:CHUNK]))
        yield
        s_in = s_cur[d, g]
        wr = _dot_nt(jnp.concatenate([w, r_h], axis=0) if emit_out else w, bd(s_in))
        yield
        u = (-(wr[:CHUNK] + ut)).astype(BF16)
        s_cur[d, g] = s_in * p_tot + head_blocks(_dot_tn(jnp.concatenate([u, vc], axis=0),
                                                         jnp.concatenate([b_t, k_t], axis=0)))
        yield
        if emit_out:
            R[f'y{d}'][0, sl, cs] = wr[CHUNK:] + prod(a_rb, u) + m1y[CHUNK:]

    jobs = [(pos * SCAN_CHUNK_SKEW, unit_stages(d, g, c)) for d in (0, 1) for g in range(n_groups)
            for pos, c in enumerate(range(n_chunk) if d == 0 else range(n_chunk - 1, -1, -1))]
    tick = 0
    while jobs:
        for entry in list(jobs):
            if tick >= entry[0]:
                try:
                    next(entry[1])
                except StopIteration:
                    jobs.remove(entry)
        tick += 1
    for (d, g), s_val in s_cur.items():
        s_ref[d, g] = s_val

    @pl.when(i == nt - 1)
    def _():
        R['sfin'][:, 0] = s_ref[...]


def _scan_call(prep, s0, emit_out):
    ops, ptot = prep['ops'], prep['ptot']
    _, _, b, t, _ = ops.shape
    tb = min(SCAN_TB, t)
    nt = t // tb
    ng = SCAN_GROUPS
    gw = ng * GROUP_W
    npair = N_GROUPS // ng
    use_s0 = s0 is not None
    tblk = (lambda i: i, lambda i: nt - 1 - i)
    names, in_specs, args = [], [], []

    def add(name, spec, arr):
        names.append(name)
        in_specs.append(spec)
        args.append(arr)

    for d in (0, 1):
        add(f'ops{d}', pl.BlockSpec((1, N_OPERANDS, 1, tb, gw), lambda bi, p, i, d=d: (d, 0, bi, tblk[d](i), p)), ops)
        add(f'ptot{d}', pl.BlockSpec((1, 1, 1, SUBLANES, gw), lambda bi, p, i, d=d: (d, bi, tblk[d](i), 0, p)), ptot)
        add(f'v{d}', pl.BlockSpec((1, tb, gw), lambda bi, p, i, d=d: (bi, tblk[d](i), p)), prep['vop'])
    state_spec = pl.BlockSpec((2, 1, ng, HEAD_DIM, GROUP_W), lambda bi, p, i: (0, bi, p, 0, 0))
    if use_s0:
        add('s0', state_spec, s0)

    out_names, out_shape, out_specs = [], [], []
    if emit_out:
        for d in (0, 1):
            out_names.append(f'y{d}')
            out_shape.append(jax.ShapeDtypeStruct((b, t, W_A), F32))
            out_specs.append(pl.BlockSpec((1, tb, gw), lambda bi, p, i, d=d: (bi, tblk[d](i), p)))
    out_names.append('sfin')
    out_shape.append(jax.ShapeDtypeStruct((2, b, N_GROUPS, HEAD_DIM, GROUP_W), F32))
    out_specs.append(state_spec)

    outs = pl.pallas_call(
        functools.partial(_scan_kernel, names=tuple(names + out_names + ['s_scr']), tb=tb, n_groups=ng,
                          use_s0=use_s0, emit_out=emit_out),
        out_shape=out_shape,
        grid=(b, npair, nt),
        in_specs=in_specs,
        out_specs=out_specs,
        scratch_shapes=[pltpu.VMEM((2, ng, HEAD_DIM, GROUP_W), F32)],
        compiler_params=_cparams(("arbitrary", "arbitrary", "arbitrary")),
    )(*args)
    return dict(zip(out_names, outs))


def _post1_kernel(y0, y1, bo, ga, gb, pc, pp, pn, mg0, mg1, gnw, gnb, pw, ps, woa, wob, bm_ref, o_ref, *, tm, t_total):
    i = pl.program_id(1)
    nt = pl.num_programs(1)
    bm = bm_ref[...]

    parts = []
    for hg in range(N_GROUPS):
        cs = slice(hg * GROUP_W, (hg + 1) * GROUP_W)
        y = y0[0, :, cs] + y1[0, :, cs]
        mu = _headsums([y], bm)[0] * (1.0 / HEAD_DIM)
        yc = y - mu
        var = _headsums([yc * yc], bm)[0] * (1.0 / HEAD_DIM)
        yn = yc * lax.rsqrt(var + GN_EPS) * gnw[:, cs] + gnb[:, cs]
        yn = yn + bo[0, :, cs]
        gate = ga[0, :, cs]
        parts.append((yn * (gate * _sigmoid(gate))).astype(BF16))
    y_a = _dot(jnp.concatenate(parts, axis=1), woa[...])

    pscale = jnp.where(i > 0, 1.0, 0.0).astype(F32)
    nscale = jnp.where(i < nt - 1, 1.0, 0.0).astype(F32)
    n_ext = tm + 2 * POOL_HALO
    tpos = (lax.broadcasted_iota(jnp.int32, (tm, 1), 0) + i * tm).astype(F32)
    parts = []
    for gi, win in enumerate(POOL_WINDOWS):
        left = win // 2
        right = win - 1 - left
        cs = slice(gi * POOL_GROUP_W, (gi + 1) * POOL_GROUP_W)
        p = pc[0, :, cs]
        ext = jnp.concatenate([pp[0, :, cs] * pscale, p, pn[0, :, cs] * nscale], axis=0)
        s = ext
        step = 1
        while step < win:
            s = s + pltpu.roll(s, step, axis=0)
            step *= 2
        if right > 0:
            s = pltpu.roll(s, n_ext - right, axis=0)
        s = s[POOL_HALO:POOL_HALO + tm]
        cnt = jnp.minimum(tpos + (right + 1), float(t_total)) - jnp.maximum(tpos - left, 0.0)
        pooled = s / cnt - p
        mixed = _dot(pooled.astype(BF16), pw[gi]) * ps[:, cs]
        gate = gb[0, :, cs]
        parts.append((mixed * (gate * _sigmoid(gate))).astype(BF16))
    y_b = _dot(jnp.concatenate(parts, axis=1), wob[...])

    merged = _sigmoid(mg0[0]) * y_a + _sigmoid(mg1[0]) * y_b
    o_ref[0] = merged.astype(BF16)


def _post1_call(sc, bonus, u, cols, lp):
    y0, y1 = sc['y0'], sc['y1']
    b, t, _ = y0.shape
    d = lp['w_out_a'].shape[1]
    tm = min(256, t)
    nt = t // tm
    hb = tm // POOL_HALO
    nh = t // POOL_HALO
    blk = lambda c: pl.BlockSpec((1, tm, W_A), lambda bi, i: (bi, i, c))
    wide = lambda c: pl.BlockSpec((1, tm, d), lambda bi, i: (bi, i, c))
    const2 = lambda shape: pl.BlockSpec(shape, lambda bi, i: (0, 0))
    pool_col = cols['pool']
    in_specs = [blk(0), blk(0), blk(0),
                blk(cols['gate_a']), blk(cols['gate_b']), blk(pool_col),
                pl.BlockSpec((1, POOL_HALO, W_B), lambda bi, i: (bi, jnp.maximum(i * hb - 1, 0), pool_col)),
                pl.BlockSpec((1, POOL_HALO, W_B), lambda bi, i: (bi, jnp.minimum((i + 1) * hb, nh - 1), pool_col)),
                wide(cols['merge0']), wide(cols['merge1']),
                const2((1, W_A)), const2((1, W_A)),
                pl.BlockSpec((len(POOL_WINDOWS), POOL_GROUP_W, POOL_GROUP_W), lambda bi, i: (0, 0, 0)),
                const2((1, W_B)), const2((W_A, d)), const2((W_B, d)), const2((GROUP_W, GROUP_W))]
    args = [y0, y1, bonus, u, u, u, u, u, u, u,
            lp['gn_w'], lp['gn_b'], lp['pool_w'], lp['pool_scale'], lp['w_out_a'], lp['w_out_b'], lp['head_mask']]
    return pl.pallas_call(
        functools.partial(_post1_kernel, tm=tm, t_total=t),
        out_shape=jax.ShapeDtypeStruct((b, t, d), BF16),
        grid=(b, nt),
        in_specs=in_specs,
        out_specs=pl.BlockSpec((1, tm, d), lambda bi, i: (bi, i, 0)),
        compiler_params=_cparams(("arbitrary", "arbitrary")),
    )(*args)


def _post2_kernel(m_ref, h_ref, gate_ref, w_ref, fg_ref, o_ref, *, final_norm):
    out = _dot(m_ref[0], w_ref[...])
    h = h_ref[0] + gate_ref[0] * out
    if final_norm:
        ms = jnp.mean(h * h, axis=-1, keepdims=True)
        h = h * lax.rsqrt(ms + NORM_EPS) * fg_ref[...]
    o_ref[0] = h


def _post2_call(merged, h, gate, w_out, final_g, final_norm):
    bm, t, d = h.shape
    tm = min(512, t)
    return pl.pallas_call(
        functools.partial(_post2_kernel, final_norm=final_norm),
        out_shape=jax.ShapeDtypeStruct((bm, t, d), F32),
        grid=(bm, t // tm),
        in_specs=[
            pl.BlockSpec((1, tm, d), lambda b, m: (b, m, 0)),
            pl.BlockSpec((1, tm, d), lambda b, m: (b, m, 0)),
            pl.BlockSpec((1, 1, d), lambda b, m: (b, 0, 0)),
            pl.BlockSpec((d, d), lambda b, m: (0, 0)),
            pl.BlockSpec((1, d), lambda b, m: (0, 0)),
        ],
        out_specs=pl.BlockSpec((1, tm, d), lambda b, m: (b, m, 0)),
        compiler_params=_cparams(("arbitrary", "arbitrary")),
    )(merged, h, gate, w_out, final_g)


def _pad_rows(w, rows):
    return jnp.pad(w, ((0, 0),) * (w.ndim - 2) + ((0, rows - w.shape[-2]), (0, 0)))


def kernel(x, c, ctx, c_ctx, ada_w, ada_b, norm_g, w_in, tok_mu, decay_w0, decay_lora_b, iclr_a0, iclr_lora_b, key_k, key_a, bonus_rk, gn_w, gn_b, vres_v0, vres_lora_a, vres_lora_b, pool_w, pool_scale, w_out_a, w_out_b, w_out, final_g):
    depth = w_in.shape[0]
    batch, seq, d = x.shape
    ctx_len = ctx.shape[1]
    assert seq % GRID_W == 0 and seq % CHUNK == 0 and ctx_len % CHUNK == 0
    assert d % LANES == 0

    off_merge = 3 * W_A + W_B + W_A + W_B
    off_dl = off_merge + 2 * d
    off_vres = off_dl + 2 * R_DECAY + 2 * R_ICLR
    nu = -(-(off_vres + LANES) // INPROJ_TN) * INPROJ_TN
    cols = dict(pool=REF_OFF_POOL // W_B, gate_a=REF_OFF_GATE_A // W_A, gate_b=REF_OFF_GATE_B // W_A,
                merge0=off_merge // d, merge1=off_merge // d + 1)
    assert off_merge % d == 0 and off_dl % GROUP_W == 0 and off_vres % LANES == 0

    head_mask = (jnp.arange(GROUP_W)[:, None] // HEAD_DIM == jnp.arange(GROUP_W)[None, :] // HEAD_DIM).astype(BF16)

    rows = -(-(batch + 1) // SUBLANES) * SUBLANES
    c_all = jnp.zeros((rows, d), F32).at[:batch].set(c).at[batch].set(c_ctx)
    mod = _ada_call(c_all, ada_w, ada_b)

    h_lat, h_ctx = x, ctx
    vfirst = {}
    tail = jnp.zeros((depth, d, nu - off_vres), F32)
    if depth > 1:
        tail = tail.at[1:, :, :R_VRES].set(vres_lora_a)
    w_perm = _wcast_call(w_in, tail, 2 * d)
    for l in range(depth):
        last = l == depth - 1

        dlw = decay_lora_b[l]
        dlw = jnp.stack([jnp.pad(dlw[0], ((0, LANES - R_DECAY), (0, 0))),
                         jnp.pad(dlw[1], ((R_DECAY, LANES - 2 * R_DECAY), (0, 0)))]).astype(BF16)
        ilw = iclr_lora_b[l]
        ilw = jnp.stack([jnp.pad(ilw[0], ((0, LANES - R_ICLR), (0, 0))),
                         jnp.pad(ilw[1], ((R_ICLR, LANES - 2 * R_ICLR), (0, 0)))]).astype(BF16)
        lp = dict(tok_mu=tok_mu[l], decay_w0=decay_w0[l], decay_lora=dlw, iclr_a0=iclr_a0[l], iclr_lora=ilw,
                  key_k=key_k[l].reshape(1, W_A), key_a=key_a[l], bonus_rk=bonus_rk[l].reshape(2, W_A),
                  head_mask=head_mask, gn_w=gn_w[l].reshape(1, W_A), gn_b=gn_b[l].reshape(1, W_A),
                  pool_w=pool_w[l].astype(BF16), pool_scale=pool_scale[l].reshape(1, W_B),
                  w_out_a=w_out_a[l].astype(BF16), w_out_b=w_out_b[l].astype(BF16))
        w_out_l = w_out[l].astype(BF16)
        g_l = norm_g[l].reshape(1, d)

        m = mod[l]
        shift_l, scale_l, gate_l = (m[:batch, j * d:(j + 1) * d].reshape(batch, 1, d) for j in range(3))
        shift_c, scale_c, gate_c = (m[batch:batch + 1, j * d:(j + 1) * d].reshape(1, 1, d) for j in range(3))

        u_c = _inproj_call(h_ctx.reshape(1, batch * ctx_len, d), shift_c, scale_c, g_l, w_perm, l)
        u_c = u_c.reshape(batch, ctx_len, nu)
        u_l = _inproj_call(h_lat, shift_l, scale_l, g_l, w_perm, l)

        vres_c = vres_l = None
        if l > 0:
            vb = _pad_rows(vres_lora_b[l - 1], LANES).astype(BF16)
            v0 = vres_v0[l - 1].reshape(1, W_A)
            vres_c = (vfirst['c'], off_vres // LANES, vb, v0)
            vres_l = (vfirst['l'], off_vres // LANES, vb, v0)
        keep_v = l == 0 and depth > 1
        prep_c = _prep_call(u_c, off_dl // GROUP_W, lp, False, not last, vres_c, keep_v)
        prep_l = _prep_call(u_l, off_dl // GROUP_W, lp, True, True, vres_l, keep_v)
        if keep_v:
            vfirst = dict(c=prep_c['vmix'], l=prep_l['vmix'])
        sc_c = _scan_call(prep_c, None, not last)
        sc_l = _scan_call(prep_l, sc_c['sfin'], True)

        merged_l = _post1_call(sc_l, prep_l['bonus'], u_l, cols, lp)
        h_lat = _post2_call(merged_l, h_lat, gate_l, w_out_l, final_g.reshape(1, d), last)
        if not last:
            merged_c = _post1_call(sc_c, prep_c['bonus'], u_c, cols, lp)
            h_ctx = _post2_call(merged_c.reshape(1, batch * ctx_len, d), h_ctx.reshape(1, batch * ctx_len, d),
                                gate_c, w_out_l, final_g.reshape(1, d), False).reshape(batch, ctx_len, d)
    return h_lat
```

```python
import functools
import math

import jax
import jax.numpy as jnp
from jax import lax
from jax.experimental import pallas as pl
from jax.experimental.pallas import tpu as pltpu

F32 = jnp.float32
BF16 = jnp.bfloat16

GRID_W = 64
W_A = 1024
W_B = 1024
HEAD_DIM = 64
POOL_WINDOWS = (2, 4, 8, 16)
POOL_GROUP_W = W_B // len(POOL_WINDOWS)
R_DECAY = 64
R_ICLR = 64
R_VRES = 32
NORM_EPS = 1e-6
GN_EPS = HEAD_DIM * 1e-5
REF_OFF_POOL = 3 * W_A
REF_OFF_GATE_A = REF_OFF_POOL + W_B
REF_OFF_GATE_B = REF_OFF_GATE_A + W_A
REF_OFF_DECAY = REF_OFF_GATE_B + W_B
REF_OFF_ICLR = REF_OFF_DECAY + 2 * R_DECAY
REF_OFF_MERGE = REF_OFF_ICLR + 2 * R_ICLR

LANES = 128
SUBLANES = 8
MXU_DIM = 256
VMEM_LIMIT_BYTES = 56 * 1024 * 1024

GROUP_W = MXU_DIM
GROUP_HEADS = GROUP_W // HEAD_DIM
N_GROUPS = W_A // GROUP_W
SCAN_GROUPS = 4
SCAN_TB = 256
CHUNK = 64
SCAN_CHUNK_SKEW = 2
N_OPERANDS = 4
HALO = 64
POOL_HALO = 8
INPROJ_TN = 1536
INPROJ_NORM_ROWS = 32
LOG_DECAY_SCALE = math.exp(-0.5)


def _cparams(sem):
    return pltpu.CompilerParams(dimension_semantics=sem, vmem_limit_bytes=VMEM_LIMIT_BYTES)


def _sigmoid(x):
    return 1.0 / (1.0 + jnp.exp(-x))


def _dot(a, b):
    return jnp.dot(a, b, preferred_element_type=F32)


def _dot_nt(a, b):
    return lax.dot_general(a, b, (((1,), (1,)), ((), ())), preferred_element_type=F32)


def _dot_tn(a, b):
    return lax.dot_general(a, b, (((0,), (0,)), ((), ())), preferred_element_type=F32)


def _headsums(xs, bm):
    rows = xs[0].shape[0]
    s = _dot(jnp.concatenate([x.astype(BF16) for x in xs], axis=0), bm)
    return [s[n * rows:(n + 1) * rows] for n in range(len(xs))]


def _ada_kernel(c_ref, w_ref, b_ref, o_ref):
    c = c_ref[...]
    s = (c * _sigmoid(c)).astype(BF16)
    o_ref[0] = _dot(s, w_ref[0].astype(BF16)) + b_ref[0]


def _ada_call(c_all, ada_w, ada_b):
    n_layers, d, d3 = ada_w.shape
    rows = c_all.shape[0]
    tn = 768
    return pl.pallas_call(
        _ada_kernel,
        out_shape=jax.ShapeDtypeStruct((n_layers, rows, d3), F32),
        grid=(n_layers, d3 // tn),
        in_specs=[
            pl.BlockSpec((rows, d), lambda l, n: (0, 0)),
            pl.BlockSpec((1, d, tn), lambda l, n: (l, 0, n)),
            pl.BlockSpec((1, 1, tn), lambda l, n: (l, 0, n)),
        ],
        out_specs=pl.BlockSpec((1, rows, tn), lambda l, n: (l, 0, n)),
        compiler_params=_cparams(("arbitrary", "arbitrary")),
    )(c_all, ada_w, ada_b.reshape(n_layers, 1, d3))


def _wcast_kernel(w_ref, t_ref, o_ref, *, n_main):
    o_ref[0] = jnp.where(pl.program_id(1) >= n_main, t_ref[0], w_ref[0]).astype(BF16)


def _wcast_call(w_in, tail, merge_w):
    n_layers, d, width = w_in.shape
    bw = GROUP_W
    n_keep = REF_OFF_DECAY // bw
    n_lora = (REF_OFF_MERGE - REF_OFF_DECAY) // bw
    n_merge = merge_w // bw
    n_main = n_keep + n_merge + n_lora
    n_tail = tail.shape[2] // bw
    assert n_main * bw == width and n_tail * bw == tail.shape[2] and REF_OFF_DECAY % bw == 0 and REF_OFF_MERGE % bw == 0

    def src(j):
        return jnp.where(j < n_keep, j, jnp.where(j < n_keep + n_merge, j + n_lora, jnp.minimum(j - n_merge, n_main - 1)))

    return pl.pallas_call(
        functools.partial(_wcast_kernel, n_main=n_main),
        out_shape=jax.ShapeDtypeStruct((n_layers, d, (n_main + n_tail) * bw), BF16),
        grid=(n_layers, n_main + n_tail),
        in_specs=[pl.BlockSpec((1, d, bw), lambda l, j: (l, 0, src(j))),
                  pl.BlockSpec((1, d, bw), lambda l, j: (l, 0, jnp.maximum(j - n_main, 0)))],
        out_specs=pl.BlockSpec((1, d, bw), lambda l, j: (l, 0, j)),
        compiler_params=_cparams(("arbitrary", "arbitrary")),
    )(w_in, tail)


def _inproj_kernel(h_ref, shift_ref, scale_ref, g_ref, w_ref, o_ref, xn_ref):
    @pl.when(pl.program_id(2) == 0)
    def _():
        def rows(r, carry):
            sl = pl.ds(pl.multiple_of(r * INPROJ_NORM_ROWS, INPROJ_NORM_ROWS), INPROJ_NORM_ROWS)
            x = h_ref[0, sl, :]
            ms = jnp.mean(x * x, axis=-1, keepdims=True)
            y = x * lax.rsqrt(ms + NORM_EPS) * g_ref[...]
            xn_ref[sl, :] = (y * (1.0 + scale_ref[0]) + shift_ref[0]).astype(BF16)
            return carry

        lax.fori_loop(0, h_ref.shape[1] // INPROJ_NORM_ROWS, rows, 0, unroll=8)

    o_ref[0] = _dot(xn_ref[...], w_ref[0])


def _inproj_call(h, shift, scale, g, w, layer):
    bm, t, d = h.shape
    nu = w.shape[2]
    tm = min(1024, t)
    tn = INPROJ_TN
    assert nu % tn == 0
    return pl.pallas_call(
        _inproj_kernel,
        out_shape=jax.ShapeDtypeStruct((bm, t, nu), F32),
        grid=(bm, t // tm, nu // tn),
        in_specs=[
            pl.BlockSpec((1, tm, d), lambda b, m, n: (b, m, 0)),
            pl.BlockSpec((1, 1, d), lambda b, m, n: (b, 0, 0)),
            pl.BlockSpec((1, 1, d), lambda b, m, n: (b, 0, 0)),
            pl.BlockSpec((1, d), lambda b, m, n: (0, 0)),
            pl.BlockSpec((1, d, tn), lambda b, m, n: (layer, 0, n)),
        ],
        out_specs=pl.BlockSpec((1, tm, tn), lambda b, m, n: (b, m, n)),
        scratch_shapes=[pltpu.VMEM((tm, d), BF16)],
        compiler_params=_cparams(("arbitrary", "arbitrary", "arbitrary")),
    )(h, shift, scale, g, w)


def _shift_plan(grid_mode, hg):
    if grid_mode:
        return ((-1, 'first'), (1, 'last'), (-GRID_W, None), (GRID_W, None))[hg]
    return ((-1, None), (-1, None), (1, None), (1, None))[hg]


def _prep_kernel(*refs, names, tb, grid_mode, has_vres, emit_out, emit_v):
    R = dict(zip(names, refs))
    i = pl.program_id(1)
    nt = pl.num_programs(1)
    ps = jnp.where(i > 0, 1.0, 0.0).astype(F32)
    ns = jnp.where(i < nt - 1, 1.0, 0.0).astype(F32)
    n_ext = tb + 2 * HALO
    trow1 = lax.broadcasted_iota(jnp.int32, (tb, 1), 0)
    gcol = jnp.bitwise_and(trow1, GRID_W - 1)
    col_mask = dict(first=jnp.where(gcol != 0, 1.0, 0.0).astype(F32),
                    last=jnp.where(gcol != GRID_W - 1, 1.0, 0.0).astype(F32))
    bm = R['head_mask'][...]
    brow = lax.broadcasted_iota(jnp.int32, (tb, tb), 0)
    bcol = lax.broadcasted_iota(jnp.int32, (tb, tb), 1)
    same_chunk = (brow >> 6) == (bcol >> 6)
    tris = (jnp.where(jnp.logical_and(same_chunk, bcol <= brow), 1.0, 0.0).astype(BF16),
            jnp.where(jnp.logical_and(same_chunk, bcol >= brow), 1.0, 0.0).astype(BF16))
    n_chunk = tb // CHUNK
    R['ptot'][...] = jnp.zeros_like(R['ptot'])

    def mixed(hg, m):
        off, cmask = _shift_plan(grid_mode, hg)
        cs = slice(m * W_A + hg * GROUP_W, m * W_A + (hg + 1) * GROUP_W)
        x = R['cur'][0, :, cs]
        if off == -GRID_W:
            sh = jnp.concatenate([R['prv'][0, :, cs] * ps, x[:tb - HALO]], axis=0)
        elif off == GRID_W:
            sh = jnp.concatenate([x[HALO:], R['nxt'][0, :, cs] * ns], axis=0)
        else:
            ext = jnp.concatenate([R['prv'][0, :, cs] * ps, x, R['nxt'][0, :, cs] * ns], axis=0)
            sh = pltpu.roll(ext, 1 if off == -1 else n_ext - 1, axis=0)[HALO:HALO + tb]
        if cmask is not None:
            sh = sh * col_mask[cmask]
        return x + R['tok_mu'][m:m + 1, hg * GROUP_W:(hg + 1) * GROUP_W] * (sh - x)

    dl = R['dl'][0]
    dlo = jnp.tanh(dl[:, :LANES]).astype(BF16)
    alo = dl[:, LANES:].astype(BF16)
    for hg in range(N_GROUPS):
        cs = slice(hg * GROUP_W, (hg + 1) * GROUP_W)
        r = mixed(hg, 0)
        k = mixed(hg, 1)
        v = mixed(hg, 2)
        if has_vres:
            gate = _sigmoid(R['vres_v0'][:, cs] + _dot(R['lo'][0].astype(BF16), R['vres_b'][:, cs]))
            v = v + (R['vf'][0, :, cs] - v) * gate
        if emit_v:
            R['vmix'][0, :, cs] = v
        kk = k * R['key_k'][:, cs]
        a_d, kd_d, sums_in = [], [], [kk * kk]
        for d in (0, 1):
            a = _sigmoid(R['iclr_a0'][d:d + 1, cs] + _dot(alo, R['iclr_lora'][d, :, cs]))
            ka = R['key_a'][d:d + 1, cs]
            kd = k * ((1.0 - ka) + a * ka)
            a_d.append(a)
            kd_d.append(kd)
            if emit_out:
                sums_in.append(r * kd * R['bonus_rk'][d:d + 1, cs])
        sums = _headsums(sums_in, bm)
        kk = kk / jnp.maximum(jnp.sqrt(sums[0]), 1e-12)
        if emit_out:
            R['bonus'][0, :, cs] = (sums[1] + sums[2]) * v
        R['vop'][0, :, cs] = v.astype(BF16)
        for d in (0, 1):
            z = R['decay_w0'][d:d + 1, cs] + _dot(dlo, R['decay_lora'][d, :, cs])
            lg = -LOG_DECAY_SCALE * _sigmoid(z)
            g1 = lg.astype(BF16)
            g2 = (lg - g1.astype(F32)).astype(BF16)
            cum = _dot(tris[d], g1) + _dot(tris[d], g2)
            bb = kk * a_d[d]
            kd = kd_d[d]
            for c in range(n_chunk):
                sl = slice(c * CHUNK, (c + 1) * CHUNK)
                lc = cum[sl]
                ltot = lc[CHUNK - 1:CHUNK] if d == 0 else lc[0:1]
                e_pos = jnp.exp(lc)
                e_neg = 1.0 / e_pos
                R['ptot'][d, 0, 0, c:c + 1, cs] = jnp.exp(ltot)
                operands = (r[sl] * e_pos, kk[sl] * jnp.exp(lc - lg[sl]), kd[sl] * e_neg, bb[sl] * e_neg)
                for m in range(N_OPERANDS):
                    R['ops'][d, m, 0, sl, cs] = operands[m].astype(BF16)


def _prep_call(u, dl_col, lp, grid_mode, emit_out, vres=None, emit_v=False):
    b, t, _ = u.shape
    tb = min(SCAN_TB, t)
    nt = t // tb
    hb = tb // HALO
    nh = t // HALO
    w3 = 3 * W_A
    has_vres = vres is not None
    assert GROUP_W == W_A // 4 and tb % HALO == 0 and tb % CHUNK == 0 and tb // CHUNK <= SUBLANES
    names, in_specs, args = [], [], []

    def add(name, spec, arr):
        names.append(name)
        in_specs.append(spec)
        args.append(arr)

    add('cur', pl.BlockSpec((1, tb, w3), lambda bi, i: (bi, i, 0)), u)
    add('prv', pl.BlockSpec((1, HALO, w3), lambda bi, i: (bi, jnp.maximum(i * hb - 1, 0), 0)), u)
    add('nxt', pl.BlockSpec((1, HALO, w3), lambda bi, i: (bi, jnp.minimum((i + 1) * hb, nh - 1), 0)), u)
    add('dl', pl.BlockSpec((1, tb, GROUP_W), lambda bi, i: (bi, i, dl_col)), u)
    if has_vres:
        v_first, vres_col, vres_b, vres_v0 = vres
        add('vf', pl.BlockSpec((1, tb, W_A), lambda bi, i: (bi, i, 0)), v_first)
        add('lo', pl.BlockSpec((1, tb, LANES), lambda bi, i: (bi, i, vres_col)), u)
        add('vres_b', pl.BlockSpec((LANES, W_A), lambda bi, i: (0, 0)), vres_b)
        add('vres_v0', pl.BlockSpec((1, W_A), lambda bi, i: (0, 0)), vres_v0)
    vec = lambda rows: pl.BlockSpec((rows, W_A), lambda bi, i: (0, 0))
    lora = pl.BlockSpec((2, LANES, W_A), lambda bi, i: (0, 0, 0))
    add('tok_mu', vec(3), lp['tok_mu'])
    add('decay_w0', vec(2), lp['decay_w0'])
    add('decay_lora', lora, lp['decay_lora'])
    add('iclr_a0', vec(2), lp['iclr_a0'])
    add('iclr_lora', lora, lp['iclr_lora'])
    add('key_k', vec(1), lp['key_k'])
    add('key_a', vec(2), lp['key_a'])
    add('bonus_rk', vec(2), lp['bonus_rk'])
    add('head_mask', pl.BlockSpec((GROUP_W, GROUP_W), lambda bi, i: (0, 0)), lp['head_mask'])

    out_names = ['ops', 'ptot', 'vop']
    out_shape = [jax.ShapeDtypeStruct((2, N_OPERANDS, b, t, W_A), BF16),
                 jax.ShapeDtypeStruct((2, b, nt, SUBLANES, W_A), F32),
                 jax.ShapeDtypeStruct((b, t, W_A), BF16)]
    out_specs = [pl.BlockSpec((2, N_OPERANDS, 1, tb, W_A), lambda bi, i: (0, 0, bi, i, 0)),
                 pl.BlockSpec((2, 1, 1, SUBLANES, W_A), lambda bi, i: (0, bi, i, 0, 0)),
                 pl.BlockSpec((1, tb, W_A), lambda bi, i: (bi, i, 0))]
    for flag, nm in ((emit_out, 'bonus'), (emit_v, 'vmix')):
        if flag:
            out_names.append(nm)
            out_shape.append(jax.ShapeDtypeStruct((b, t, W_A), F32))
            out_specs.append(pl.BlockSpec((1, tb, W_A), lambda bi, i: (bi, i, 0)))

    outs = pl.pallas_call(
        functools.partial(_prep_kernel, names=tuple(names + out_names), tb=tb, grid_mode=grid_mode,
                          has_vres=has_vres, emit_out=emit_out, emit_v=emit_v),
        out_shape=out_shape,
        grid=(b, nt),
        in_specs=in_specs,
        out_specs=out_specs,
        compiler_params=_cparams(("arbitrary", "arbitrary")),
    )(*args)
    return dict(zip(out_names, outs))


def _scan_kernel(*refs, names, tb, n_groups, use_s0):
    R = dict(zip(names, refs))
    s_ref = R['s_scr']
    i = pl.program_id(2)
    nt = pl.num_programs(2)

    @pl.when(i == 0)
    def _():
        if use_s0:
            s_ref[...] = R['s0'][:, 0]
        else:
            s_ref[...] = jnp.zeros_like(s_ref)

    lane1 = lax.broadcasted_iota(jnp.int32, (1, LANES), 1)
    half_f = (jnp.where(lane1 < HEAD_DIM, 1.0, 0.0).astype(F32), jnp.where(lane1 >= HEAD_DIM, 1.0, 0.0).astype(F32))
    half_b = (half_f[0].astype(BF16), half_f[1].astype(BF16))
    zero_tile = jnp.zeros((HEAD_DIM, LANES), BF16)
    n_tiles = GROUP_W // LANES

    def bd(x):
        xb = x.astype(BF16)
        rows = []
        for j in range(GROUP_HEADS):
            t = (j * HEAD_DIM) // LANES
            piece = xb[:, t * LANES:(t + 1) * LANES] * half_b[(j * HEAD_DIM % LANES) // HEAD_DIM]
            rows.append(jnp.concatenate([piece if tt == t else zero_tile for tt in range(n_tiles)], axis=1))
        return jnp.concatenate(rows, axis=0)

    def head_blocks(full):
        tiles = []
        for t in range(n_tiles):
            acc = None
            for j in range(GROUP_HEADS):
                if (j * HEAD_DIM) // LANES == t:
                    part = (full[j * HEAD_DIM:(j + 1) * HEAD_DIM, t * LANES:(t + 1) * LANES]
                            * half_f[(j * HEAD_DIM % LANES) // HEAD_DIM])
                    acc = part if acc is None else acc + part
            tiles.append(acc)
        return jnp.concatenate(tiles, axis=1)

    def prod(a, b):
        return _dot(a.astype(BF16), bd(b))

    trow = lax.broadcasted_iota(jnp.int32, (CHUNK, GROUP_W), 0)
    scol = jnp.bitwise_and(lax.broadcasted_iota(jnp.int32, (CHUNK, GROUP_W), 1), CHUNK - 1)
    eye = jnp.where(trow == scol, 1.0, 0.0).astype(F32)
    same16 = (trow >> 4) == (scol >> 4)
    same32 = (trow >> 5) == (scol >> 5)
    tri_masks = []
    for before, upto in ((scol < trow, scol <= trow), (scol > trow, scol >= trow)):
        tri_masks.append(dict(
            neg_d16=jnp.where(jnp.logical_and(before, same16), -1.0, 0.0).astype(F32),
            o32=jnp.where(jnp.logical_and(before, jnp.logical_and(same32, jnp.logical_not(same16))), 1.0, 0.0).astype(BF16),
            o64=jnp.where(jnp.logical_and(before, jnp.logical_not(same32)), 1.0, 0.0).astype(BF16),
            strict=jnp.where(before, 1.0, 0.0).astype(F32),
            incl=jnp.where(upto, 1.0, 0.0).astype(F32)))

    n_chunk = tb // CHUNK
    s_cur = {(d, g): s_ref[d, g] for d in (0, 1) for g in range(n_groups)}

    def unit_stages(d, g, c):
        sl = slice(c * CHUNK, (c + 1) * CHUNK)
        cs = slice(g * GROUP_W, (g + 1) * GROUP_W)
        r_h, kap_h, k_h, b_h = (R[f'ops{d}'][0, m, 0, sl, cs] for m in range(N_OPERANDS))
        vc = R[f'v{d}'][0, sl, cs]
        p_tot = R[f'ptot{d}'][0, 0, 0, c:c + 1, cs]
        p_tot_b = p_tot.astype(BF16)
        k_t = k_h * p_tot_b
        b_t = b_h * p_tot_b
        mk = tri_masks[d]
        x2 = jnp.concatenate([kap_h, r_h], axis=0)
        gb = _dot_nt(x2, bd(b_h))
        yield
        gk = _dot_nt(x2, bd(k_h))
        yield
        ab = gb[:CHUNK]
        abb = ab.astype(BF16)
        n1 = ab * mk['neg_d16']
        o32 = abb * mk['o32']
        o64 = abb * mk['o64']
        a_rb = (gb[CHUNK:] * mk['incl']).astype(BF16)
        a_kk = jnp.concatenate([gk[:CHUNK] * mk['strict'], gk[CHUNK:] * mk['incl']], axis=0).astype(BF16)
        m1y = _dot(a_kk, bd(vc))
        yield
        n1b = n1.astype(BF16)
        nk = prod(n1b, n1b).astype(BF16)
        x = eye + n1
        yield
        for _ in range(2):
            both = _dot(jnp.concatenate([nk, x.astype(BF16)], axis=0), bd(nk))
            nk = both[:CHUNK].astype(BF16)
            x = x + both[CHUNK:]
            yield
        x = x + prod(x, nk)
        yield
        for off in (o32, o64):
            xb = x.astype(BF16)
            xo = prod(xb, off).astype(BF16)
            yield
            x = x - prod(xo, xb)
            yield
        t16 = x.astype(BF16)
        w = _dot(t16, bd(kap_h)).astype(BF16)
        yield
        ut = _dot(t16, bd(m1y[:CHUNK]))
        yield
        s_in = s_cur[d, g]
        wr = _dot_nt(jnp.concatenate([w, r_h], axis=0), bd(s_in))
        yield
        u = (-(wr[:CHUNK] + ut)).astype(BF16)
        s_cur[d, g] = s_in * p_tot + head_blocks(_dot_tn(jnp.concatenate([u, vc], axis=0),
                                                         jnp.concatenate([b_t, k_t], axis=0)))
        yield
        R[f'y{d}'][0, sl, cs] = wr[CHUNK:] + prod(a_rb, u) + m1y[CHUNK:]

    jobs = [(pos * SCAN_CHUNK_SKEW, unit_stages(d, g, c)) for d in (0, 1) for g in range(n_groups)
            for pos, c in enumerate(range(n_chunk) if d == 0 else range(n_chunk - 1, -1, -1))]
    tick = 0
    while jobs:
        for entry in list(jobs):
            if tick >= entry[0]:
                try:
                    next(entry[1])
                except StopIteration:
                    jobs.remove(entry)
        tick += 1
    for (d, g), s_val in s_cur.items():
        s_ref[d, g] = s_val

    @pl.when(i == nt - 1)
    def _():
        R['sfin'][:, 0] = s_ref[...]


def _scan_call(prep, s0):
    ops, ptot = prep['ops'], prep['ptot']
    _, _, b, t, _ = ops.shape
    tb = min(SCAN_TB, t)
    nt = t // tb
    ng = SCAN_GROUPS
    gw = ng * GROUP_W
    npair = N_GROUPS // ng
    use_s0 = s0 is not None
    tblk = (lambda i: i, lambda i: nt - 1 - i)
    names, in_specs, args = [], [], []

    def add(name, spec, arr):
        names.append(name)
        in_specs.append(spec)
        args.append(arr)

    for d in (0, 1):
        add(f'ops{d}', pl.BlockSpec((1, N_OPERANDS, 1, tb, gw), lambda bi, p, i, d=d: (d, 0, bi, tblk[d](i), p)), ops)
        add(f'ptot{d}', pl.BlockSpec((1, 1, 1, SUBLANES, gw), lambda bi, p, i, d=d: (d, bi, tblk[d](i), 0, p)), ptot)
        add(f'v{d}', pl.BlockSpec((1, tb, gw), lambda bi, p, i, d=d: (bi, tblk[d](i), p)), prep['vop'])
    state_spec = pl.BlockSpec((2, 1, ng, HEAD_DIM, GROUP_W), lambda bi, p, i: (0, bi, p, 0, 0))
    if use_s0:
        add('s0', state_spec, s0)

    out_names, out_shape, out_specs = [], [], []
    for d in (0, 1):
        out_names.append(f'y{d}')
        out_shape.append(jax.ShapeDtypeStruct((b, t, W_A), F32))
        out_specs.append(pl.BlockSpec((1, tb, gw), lambda bi, p, i, d=d: (bi, tblk[d](i), p)))
    out_names.append('sfin')
    out_shape.append(jax.ShapeDtypeStruct((2, b, N_GROUPS, HEAD_DIM, GROUP_W), F32))
    out_specs.append(state_spec)

    outs = pl.pallas_call(
        functools.partial(_scan_kernel, names=tuple(names + out_names + ['s_scr']), tb=tb, n_groups=ng,
                          use_s0=use_s0),
        out_shape=out_shape,
        grid=(b, npair, nt),
        in_specs=in_specs,
        out_specs=out_specs,
        scratch_shapes=[pltpu.VMEM((2, ng, HEAD_DIM, GROUP_W), F32)],
        compiler_params=_cparams(("arbitrary", "arbitrary", "arbitrary")),
    )(*args)
    return dict(zip(out_names, outs))


def _post1_kernel(y0, y1, bo, ga, gb, pc, pp, pn, mg0, mg1, gnw, gnb, pw, ps, woa, wob, bm_ref, o_ref, *, tm, t_total):
    i = pl.program_id(1)
    nt = pl.num_programs(1)
    bm = bm_ref[...]

    parts = []
    for hg in range(N_GROUPS):
        cs = slice(hg * GROUP_W, (hg + 1) * GROUP_W)
        y = y0[0, :, cs] + y1[0, :, cs]
        mu = _headsums([y], bm)[0] * (1.0 / HEAD_DIM)
        yc = y - mu
        var = _headsums([yc * yc], bm)[0] * (1.0 / HEAD_DIM)
        yn = yc * lax.rsqrt(var + GN_EPS) * gnw[:, cs] + gnb[:, cs]
        yn = yn + bo[0, :, cs]
        gate = ga[0, :, cs]
        parts.append((yn * (gate * _sigmoid(gate))).astype(BF16))
    y_a = _dot(jnp.concatenate(parts, axis=1), woa[...])

    pscale = jnp.where(i > 0, 1.0, 0.0).astype(F32)
    nscale = jnp.where(i < nt - 1, 1.0, 0.0).astype(F32)
    n_ext = tm + 2 * POOL_HALO
    tpos = (lax.broadcasted_iota(jnp.int32, (tm, 1), 0) + i * tm).astype(F32)
    parts = []
    for gi, win in enumerate(POOL_WINDOWS):
        left = win // 2
        right = win - 1 - left
        cs = slice(gi * POOL_GROUP_W, (gi + 1) * POOL_GROUP_W)
        p = pc[0, :, cs]
        ext = jnp.concatenate([pp[0, :, cs] * pscale, p, pn[0, :, cs] * nscale], axis=0)
        s = ext
        step = 1
        while step < win:
            s = s + pltpu.roll(s, step, axis=0)
            step *= 2
        if right > 0:
            s = pltpu.roll(s, n_ext - right, axis=0)
        s = s[POOL_HALO:POOL_HALO + tm]
        cnt = jnp.minimum(tpos + (right + 1), float(t_total)) - jnp.maximum(tpos - left, 0.0)
        pooled = s / cnt - p
        mixed = _dot(pooled.astype(BF16), pw[gi]) * ps[:, cs]
        gate = gb[0, :, cs]
        parts.append((mixed * (gate * _sigmoid(gate))).astype(BF16))
    y_b = _dot(jnp.concatenate(parts, axis=1), wob[...])

    merged = _sigmoid(mg0[0]) * y_a + _sigmoid(mg1[0]) * y_b
    o_ref[0] = merged.astype(BF16)


def _post1_call(sc, bonus, u, cols, lp):
    y0, y1 = sc['y0'], sc['y1']
    b, t, _ = y0.shape
    d = lp['w_out_a'].shape[1]
    tm = min(256, t)
    nt = t // tm
    hb = tm // POOL_HALO
    nh = t // POOL_HALO
    blk = lambda c: pl.BlockSpec((1, tm, W_A), lambda bi, i: (bi, i, c))
    wide = lambda c: pl.BlockSpec((1, tm, d), lambda bi, i: (bi, i, c))
    const2 = lambda shape: pl.BlockSpec(shape, lambda bi, i: (0, 0))
    pool_col = cols['pool']
    in_specs = [blk(0), blk(0), blk(0),
                blk(cols['gate_a']), blk(cols['gate_b']), blk(pool_col),
                pl.BlockSpec((1, POOL_HALO, W_B), lambda bi, i: (bi, jnp.maximum(i * hb - 1, 0), pool_col)),
                pl.BlockSpec((1, POOL_HALO, W_B), lambda bi, i: (bi, jnp.minimum((i + 1) * hb, nh - 1), pool_col)),
                wide(cols['merge0']), wide(cols['merge1']),
                const2((1, W_A)), const2((1, W_A)),
                pl.BlockSpec((len(POOL_WINDOWS), POOL_GROUP_W, POOL_GROUP_W), lambda bi, i: (0, 0, 0)),
                const2((1, W_B)), const2((W_A, d)), const2((W_B, d)), const2((GROUP_W, GROUP_W))]
    args = [y0, y1, bonus, u, u, u, u, u, u, u,
            lp['gn_w'], lp['gn_b'], lp['pool_w'], lp['pool_scale'], lp['w_out_a'], lp['w_out_b'], lp['head_mask']]
    return pl.pallas_call(
        functools.partial(_post1_kernel, tm=tm, t_total=t),
        out_shape=jax.ShapeDtypeStruct((b, t, d), BF16),
        grid=(b, nt),
        in_specs=in_specs,
        out_specs=pl.BlockSpec((1, tm, d), lambda bi, i: (bi, i, 0)),
        compiler_params=_cparams(("arbitrary", "arbitrary")),
    )(*args)


def _post2_kernel(m_ref, h_ref, gate_ref, w_ref, fg_ref, o_ref, *, final_norm):
    out = _dot(m_ref[0], w_ref[...])
    h = h_ref[0] + gate_ref[0] * out
    if final_norm:
        ms = jnp.mean(h * h, axis=-1, keepdims=True)
        h = h * lax.rsqrt(ms + NORM_EPS) * fg_ref[...]
    o_ref[0] = h


def _post2_call(merged, h, gate, w_out, final_g, final_norm):
    bm, t, d = h.shape
    tm = min(512, t)
    return pl.pallas_call(
        functools.partial(_post2_kernel, final_norm=final_norm),
        out_shape=jax.ShapeDtypeStruct((bm, t, d), F32),
        grid=(bm, t // tm),
        in_specs=[
            pl.BlockSpec((1, tm, d), lambda b, m: (b, m, 0)),
            pl.BlockSpec((1, tm, d), lambda b, m: (b, m, 0)),
            pl.BlockSpec((1, 1, d), lambda b, m: (b, 0, 0)),
            pl.BlockSpec((d, d), lambda b, m: (0, 0)),
            pl.BlockSpec((1, d), lambda b, m: (0, 0)),
        ],
        out_specs=pl.BlockSpec((1, tm, d), lambda b, m: (b, m, 0)),
        compiler_params=_cparams(("arbitrary", "arbitrary")),
    )(merged, h, gate, w_out, final_g)


def _pad_rows(w, rows):
    return jnp.pad(w, ((0, 0),) * (w.ndim - 2) + ((0, rows - w.shape[-2]), (0, 0)))


def kernel(x, c, ctx, c_ctx, ada_w, ada_b, norm_g, w_in, tok_mu, decay_w0, decay_lora_b, iclr_a0, iclr_lora_b, key_k, key_a, bonus_rk, gn_w, gn_b, vres_v0, vres_lora_a, vres_lora_b, pool_w, pool_scale, w_out_a, w_out_b, w_out, final_g):
    depth = w_in.shape[0]
    batch, seq, d = x.shape
    ctx_len = ctx.shape[1]
    assert seq % GRID_W == 0 and seq % CHUNK == 0 and ctx_len % CHUNK == 0
    assert d % LANES == 0

    off_merge = 3 * W_A + W_B + W_A + W_B
    off_dl = off_merge + 2 * d
    off_vres = off_dl + 2 * R_DECAY + 2 * R_ICLR
    nu = -(-(off_vres + LANES) // INPROJ_TN) * INPROJ_TN
    cols = dict(pool=REF_OFF_POOL // W_B, gate_a=REF_OFF_GATE_A // W_A, gate_b=REF_OFF_GATE_B // W_A,
                merge0=off_merge // d, merge1=off_merge // d + 1)
    assert off_merge % d == 0 and off_dl % GROUP_W == 0 and off_vres % LANES == 0

    head_mask = (jnp.arange(GROUP_W)[:, None] // HEAD_DIM == jnp.arange(GROUP_W)[None, :] // HEAD_DIM).astype(BF16)

    rows = -(-(batch + 1) // SUBLANES) * SUBLANES
    c_all = jnp.zeros((rows, d), F32).at[:batch].set(c).at[batch].set(c_ctx)
    mod = _ada_call(c_all, ada_w, ada_b)

    h_lat, h_ctx = x, ctx
    vfirst = {}
    tail = jnp.zeros((depth, d, nu - off_vres), F32)
    if depth > 1:
        tail = tail.at[1:, :, :R_VRES].set(vres_lora_a)
    w_perm = _wcast_call(w_in, tail, 2 * d)
    for l in range(depth):
        last = l == depth - 1

        dlw = decay_lora_b[l]
        dlw = jnp.stack([jnp.pad(dlw[0], ((0, LANES - R_DECAY), (0, 0))),
                         jnp.pad(dlw[1], ((R_DECAY, LANES - 2 * R_DECAY), (0, 0)))]).astype(BF16)
        ilw = iclr_lora_b[l]
        ilw = jnp.stack([jnp.pad(ilw[0], ((0, LANES - R_ICLR), (0, 0))),
                         jnp.pad(ilw[1], ((R_ICLR, LANES - 2 * R_ICLR), (0, 0)))]).astype(BF16)
        lp = dict(tok_mu=tok_mu[l], decay_w0=decay_w0[l], decay_lora=dlw, iclr_a0=iclr_a0[l], iclr_lora=ilw,
                  key_k=key_k[l].reshape(1, W_A), key_a=key_a[l], bonus_rk=bonus_rk[l].reshape(2, W_A),
                  head_mask=head_mask, gn_w=gn_w[l].reshape(1, W_A), gn_b=gn_b[l].reshape(1, W_A),
                  pool_w=pool_w[l].astype(BF16), pool_scale=pool_scale[l].reshape(1, W_B),
                  w_out_a=w_out_a[l].astype(BF16), w_out_b=w_out_b[l].astype(BF16))
        w_out_l = w_out[l].astype(BF16)
        g_l = norm_g[l].reshape(1, d)

        m = mod[l]
        shift_l, scale_l, gate_l = (m[:batch, j * d:(j + 1) * d].reshape(batch, 1, d) for j in range(3))
        shift_c, scale_c, gate_c = (m[batch:batch + 1, j * d:(j + 1) * d].reshape(1, 1, d) for j in range(3))

        u_c = _inproj_call(h_ctx.reshape(1, batch * ctx_len, d), shift_c, scale_c, g_l, w_perm, l)
        u_c = u_c.reshape(batch, ctx_len, nu)
        u_l = _inproj_call(h_lat, shift_l, scale_l, g_l, w_perm, l)

        vres_c = vres_l = None
        if l > 0:
            vb = _pad_rows(vres_lora_b[l - 1], LANES).astype(BF16)
            v0 = vres_v0[l - 1].reshape(1, W_A)
            vres_c = (vfirst['c'], off_vres // LANES, vb, v0)
            vres_l = (vfirst['l'], off_vres // LANES, vb, v0)
        keep_v = l == 0 and depth > 1
        prep_c = _prep_call(u_c, off_dl // GROUP_W, lp, False, not last, vres_c, keep_v)
        prep_l = _prep_call(u_l, off_dl // GROUP_W, lp, True, True, vres_l, keep_v)
        if keep_v:
            vfirst = dict(c=prep_c['vmix'], l=prep_l['vmix'])
        sc_c = _scan_call(prep_c, None)
        sc_l = _scan_call(prep_l, sc_c['sfin'])

        merged_l = _post1_call(sc_l, prep_l['bonus'], u_l, cols, lp)
        h_lat = _post2_call(merged_l, h_lat, gate_l, w_out_l, final_g.reshape(1, d), last)
        if not last:
            merged_c = _post1_call(sc_c, prep_c['bonus'], u_c, cols, lp)
            h_ctx = _post2_call(merged_c.reshape(1, batch * ctx_len, d), h_ctx.reshape(1, batch * ctx_len, d),
                                gate_c, w_out_l, final_g.reshape(1, d), False).reshape(batch, ctx_len, d)
    return h_lat
```

```python
import functools
import math

import jax
import jax.numpy as jnp
from jax import lax
from jax.experimental import pallas as pl
from jax.experimental.pallas import tpu as pltpu

F32 = jnp.float32
BF16 = jnp.bfloat16

GRID_W = 64
W_A = 1024
W_B = 1024
HEAD_DIM = 64
POOL_WINDOWS = (2, 4, 8, 16)
POOL_GROUP_W = W_B // len(POOL_WINDOWS)
R_DECAY = 64
R_ICLR = 64
R_VRES = 32
NORM_EPS = 1e-6
GN_EPS = HEAD_DIM * 1e-5
REF_OFF_POOL = 3 * W_A
REF_OFF_GATE_A = REF_OFF_POOL + W_B
REF_OFF_GATE_B = REF_OFF_GATE_A + W_A
REF_OFF_DECAY = REF_OFF_GATE_B + W_B
REF_OFF_ICLR = REF_OFF_DECAY + 2 * R_DECAY
REF_OFF_MERGE = REF_OFF_ICLR + 2 * R_ICLR

LANES = 128
SUBLANES = 8
MXU_DIM = 256
VMEM_LIMIT_BYTES = 56 * 1024 * 1024

GROUP_W = MXU_DIM
GROUP_HEADS = GROUP_W // HEAD_DIM
N_GROUPS = W_A // GROUP_W
SCAN_GROUPS = 4
SCAN_TB = 256
CHUNK = 64
SCAN_CHUNK_SKEW = 2
N_OPERANDS = 4
HALO = 64
POOL_HALO = 8
INPROJ_TM = 1024
INPROJ_TN = 1536
ADA_TN = 768
POST1_TM = 256
POST2_TM = 512
INPROJ_NORM_ROWS = 16
LOG_DECAY_SCALE = math.exp(-0.5)


def _cparams(sem):
    return pltpu.CompilerParams(dimension_semantics=sem, vmem_limit_bytes=VMEM_LIMIT_BYTES)


def _sigmoid(x):
    return 1.0 / (1.0 + jnp.exp(-x))


def _dot(a, b):
    return jnp.dot(a, b, preferred_element_type=F32)


def _dot_nt(a, b):
    return lax.dot_general(a, b, (((1,), (1,)), ((), ())), preferred_element_type=F32)


def _dot_tn(a, b):
    return lax.dot_general(a, b, (((0,), (0,)), ((), ())), preferred_element_type=F32)


def _headsums(xs, bm):
    rows = xs[0].shape[0]
    s = _dot(jnp.concatenate([x.astype(BF16) for x in xs], axis=0), bm)
    return [s[n * rows:(n + 1) * rows] for n in range(len(xs))]


def _ada_kernel(c_ref, w_ref, b_ref, o_ref):
    c = c_ref[...]
    s = (c * _sigmoid(c)).astype(BF16)
    o_ref[0] = _dot(s, w_ref[0].astype(BF16)) + b_ref[0]


def _ada_call(c_all, ada_w, ada_b):
    n_layers, d, d3 = ada_w.shape
    rows = c_all.shape[0]
    tn = ADA_TN
    assert d3 % tn == 0
    return pl.pallas_call(
        _ada_kernel,
        out_shape=jax.ShapeDtypeStruct((n_layers, rows, d3), F32),
        grid=(n_layers, d3 // tn),
        in_specs=[
            pl.BlockSpec((rows, d), lambda l, n: (0, 0)),
            pl.BlockSpec((1, d, tn), lambda l, n: (l, 0, n)),
            pl.BlockSpec((1, 1, tn), lambda l, n: (l, 0, n)),
        ],
        out_specs=pl.BlockSpec((1, rows, tn), lambda l, n: (l, 0, n)),
        compiler_params=_cparams(("arbitrary", "arbitrary")),
    )(c_all, ada_w, ada_b.reshape(n_layers, 1, d3))


def _wcast_kernel(w_ref, t_ref, o_ref, *, n_main):
    o_ref[0] = jnp.where(pl.program_id(1) >= n_main, t_ref[0], w_ref[0]).astype(BF16)


def _wcast_call(w_in, tail, merge_w):
    n_layers, d, width = w_in.shape
    bw = GROUP_W
    n_keep = REF_OFF_DECAY // bw
    n_lora = (REF_OFF_MERGE - REF_OFF_DECAY) // bw
    n_merge = merge_w // bw
    n_main = n_keep + n_merge + n_lora
    n_tail = tail.shape[2] // bw
    assert n_main * bw == width and n_tail * bw == tail.shape[2] and REF_OFF_DECAY % bw == 0 and REF_OFF_MERGE % bw == 0

    def src(j):
        return jnp.where(j < n_keep, j, jnp.where(j < n_keep + n_merge, j + n_lora, jnp.minimum(j - n_merge, n_main - 1)))

    return pl.pallas_call(
        functools.partial(_wcast_kernel, n_main=n_main),
        out_shape=jax.ShapeDtypeStruct((n_layers, d, (n_main + n_tail) * bw), BF16),
        grid=(n_layers, n_main + n_tail),
        in_specs=[pl.BlockSpec((1, d, bw), lambda l, j: (l, 0, src(j))),
                  pl.BlockSpec((1, d, bw), lambda l, j: (l, 0, jnp.maximum(j - n_main, 0)))],
        out_specs=pl.BlockSpec((1, d, bw), lambda l, j: (l, 0, j)),
        compiler_params=_cparams(("arbitrary", "arbitrary")),
    )(w_in, tail)


def _inproj_kernel(h_ref, shift_ref, scale_ref, g_ref, w_ref, o_ref, xn_ref):
    @pl.when(pl.program_id(2) == 0)
    def _():
        def rows(r, carry):
            sl = pl.ds(pl.multiple_of(r * INPROJ_NORM_ROWS, INPROJ_NORM_ROWS), INPROJ_NORM_ROWS)
            x = h_ref[0, sl, :]
            ms = jnp.mean(x * x, axis=-1, keepdims=True)
            y = x * lax.rsqrt(ms + NORM_EPS) * g_ref[...]
            xn_ref[sl, :] = (y * (1.0 + scale_ref[0]) + shift_ref[0]).astype(BF16)
            return carry

        lax.fori_loop(0, h_ref.shape[1] // INPROJ_NORM_ROWS, rows, 0, unroll=16)

    o_ref[0] = _dot(xn_ref[...], w_ref[0])


def _inproj_call(h, shift, scale, g, w, layer):
    bm, t, d = h.shape
    nu = w.shape[2]
    tm = min(INPROJ_TM, t)
    tn = INPROJ_TN
    assert nu % tn == 0
    return pl.pallas_call(
        _inproj_kernel,
        out_shape=jax.ShapeDtypeStruct((bm, t, nu), F32),
        grid=(bm, t // tm, nu // tn),
        in_specs=[
            pl.BlockSpec((1, tm, d), lambda b, m, n: (b, m, 0)),
            pl.BlockSpec((1, 1, d), lambda b, m, n: (b, 0, 0)),
            pl.BlockSpec((1, 1, d), lambda b, m, n: (b, 0, 0)),
            pl.BlockSpec((1, d), lambda b, m, n: (0, 0)),
            pl.BlockSpec((1, d, tn), lambda b, m, n: (layer, 0, n)),
        ],
        out_specs=pl.BlockSpec((1, tm, tn), lambda b, m, n: (b, m, n)),
        scratch_shapes=[pltpu.VMEM((tm, d), BF16)],
        compiler_params=_cparams(("arbitrary", "arbitrary", "arbitrary")),
    )(h, shift, scale, g, w)


def _shift_plan(grid_mode, hg):
    if grid_mode:
        return ((-1, 'first'), (1, 'last'), (-GRID_W, None), (GRID_W, None))[hg]
    return ((-1, None), (-1, None), (1, None), (1, None))[hg]


def _prep_kernel(*refs, names, tb, grid_mode, has_vres, emit_out, emit_v):
    R = dict(zip(names, refs))
    i = pl.program_id(1)
    nt = pl.num_programs(1)
    ps = jnp.where(i > 0, 1.0, 0.0).astype(F32)
    ns = jnp.where(i < nt - 1, 1.0, 0.0).astype(F32)
    n_ext = tb + 2 * HALO
    trow1 = lax.broadcasted_iota(jnp.int32, (tb, 1), 0)
    gcol = jnp.bitwise_and(trow1, GRID_W - 1)
    col_mask = dict(first=jnp.where(gcol != 0, 1.0, 0.0).astype(F32),
                    last=jnp.where(gcol != GRID_W - 1, 1.0, 0.0).astype(F32))
    bm = R['head_mask'][...]
    brow = lax.broadcasted_iota(jnp.int32, (tb, tb), 0)
    bcol = lax.broadcasted_iota(jnp.int32, (tb, tb), 1)
    same_chunk = (brow >> 6) == (bcol >> 6)
    tris = (jnp.where(jnp.logical_and(same_chunk, bcol <= brow), 1.0, 0.0).astype(BF16),
            jnp.where(jnp.logical_and(same_chunk, bcol >= brow), 1.0, 0.0).astype(BF16))
    n_chunk = tb // CHUNK
    R['ptot'][...] = jnp.zeros_like(R['ptot'])

    def mixed(hg, m):
        off, cmask = _shift_plan(grid_mode, hg)
        cs = slice(m * W_A + hg * GROUP_W, m * W_A + (hg + 1) * GROUP_W)
        x = R['cur'][0, :, cs]
        if off == -GRID_W:
            sh = jnp.concatenate([R['prv'][0, :, cs] * ps, x[:tb - HALO]], axis=0)
        elif off == GRID_W:
            sh = jnp.concatenate([x[HALO:], R['nxt'][0, :, cs] * ns], axis=0)
        else:
            ext = jnp.concatenate([R['prv'][0, :, cs] * ps, x, R['nxt'][0, :, cs] * ns], axis=0)
            sh = pltpu.roll(ext, 1 if off == -1 else n_ext - 1, axis=0)[HALO:HALO + tb]
        if cmask is not None:
            sh = sh * col_mask[cmask]
        return x + R['tok_mu'][m:m + 1, hg * GROUP_W:(hg + 1) * GROUP_W] * (sh - x)

    dl = R['dl'][0]
    dlo = jnp.tanh(dl[:, :LANES]).astype(BF16)
    alo = dl[:, LANES:].astype(BF16)
    for hg in range(N_GROUPS):
        cs = slice(hg * GROUP_W, (hg + 1) * GROUP_W)
        r = mixed(hg, 0)
        k = mixed(hg, 1)
        v = mixed(hg, 2)
        if has_vres:
            gate = _sigmoid(R['vres_v0'][:, cs] + _dot(R['lo'][0].astype(BF16), R['vres_b'][:, cs]))
            v = v + (R['vf'][0, :, cs] - v) * gate
        if emit_v:
            R['vmix'][0, :, cs] = v
        kk = k * R['key_k'][:, cs]
        a_d, kd_d, sums_in = [], [], [kk * kk]
        for d in (0, 1):
            a = _sigmoid(R['iclr_a0'][d:d + 1, cs] + _dot(alo, R['iclr_lora'][d, :, cs]))
            ka = R['key_a'][d:d + 1, cs]
            kd = k * ((1.0 - ka) + a * ka)
            a_d.append(a)
            kd_d.append(kd)
            if emit_out:
                sums_in.append(r * kd * R['bonus_rk'][d:d + 1, cs])
        sums = _headsums(sums_in, bm)
        kk = kk / jnp.maximum(jnp.sqrt(sums[0]), 1e-12)
        if emit_out:
            R['bonus'][0, :, cs] = (sums[1] + sums[2]) * v
        R['vop'][0, :, cs] = v.astype(BF16)
        for d in (0, 1):
            z = R['decay_w0'][d:d + 1, cs] + _dot(dlo, R['decay_lora'][d, :, cs])
            lg = -LOG_DECAY_SCALE * _sigmoid(z)
            g1 = lg.astype(BF16)
            g2 = (lg - g1.astype(F32)).astype(BF16)
            cum = _dot(tris[d], g1) + _dot(tris[d], g2)
            bb = kk * a_d[d]
            kd = kd_d[d]
            for c in range(n_chunk):
                sl = slice(c * CHUNK, (c + 1) * CHUNK)
                lc = cum[sl]
                ltot = lc[CHUNK - 1:CHUNK] if d == 0 else lc[0:1]
                e_pos = jnp.exp(lc)
                e_neg = 1.0 / e_pos
                R['ptot'][d, 0, 0, c:c + 1, cs] = jnp.exp(ltot)
                operands = (r[sl] * e_pos, kk[sl] * jnp.exp(lc - lg[sl]), kd[sl] * e_neg, bb[sl] * e_neg)
                for m in range(N_OPERANDS):
                    R['ops'][d, m, 0, sl, cs] = operands[m].astype(BF16)


def _prep_call(u, dl_col, lp, grid_mode, emit_out, vres=None, emit_v=False):
    b, t, _ = u.shape
    tb = min(SCAN_TB, t)
    nt = t // tb
    hb = tb // HALO
    nh = t // HALO
    w3 = 3 * W_A
    has_vres = vres is not None
    assert GROUP_W == W_A // 4 and tb % HALO == 0 and tb % CHUNK == 0 and tb // CHUNK <= SUBLANES
    names, in_specs, args = [], [], []

    def add(name, spec, arr):
        names.append(name)
        in_specs.append(spec)
        args.append(arr)

    add('cur', pl.BlockSpec((1, tb, w3), lambda bi, i: (bi, i, 0)), u)
    add('prv', pl.BlockSpec((1, HALO, w3), lambda bi, i: (bi, jnp.maximum(i * hb - 1, 0), 0)), u)
    add('nxt', pl.BlockSpec((1, HALO, w3), lambda bi, i: (bi, jnp.minimum((i + 1) * hb, nh - 1), 0)), u)
    add('dl', pl.BlockSpec((1, tb, GROUP_W), lambda bi, i: (bi, i, dl_col)), u)
    if has_vres:
        v_first, vres_col, vres_b, vres_v0 = vres
        add('vf', pl.BlockSpec((1, tb, W_A), lambda bi, i: (bi, i, 0)), v_first)
        add('lo', pl.BlockSpec((1, tb, LANES), lambda bi, i: (bi, i, vres_col)), u)
        add('vres_b', pl.BlockSpec((LANES, W_A), lambda bi, i: (0, 0)), vres_b)
        add('vres_v0', pl.BlockSpec((1, W_A), lambda bi, i: (0, 0)), vres_v0)
    vec = lambda rows: pl.BlockSpec((rows, W_A), lambda bi, i: (0, 0))
    lora = pl.BlockSpec((2, LANES, W_A), lambda bi, i: (0, 0, 0))
    add('tok_mu', vec(3), lp['tok_mu'])
    add('decay_w0', vec(2), lp['decay_w0'])
    add('decay_lora', lora, lp['decay_lora'])
    add('iclr_a0', vec(2), lp['iclr_a0'])
    add('iclr_lora', lora, lp['iclr_lora'])
    add('key_k', vec(1), lp['key_k'])
    add('key_a', vec(2), lp['key_a'])
    add('bonus_rk', vec(2), lp['bonus_rk'])
    add('head_mask', pl.BlockSpec((GROUP_W, GROUP_W), lambda bi, i: (0, 0)), lp['head_mask'])

    out_names = ['ops', 'ptot', 'vop']
    out_shape = [jax.ShapeDtypeStruct((2, N_OPERANDS, b, t, W_A), BF16),
                 jax.ShapeDtypeStruct((2, b, nt, SUBLANES, W_A), F32),
                 jax.ShapeDtypeStruct((b, t, W_A), BF16)]
    out_specs = [pl.BlockSpec((2, N_OPERANDS, 1, tb, W_A), lambda bi, i: (0, 0, bi, i, 0)),
                 pl.BlockSpec((2, 1, 1, SUBLANES, W_A), lambda bi, i: (0, bi, i, 0, 0)),
                 pl.BlockSpec((1, tb, W_A), lambda bi, i: (bi, i, 0))]
    for flag, nm in ((emit_out, 'bonus'), (emit_v, 'vmix')):
        if flag:
            out_names.append(nm)
            out_shape.append(jax.ShapeDtypeStruct((b, t, W_A), F32))
            out_specs.append(pl.BlockSpec((1, tb, W_A), lambda bi, i: (bi, i, 0)))

    outs = pl.pallas_call(
        functools.partial(_prep_kernel, names=tuple(names + out_names), tb=tb, grid_mode=grid_mode,
                          has_vres=has_vres, emit_out=emit_out, emit_v=emit_v),
        out_shape=out_shape,
        grid=(b, nt),
        in_specs=in_specs,
        out_specs=out_specs,
        compiler_params=_cparams(("arbitrary", "arbitrary")),
    )(*args)
    return dict(zip(out_names, outs))


def _scan_kernel(*refs, names, tb, n_groups, use_s0):
    R = dict(zip(names, refs))
    s_ref = R['s_scr']
    i = pl.program_id(2)
    nt = pl.num_programs(2)

    @pl.when(i == 0)
    def _():
        if use_s0:
            s_ref[...] = R['s0'][:, 0]
        else:
            s_ref[...] = jnp.zeros_like(s_ref)

    lane1 = lax.broadcasted_iota(jnp.int32, (1, LANES), 1)
    half_f = (jnp.where(lane1 < HEAD_DIM, 1.0, 0.0).astype(F32), jnp.where(lane1 >= HEAD_DIM, 1.0, 0.0).astype(F32))
    half_b = (half_f[0].astype(BF16), half_f[1].astype(BF16))
    zero_tile = jnp.zeros((HEAD_DIM, LANES), BF16)
    n_tiles = GROUP_W // LANES

    def bd(x):
        xb = x.astype(BF16)
        rows = []
        for j in range(GROUP_HEADS):
            t = (j * HEAD_DIM) // LANES
            piece = xb[:, t * LANES:(t + 1) * LANES] * half_b[(j * HEAD_DIM % LANES) // HEAD_DIM]
            rows.append(jnp.concatenate([piece if tt == t else zero_tile for tt in range(n_tiles)], axis=1))
        return jnp.concatenate(rows, axis=0)

    def head_blocks(full):
        tiles = []
        for t in range(n_tiles):
            acc = None
            for j in range(GROUP_HEADS):
                if (j * HEAD_DIM) // LANES == t:
                    part = (full[j * HEAD_DIM:(j + 1) * HEAD_DIM, t * LANES:(t + 1) * LANES]
                            * half_f[(j * HEAD_DIM % LANES) // HEAD_DIM])
                    acc = part if acc is None else acc + part
            tiles.append(acc)
        return jnp.concatenate(tiles, axis=1)

    def prod(a, b):
        return _dot(a.astype(BF16), bd(b))

    trow = lax.broadcasted_iota(jnp.int32, (CHUNK, GROUP_W), 0)
    scol = jnp.bitwise_and(lax.broadcasted_iota(jnp.int32, (CHUNK, GROUP_W), 1), CHUNK - 1)
    eye = jnp.where(trow == scol, 1.0, 0.0).astype(F32)
    same16 = (trow >> 4) == (scol >> 4)
    same32 = (trow >> 5) == (scol >> 5)
    tri_masks = []
    for before, upto in ((scol < trow, scol <= trow), (scol > trow, scol >= trow)):
        tri_masks.append(dict(
            neg_d16=jnp.where(jnp.logical_and(before, same16), -1.0, 0.0).astype(F32),
            o32=jnp.where(jnp.logical_and(before, jnp.logical_and(same32, jnp.logical_not(same16))), 1.0, 0.0).astype(BF16),
            o64=jnp.where(jnp.logical_and(before, jnp.logical_not(same32)), 1.0, 0.0).astype(BF16),
            strict=jnp.where(before, 1.0, 0.0).astype(F32),
            incl=jnp.where(upto, 1.0, 0.0).astype(F32)))

    n_chunk = tb // CHUNK
    s_cur = {(d, g): s_ref[d, g] for d in (0, 1) for g in range(n_groups)}

    def unit_stages(d, g, c):
        sl = slice(c * CHUNK, (c + 1) * CHUNK)
        cs = slice(g * GROUP_W, (g + 1) * GROUP_W)
        r_h, kap_h, k_h, b_h = (R[f'ops{d}'][0, m, 0, sl, cs] for m in range(N_OPERANDS))
        vc = R[f'v{d}'][0, sl, cs]
        p_tot = R[f'ptot{d}'][0, 0, 0, c:c + 1, cs]
        p_tot_b = p_tot.astype(BF16)
        k_t = k_h * p_tot_b
        b_t = b_h * p_tot_b
        mk = tri_masks[d]
        x2 = jnp.concatenate([kap_h, r_h], axis=0)
        gb = _dot_nt(x2, bd(b_h))
        yield
        gk = _dot_nt(x2, bd(k_h))
        yield
        ab = gb[:CHUNK]
        abb = ab.astype(BF16)
        n1 = ab * mk['neg_d16']
        o32 = abb * mk['o32']
        o64 = abb * mk['o64']
        a_rb = (gb[CHUNK:] * mk['incl']).astype(BF16)
        a_kk = jnp.concatenate([gk[:CHUNK] * mk['strict'], gk[CHUNK:] * mk['incl']], axis=0).astype(BF16)
        m1y = _dot(a_kk, bd(vc))
        yield
        n1b = n1.astype(BF16)
        nk = prod(n1b, n1b).astype(BF16)
        x = eye + n1
        yield
        for _ in range(2):
            both = _dot(jnp.concatenate([nk, x.astype(BF16)], axis=0), bd(nk))
            nk = both[:CHUNK].astype(BF16)
            x = x + both[CHUNK:]
            yield
        x = x + prod(x, nk)
        yield
        for off in (o32, o64):
            xb = x.astype(BF16)
            xo = prod(xb, off).astype(BF16)
            yield
            x = x - prod(xo, xb)
            yield
        t16 = x.astype(BF16)
        w = _dot(t16, bd(kap_h)).astype(BF16)
        yield
        ut = _dot(t16, bd(m1y[:CHUNK]))
        yield
        s_in = s_cur[d, g]
        wr = _dot_nt(jnp.concatenate([w, r_h], axis=0), bd(s_in))
        yield
        u = (-(wr[:CHUNK] + ut)).astype(BF16)
        s_cur[d, g] = s_in * p_tot + head_blocks(_dot_tn(jnp.concatenate([u, vc], axis=0),
                                                         jnp.concatenate([b_t, k_t], axis=0)))
        yield
        R[f'y{d}'][0, sl, cs] = wr[CHUNK:] + prod(a_rb, u) + m1y[CHUNK:]

    jobs = [(pos * SCAN_CHUNK_SKEW, unit_stages(d, g, c)) for d in (0, 1) for g in range(n_groups)
            for pos, c in enumerate(range(n_chunk) if d == 0 else range(n_chunk - 1, -1, -1))]
    tick = 0
    while jobs:
        for entry in list(jobs):
            if tick >= entry[0]:
                try:
                    next(entry[1])
                except StopIteration:
                    jobs.remove(entry)
        tick += 1
    for (d, g), s_val in s_cur.items():
        s_ref[d, g] = s_val

    @pl.when(i == nt - 1)
    def _():
        R['sfin'][:, 0] = s_ref[...]


def _scan_call(prep, s0):
    ops, ptot = prep['ops'], prep['ptot']
    _, _, b, t, _ = ops.shape
    tb = min(SCAN_TB, t)
    nt = t // tb
    ng = SCAN_GROUPS
    gw = ng * GROUP_W
    npair = N_GROUPS // ng
    use_s0 = s0 is not None
    tblk = (lambda i: i, lambda i: nt - 1 - i)
    names, in_specs, args = [], [], []

    def add(name, spec, arr):
        names.append(name)
        in_specs.append(spec)
        args.append(arr)

    for d in (0, 1):
        add(f'ops{d}', pl.BlockSpec((1, N_OPERANDS, 1, tb, gw), lambda bi, p, i, d=d: (d, 0, bi, tblk[d](i), p)), ops)
        add(f'ptot{d}', pl.BlockSpec((1, 1, 1, SUBLANES, gw), lambda bi, p, i, d=d: (d, bi, tblk[d](i), 0, p)), ptot)
        add(f'v{d}', pl.BlockSpec((1, tb, gw), lambda bi, p, i, d=d: (bi, tblk[d](i), p)), prep['vop'])
    state_spec = pl.BlockSpec((2, 1, ng, HEAD_DIM, GROUP_W), lambda bi, p, i: (0, bi, p, 0, 0))
    if use_s0:
        add('s0', state_spec, s0)

    out_names, out_shape, out_specs = [], [], []
    for d in (0, 1):
        out_names.append(f'y{d}')
        out_shape.append(jax.ShapeDtypeStruct((b, t, W_A), F32))
        out_specs.append(pl.BlockSpec((1, tb, gw), lambda bi, p, i, d=d: (bi, tblk[d](i), p)))
    out_names.append('sfin')
    out_shape.append(jax.ShapeDtypeStruct((2, b, N_GROUPS, HEAD_DIM, GROUP_W), F32))
    out_specs.append(state_spec)

    outs = pl.pallas_call(
        functools.partial(_scan_kernel, names=tuple(names + out_names + ['s_scr']), tb=tb, n_groups=ng,
                          use_s0=use_s0),
        out_shape=out_shape,
        grid=(b, npair, nt),
        in_specs=in_specs,
        out_specs=out_specs,
        scratch_shapes=[pltpu.VMEM((2, ng, HEAD_DIM, GROUP_W), F32)],
        compiler_params=_cparams(("arbitrary", "arbitrary", "arbitrary")),
    )(*args)
    return dict(zip(out_names, outs))


def _post1_kernel(y0, y1, bo, ga, gb, pc, pp, pn, mg0, mg1, gnw, gnb, pw, ps, woa, wob, bm_ref, o_ref, *, tm, t_total):
    i = pl.program_id(1)
    nt = pl.num_programs(1)
    bm = bm_ref[...]

    parts = []
    for hg in range(N_GROUPS):
        cs = slice(hg * GROUP_W, (hg + 1) * GROUP_W)
        y = y0[0, :, cs] + y1[0, :, cs]
        mu = _headsums([y], bm)[0] * (1.0 / HEAD_DIM)
        yc = y - mu
        var = _headsums([yc * yc], bm)[0] * (1.0 / HEAD_DIM)
        yn = yc * lax.rsqrt(var + GN_EPS) * gnw[:, cs] + gnb[:, cs]
        yn = yn + bo[0, :, cs]
        gate = ga[0, :, cs]
        parts.append((yn * (gate * _sigmoid(gate))).astype(BF16))
    y_a = _dot(jnp.concatenate(parts, axis=1), woa[...])

    pscale = jnp.where(i > 0, 1.0, 0.0).astype(F32)
    nscale = jnp.where(i < nt - 1, 1.0, 0.0).astype(F32)
    n_ext = tm + 2 * POOL_HALO
    tpos = (lax.broadcasted_iota(jnp.int32, (tm, 1), 0) + i * tm).astype(F32)
    parts = []
    for gi, win in enumerate(POOL_WINDOWS):
        left = win // 2
        right = win - 1 - left
        cs = slice(gi * POOL_GROUP_W, (gi + 1) * POOL_GROUP_W)
        p = pc[0, :, cs]
        ext = jnp.concatenate([pp[0, :, cs] * pscale, p, pn[0, :, cs] * nscale], axis=0)
        s = ext
        step = 1
        while step < win:
            s = s + pltpu.roll(s, step, axis=0)
            step *= 2
        if right > 0:
            s = pltpu.roll(s, n_ext - right, axis=0)
        s = s[POOL_HALO:POOL_HALO + tm]
        cnt = jnp.minimum(tpos + (right + 1), float(t_total)) - jnp.maximum(tpos - left, 0.0)
        pooled = s / cnt - p
        mixed = _dot(pooled.astype(BF16), pw[gi]) * ps[:, cs]
        gate = gb[0, :, cs]
        parts.append((mixed * (gate * _sigmoid(gate))).astype(BF16))
    y_b = _dot(jnp.concatenate(parts, axis=1), wob[...])

    merged = _sigmoid(mg0[0]) * y_a + _sigmoid(mg1[0]) * y_b
    o_ref[0] = merged.astype(BF16)


def _post1_call(sc, bonus, u, cols, lp):
    y0, y1 = sc['y0'], sc['y1']
    b, t, _ = y0.shape
    d = lp['w_out_a'].shape[1]
    tm = min(POST1_TM, t)
    nt = t // tm
    hb = tm // POOL_HALO
    nh = t // POOL_HALO
    blk = lambda c: pl.BlockSpec((1, tm, W_A), lambda bi, i: (bi, i, c))
    wide = lambda c: pl.BlockSpec((1, tm, d), lambda bi, i: (bi, i, c))
    const2 = lambda shape: pl.BlockSpec(shape, lambda bi, i: (0, 0))
    pool_col = cols['pool']
    in_specs = [blk(0), blk(0), blk(0),
                blk(cols['gate_a']), blk(cols['gate_b']), blk(pool_col),
                pl.BlockSpec((1, POOL_HALO, W_B), lambda bi, i: (bi, jnp.maximum(i * hb - 1, 0), pool_col)),
                pl.BlockSpec((1, POOL_HALO, W_B), lambda bi, i: (bi, jnp.minimum((i + 1) * hb, nh - 1), pool_col)),
                wide(cols['merge0']), wide(cols['merge1']),
                const2((1, W_A)), const2((1, W_A)),
                pl.BlockSpec((len(POOL_WINDOWS), POOL_GROUP_W, POOL_GROUP_W), lambda bi, i: (0, 0, 0)),
                const2((1, W_B)), const2((W_A, d)), const2((W_B, d)), const2((GROUP_W, GROUP_W))]
    args = [y0, y1, bonus, u, u, u, u, u, u, u,
            lp['gn_w'], lp['gn_b'], lp['pool_w'], lp['pool_scale'], lp['w_out_a'], lp['w_out_b'], lp['head_mask']]
    return pl.pallas_call(
        functools.partial(_post1_kernel, tm=tm, t_total=t),
        out_shape=jax.ShapeDtypeStruct((b, t, d), BF16),
        grid=(b, nt),
        in_specs=in_specs,
        out_specs=pl.BlockSpec((1, tm, d), lambda bi, i: (bi, i, 0)),
        compiler_params=_cparams(("arbitrary", "arbitrary")),
    )(*args)


def _post2_kernel(m_ref, h_ref, gate_ref, w_ref, fg_ref, o_ref, *, final_norm):
    out = _dot(m_ref[0], w_ref[...])
    h = h_ref[0] + gate_ref[0] * out
    if final_norm:
        ms = jnp.mean(h * h, axis=-1, keepdims=True)
        h = h * lax.rsqrt(ms + NORM_EPS) * fg_ref[...]
    o_ref[0] = h


def _post2_call(merged, h, gate, w_out, final_g, final_norm):
    bm, t, d = h.shape
    tm = min(POST2_TM, t)
    return pl.pallas_call(
        functools.partial(_post2_kernel, final_norm=final_norm),
        out_shape=jax.ShapeDtypeStruct((bm, t, d), F32),
        grid=(bm, t // tm),
        in_specs=[
            pl.BlockSpec((1, tm, d), lambda b, m: (b, m, 0)),
            pl.BlockSpec((1, tm, d), lambda b, m: (b, m, 0)),
            pl.BlockSpec((1, 1, d), lambda b, m: (b, 0, 0)),
            pl.BlockSpec((d, d), lambda b, m: (0, 0)),
            pl.BlockSpec((1, d), lambda b, m: (0, 0)),
        ],
        out_specs=pl.BlockSpec((1, tm, d), lambda b, m: (b, m, 0)),
        compiler_params=_cparams(("arbitrary", "arbitrary")),
    )(merged, h, gate, w_out, final_g)


def _pad_rows(w, rows):
    return jnp.pad(w, ((0, 0),) * (w.ndim - 2) + ((0, rows - w.shape[-2]), (0, 0)))


def kernel(x, c, ctx, c_ctx, ada_w, ada_b, norm_g, w_in, tok_mu, decay_w0, decay_lora_b, iclr_a0, iclr_lora_b, key_k, key_a, bonus_rk, gn_w, gn_b, vres_v0, vres_lora_a, vres_lora_b, pool_w, pool_scale, w_out_a, w_out_b, w_out, final_g):
    depth = w_in.shape[0]
    batch, seq, d = x.shape
    ctx_len = ctx.shape[1]
    assert seq % GRID_W == 0 and seq % CHUNK == 0 and ctx_len % CHUNK == 0
    assert d % LANES == 0

    off_merge = 3 * W_A + W_B + W_A + W_B
    off_dl = off_merge + 2 * d
    off_vres = off_dl + 2 * R_DECAY + 2 * R_ICLR
    nu = -(-(off_vres + LANES) // INPROJ_TN) * INPROJ_TN
    cols = dict(pool=REF_OFF_POOL // W_B, gate_a=REF_OFF_GATE_A // W_A, gate_b=REF_OFF_GATE_B // W_A,
                merge0=off_merge // d, merge1=off_merge // d + 1)
    assert off_merge % d == 0 and off_dl % GROUP_W == 0 and off_vres % LANES == 0

    head_mask = (jnp.arange(GROUP_W)[:, None] // HEAD_DIM == jnp.arange(GROUP_W)[None, :] // HEAD_DIM).astype(BF16)

    rows = -(-(batch + 1) // SUBLANES) * SUBLANES
    c_all = jnp.zeros((rows, d), F32).at[:batch].set(c).at[batch].set(c_ctx)
    mod = _ada_call(c_all, ada_w, ada_b)

    h_lat, h_ctx = x, ctx
    vfirst = {}
    tail = jnp.zeros((depth, d, nu - off_vres), F32)
    if depth > 1:
        tail = tail.at[1:, :, :R_VRES].set(vres_lora_a)
    w_perm = _wcast_call(w_in, tail, 2 * d)
    for l in range(depth):
        last = l == depth - 1

        dlw = decay_lora_b[l]
        dlw = jnp.stack([jnp.pad(dlw[0], ((0, LANES - R_DECAY), (0, 0))),
                         jnp.pad(dlw[1], ((R_DECAY, LANES - 2 * R_DECAY), (0, 0)))]).astype(BF16)
        ilw = iclr_lora_b[l]
        ilw = jnp.stack([jnp.pad(ilw[0], ((0, LANES - R_ICLR), (0, 0))),
                         jnp.pad(ilw[1], ((R_ICLR, LANES - 2 * R_ICLR), (0, 0)))]).astype(BF16)
        lp = dict(tok_mu=tok_mu[l], decay_w0=decay_w0[l], decay_lora=dlw, iclr_a0=iclr_a0[l], iclr_lora=ilw,
                  key_k=key_k[l].reshape(1, W_A), key_a=key_a[l], bonus_rk=bonus_rk[l].reshape(2, W_A),
                  head_mask=head_mask, gn_w=gn_w[l].reshape(1, W_A), gn_b=gn_b[l].reshape(1, W_A),
                  pool_w=pool_w[l].astype(BF16), pool_scale=pool_scale[l].reshape(1, W_B),
                  w_out_a=w_out_a[l].astype(BF16), w_out_b=w_out_b[l].astype(BF16))
        w_out_l = w_out[l].astype(BF16)
        g_l = norm_g[l].reshape(1, d)

        m = mod[l]
        shift_l, scale_l, gate_l = (m[:batch, j * d:(j + 1) * d].reshape(batch, 1, d) for j in range(3))
        shift_c, scale_c, gate_c = (m[batch:batch + 1, j * d:(j + 1) * d].reshape(1, 1, d) for j in range(3))

        u_c = _inproj_call(h_ctx.reshape(1, batch * ctx_len, d), shift_c, scale_c, g_l, w_perm, l)
        u_c = u_c.reshape(batch, ctx_len, nu)
        u_l = _inproj_call(h_lat, shift_l, scale_l, g_l, w_perm, l)

        vres_c = vres_l = None
        if l > 0:
            vb = _pad_rows(vres_lora_b[l - 1], LANES).astype(BF16)
            v0 = vres_v0[l - 1].reshape(1, W_A)
            vres_c = (vfirst['c'], off_vres // LANES, vb, v0)
            vres_l = (vfirst['l'], off_vres // LANES, vb, v0)
        keep_v = l == 0 and depth > 1
        prep_c = _prep_call(u_c, off_dl // GROUP_W, lp, False, not last, vres_c, keep_v)
        prep_l = _prep_call(u_l, off_dl // GROUP_W, lp, True, True, vres_l, keep_v)
        if keep_v:
            vfirst = dict(c=prep_c['vmix'], l=prep_l['vmix'])
        sc_c = _scan_call(prep_c, None)
        sc_l = _scan_call(prep_l, sc_c['sfin'])

        merged_l = _post1_call(sc_l, prep_l['bonus'], u_l, cols, lp)
        h_lat = _post2_call(merged_l, h_lat, gate_l, w_out_l, final_g.reshape(1, d), last)
        if not last:
            merged_c = _post1_call(sc_c, prep_c['bonus'], u_c, cols, lp)
            h_ctx = _post2_call(merged_c.reshape(1, batch * ctx_len, d), h_ctx.reshape(1, batch * ctx_len, d),
                                gate_c, w_out_l, final_g.reshape(1, d), False).reshape(batch, ctx_len, d)
    return h_lat
```

```python
import functools
import math

import jax
import jax.numpy as jnp
from jax import lax
from jax.experimental import pallas as pl
from jax.experimental.pallas import tpu as pltpu

F32 = jnp.float32
BF16 = jnp.bfloat16

GRID_W = 64
W_A = 1024
W_B = 1024
HEAD_DIM = 64
POOL_WINDOWS = (2, 4, 8, 16)
POOL_GROUP_W = W_B // len(POOL_WINDOWS)
R_DECAY = 64
R_ICLR = 64
R_VRES = 32
NORM_EPS = 1e-6
GN_EPS = HEAD_DIM * 1e-5
REF_OFF_POOL = 3 * W_A
REF_OFF_GATE_A = REF_OFF_POOL + W_B
REF_OFF_GATE_B = REF_OFF_GATE_A + W_A
REF_OFF_DECAY = REF_OFF_GATE_B + W_B
REF_OFF_ICLR = REF_OFF_DECAY + 2 * R_DECAY
REF_OFF_MERGE = REF_OFF_ICLR + 2 * R_ICLR

LANES = 128
SUBLANES = 8
MXU_DIM = 256
VMEM_LIMIT_BYTES = 56 * 1024 * 1024

GROUP_W = MXU_DIM
GROUP_HEADS = GROUP_W // HEAD_DIM
N_GROUPS = W_A // GROUP_W
SCAN_GROUPS = 4
SCAN_TB = 256
CHUNK = 64
SCAN_CHUNK_SKEW = 2
N_OPERANDS = 4
HALO = 64
POOL_HALO = 8
INPROJ_TM = 1024
INPROJ_TN = 1536
ADA_TN = 768
WCAST_ROWS = 256
POST1_TM = 256
POST2_TM = 512
INPROJ_NORM_ROWS = 16
LOG_DECAY_SCALE = math.exp(-0.5)


def _cparams(sem):
    return pltpu.CompilerParams(dimension_semantics=sem, vmem_limit_bytes=VMEM_LIMIT_BYTES)


def _sigmoid(x):
    return 1.0 / (1.0 + jnp.exp(-x))


def _dot(a, b):
    return jnp.dot(a, b, preferred_element_type=F32)


def _dot_nt(a, b):
    return lax.dot_general(a, b, (((1,), (1,)), ((), ())), preferred_element_type=F32)


def _dot_tn(a, b):
    return lax.dot_general(a, b, (((0,), (0,)), ((), ())), preferred_element_type=F32)


def _headsums(xs, bm):
    rows = xs[0].shape[0]
    s = _dot(jnp.concatenate([x.astype(BF16) for x in xs], axis=0), bm)
    return [s[n * rows:(n + 1) * rows] for n in range(len(xs))]


def _ada_kernel(c_ref, w_ref, b_ref, o_ref):
    c = c_ref[...]
    s = (c * _sigmoid(c)).astype(BF16)
    o_ref[0] = _dot(s, w_ref[0].astype(BF16)) + b_ref[0]


def _ada_call(c_all, ada_w, ada_b):
    n_layers, d, d3 = ada_w.shape
    rows = c_all.shape[0]
    tn = ADA_TN
    assert d3 % tn == 0
    return pl.pallas_call(
        _ada_kernel,
        out_shape=jax.ShapeDtypeStruct((n_layers, rows, d3), F32),
        grid=(n_layers, d3 // tn),
        in_specs=[
            pl.BlockSpec((rows, d), lambda l, n: (0, 0)),
            pl.BlockSpec((1, d, tn), lambda l, n: (l, 0, n)),
            pl.BlockSpec((1, 1, tn), lambda l, n: (l, 0, n)),
        ],
        out_specs=pl.BlockSpec((1, rows, tn), lambda l, n: (l, 0, n)),
        compiler_params=_cparams(("arbitrary", "arbitrary")),
    )(c_all, ada_w, ada_b.reshape(n_layers, 1, d3))


def _wcast_kernel(w_ref, t_ref, o_ref, *, merge_w):
    width = w_ref.shape[2]
    o_ref[0, :, :REF_OFF_DECAY] = w_ref[0, :, :REF_OFF_DECAY].astype(BF16)
    o_ref[0, :, REF_OFF_DECAY:REF_OFF_DECAY + merge_w] = w_ref[0, :, REF_OFF_MERGE:REF_OFF_MERGE + merge_w].astype(BF16)
    o_ref[0, :, REF_OFF_DECAY + merge_w:width] = w_ref[0, :, REF_OFF_DECAY:REF_OFF_MERGE].astype(BF16)
    o_ref[0, :, width:] = t_ref[0].astype(BF16)


def _wcast_call(w_in, tail, merge_w):
    n_layers, d, width = w_in.shape
    extra = tail.shape[2]
    tr = WCAST_ROWS
    assert REF_OFF_MERGE + merge_w == width and d % tr == 0
    assert all(x % LANES == 0 for x in (REF_OFF_DECAY, REF_OFF_MERGE, merge_w, width, extra))
    return pl.pallas_call(
        functools.partial(_wcast_kernel, merge_w=merge_w),
        out_shape=jax.ShapeDtypeStruct((n_layers, d, width + extra), BF16),
        grid=(n_layers, d // tr),
        in_specs=[pl.BlockSpec((1, tr, width), lambda l, j: (l, j, 0)),
                  pl.BlockSpec((1, tr, extra), lambda l, j: (l, j, 0))],
        out_specs=pl.BlockSpec((1, tr, width + extra), lambda l, j: (l, j, 0)),
        compiler_params=_cparams(("arbitrary", "arbitrary")),
    )(w_in, tail)


def _inproj_kernel(h_ref, shift_ref, scale_ref, g_ref, w_ref, o_ref, xn_ref):
    @pl.when(pl.program_id(2) == 0)
    def _():
        def rows(r, carry):
            sl = pl.ds(pl.multiple_of(r * INPROJ_NORM_ROWS, INPROJ_NORM_ROWS), INPROJ_NORM_ROWS)
            x = h_ref[0, sl, :]
            ms = jnp.mean(x * x, axis=-1, keepdims=True)
            y = x * lax.rsqrt(ms + NORM_EPS) * g_ref[...]
            xn_ref[sl, :] = (y * (1.0 + scale_ref[0]) + shift_ref[0]).astype(BF16)
            return carry

        lax.fori_loop(0, h_ref.shape[1] // INPROJ_NORM_ROWS, rows, 0, unroll=16)

    o_ref[0] = _dot(xn_ref[...], w_ref[0])


def _inproj_call(h, shift, scale, g, w, layer):
    bm, t, d = h.shape
    nu = w.shape[2]
    tm = min(INPROJ_TM, t)
    tn = INPROJ_TN
    assert nu % tn == 0
    return pl.pallas_call(
        _inproj_kernel,
        out_shape=jax.ShapeDtypeStruct((bm, t, nu), F32),
        grid=(bm, t // tm, nu // tn),
        in_specs=[
            pl.BlockSpec((1, tm, d), lambda b, m, n: (b, m, 0)),
            pl.BlockSpec((1, 1, d), lambda b, m, n: (b, 0, 0)),
            pl.BlockSpec((1, 1, d), lambda b, m, n: (b, 0, 0)),
            pl.BlockSpec((1, d), lambda b, m, n: (0, 0)),
            pl.BlockSpec((1, d, tn), lambda b, m, n: (layer, 0, n)),
        ],
        out_specs=pl.BlockSpec((1, tm, tn), lambda b, m, n: (b, m, n)),
        scratch_shapes=[pltpu.VMEM((tm, d), BF16)],
        compiler_params=_cparams(("arbitrary", "arbitrary", "arbitrary")),
    )(h, shift, scale, g, w)


def _shift_plan(grid_mode, hg):
    if grid_mode:
        return ((-1, 'first'), (1, 'last'), (-GRID_W, None), (GRID_W, None))[hg]
    return ((-1, None), (-1, None), (1, None), (1, None))[hg]


def _prep_kernel(*refs, names, tb, grid_mode, has_vres, emit_out, emit_v):
    R = dict(zip(names, refs))
    i = pl.program_id(1)
    nt = pl.num_programs(1)
    ps = jnp.where(i > 0, 1.0, 0.0).astype(F32)
    ns = jnp.where(i < nt - 1, 1.0, 0.0).astype(F32)
    n_ext = tb + 2 * HALO
    trow1 = lax.broadcasted_iota(jnp.int32, (tb, 1), 0)
    gcol = jnp.bitwise_and(trow1, GRID_W - 1)
    col_mask = dict(first=jnp.where(gcol != 0, 1.0, 0.0).astype(F32),
                    last=jnp.where(gcol != GRID_W - 1, 1.0, 0.0).astype(F32))
    bm = R['head_mask'][...]
    brow = lax.broadcasted_iota(jnp.int32, (tb, tb), 0)
    bcol = lax.broadcasted_iota(jnp.int32, (tb, tb), 1)
    same_chunk = (brow >> 6) == (bcol >> 6)
    tris = (jnp.where(jnp.logical_and(same_chunk, bcol <= brow), 1.0, 0.0).astype(BF16),
            jnp.where(jnp.logical_and(same_chunk, bcol >= brow), 1.0, 0.0).astype(BF16))
    n_chunk = tb // CHUNK
    R['ptot'][...] = jnp.zeros_like(R['ptot'])

    def mixed(hg, m):
        off, cmask = _shift_plan(grid_mode, hg)
        cs = slice(m * W_A + hg * GROUP_W, m * W_A + (hg + 1) * GROUP_W)
        x = R['cur'][0, :, cs]
        if off == -GRID_W:
            sh = jnp.concatenate([R['prv'][0, :, cs] * ps, x[:tb - HALO]], axis=0)
        elif off == GRID_W:
            sh = jnp.concatenate([x[HALO:], R['nxt'][0, :, cs] * ns], axis=0)
        else:
            ext = jnp.concatenate([R['prv'][0, :, cs] * ps, x, R['nxt'][0, :, cs] * ns], axis=0)
            sh = pltpu.roll(ext, 1 if off == -1 else n_ext - 1, axis=0)[HALO:HALO + tb]
        if cmask is not None:
            sh = sh * col_mask[cmask]
        return x + R['tok_mu'][m:m + 1, hg * GROUP_W:(hg + 1) * GROUP_W] * (sh - x)

    dl = R['dl'][0]
    dlo = jnp.tanh(dl[:, :LANES]).astype(BF16)
    alo = dl[:, LANES:].astype(BF16)
    for hg in range(N_GROUPS):
        cs = slice(hg * GROUP_W, (hg + 1) * GROUP_W)
        r = mixed(hg, 0)
        k = mixed(hg, 1)
        v = mixed(hg, 2)
        if has_vres:
            gate = _sigmoid(R['vres_v0'][:, cs] + _dot(R['lo'][0].astype(BF16), R['vres_b'][:, cs]))
            v = v + (R['vf'][0, :, cs] - v) * gate
        if emit_v:
            R['vmix'][0, :, cs] = v
        kk = k * R['key_k'][:, cs]
        a_d, kd_d, sums_in = [], [], [kk * kk]
        for d in (0, 1):
            a = _sigmoid(R['iclr_a0'][d:d + 1, cs] + _dot(alo, R['iclr_lora'][d, :, cs]))
            ka = R['key_a'][d:d + 1, cs]
            kd = k * ((1.0 - ka) + a * ka)
            a_d.append(a)
            kd_d.append(kd)
            if emit_out:
                sums_in.append(r * kd * R['bonus_rk'][d:d + 1, cs])
        sums = _headsums(sums_in, bm)
        kk = kk / jnp.maximum(jnp.sqrt(sums[0]), 1e-12)
        if emit_out:
            R['bonus'][0, :, cs] = (sums[1] + sums[2]) * v
        R['vop'][0, :, cs] = v.astype(BF16)
        for d in (0, 1):
            z = R['decay_w0'][d:d + 1, cs] + _dot(dlo, R['decay_lora'][d, :, cs])
            lg = -LOG_DECAY_SCALE * _sigmoid(z)
            g1 = lg.astype(BF16)
            g2 = (lg - g1.astype(F32)).astype(BF16)
            cum = _dot(tris[d], g1) + _dot(tris[d], g2)
            bb = kk * a_d[d]
            kd = kd_d[d]
            for c in range(n_chunk):
                sl = slice(c * CHUNK, (c + 1) * CHUNK)
                lc = cum[sl]
                ltot = lc[CHUNK - 1:CHUNK] if d == 0 else lc[0:1]
                e_pos = jnp.exp(lc)
                e_neg = 1.0 / e_pos
                R['ptot'][d, 0, 0, c:c + 1, cs] = jnp.exp(ltot)
                operands = (r[sl] * e_pos, kk[sl] * jnp.exp(lc - lg[sl]), kd[sl] * e_neg, bb[sl] * e_neg)
                for m in range(N_OPERANDS):
                    R['ops'][d, m, 0, sl, cs] = operands[m].astype(BF16)


def _prep_call(u, dl_col, lp, grid_mode, emit_out, vres=None, emit_v=False):
    b, t, _ = u.shape
    tb = min(SCAN_TB, t)
    nt = t // tb
    hb = tb // HALO
    nh = t // HALO
    w3 = 3 * W_A
    has_vres = vres is not None
    assert GROUP_W == W_A // 4 and tb % HALO == 0 and tb % CHUNK == 0 and tb // CHUNK <= SUBLANES
    names, in_specs, args = [], [], []

    def add(name, spec, arr):
        names.append(name)
        in_specs.append(spec)
        args.append(arr)

    add('cur', pl.BlockSpec((1, tb, w3), lambda bi, i: (bi, i, 0)), u)
    add('prv', pl.BlockSpec((1, HALO, w3), lambda bi, i: (bi, jnp.maximum(i * hb - 1, 0), 0)), u)
    add('nxt', pl.BlockSpec((1, HALO, w3), lambda bi, i: (bi, jnp.minimum((i + 1) * hb, nh - 1), 0)), u)
    add('dl', pl.BlockSpec((1, tb, GROUP_W), lambda bi, i: (bi, i, dl_col)), u)
    if has_vres:
        v_first, vres_col, vres_b, vres_v0 = vres
        add('vf', pl.BlockSpec((1, tb, W_A), lambda bi, i: (bi, i, 0)), v_first)
        add('lo', pl.BlockSpec((1, tb, LANES), lambda bi, i: (bi, i, vres_col)), u)
        add('vres_b', pl.BlockSpec((LANES, W_A), lambda bi, i: (0, 0)), vres_b)
        add('vres_v0', pl.BlockSpec((1, W_A), lambda bi, i: (0, 0)), vres_v0)
    vec = lambda rows: pl.BlockSpec((rows, W_A), lambda bi, i: (0, 0))
    lora = pl.BlockSpec((2, LANES, W_A), lambda bi, i: (0, 0, 0))
    add('tok_mu', vec(3), lp['tok_mu'])
    add('decay_w0', vec(2), lp['decay_w0'])
    add('decay_lora', lora, lp['decay_lora'])
    add('iclr_a0', vec(2), lp['iclr_a0'])
    add('iclr_lora', lora, lp['iclr_lora'])
    add('key_k', vec(1), lp['key_k'])
    add('key_a', vec(2), lp['key_a'])
    add('bonus_rk', vec(2), lp['bonus_rk'])
    add('head_mask', pl.BlockSpec((GROUP_W, GROUP_W), lambda bi, i: (0, 0)), lp['head_mask'])

    out_names = ['ops', 'ptot', 'vop']
    out_shape = [jax.ShapeDtypeStruct((2, N_OPERANDS, b, t, W_A), BF16),
                 jax.ShapeDtypeStruct((2, b, nt, SUBLANES, W_A), F32),
                 jax.ShapeDtypeStruct((b, t, W_A), BF16)]
    out_specs = [pl.BlockSpec((2, N_OPERANDS, 1, tb, W_A), lambda bi, i: (0, 0, bi, i, 0)),
                 pl.BlockSpec((2, 1, 1, SUBLANES, W_A), lambda bi, i: (0, bi, i, 0, 0)),
                 pl.BlockSpec((1, tb, W_A), lambda bi, i: (bi, i, 0))]
    for flag, nm in ((emit_out, 'bonus'), (emit_v, 'vmix')):
        if flag:
            out_names.append(nm)
            out_shape.append(jax.ShapeDtypeStruct((b, t, W_A), F32))
            out_specs.append(pl.BlockSpec((1, tb, W_A), lambda bi, i: (bi, i, 0)))

    outs = pl.pallas_call(
        functools.partial(_prep_kernel, names=tuple(names + out_names), tb=tb, grid_mode=grid_mode,
                          has_vres=has_vres, emit_out=emit_out, emit_v=emit_v),
        out_shape=out_shape,
        grid=(b, nt),
        in_specs=in_specs,
        out_specs=out_specs,
        compiler_params=_cparams(("arbitrary", "arbitrary")),
    )(*args)
    return dict(zip(out_names, outs))


def _scan_kernel(*refs, names, tb, n_groups, use_s0):
    R = dict(zip(names, refs))
    s_ref = R['s_scr']
    i = pl.program_id(2)
    nt = pl.num_programs(2)

    @pl.when(i == 0)
    def _():
        if use_s0:
            s_ref[...] = R['s0'][:, 0]
        else:
            s_ref[...] = jnp.zeros_like(s_ref)

    lane1 = lax.broadcasted_iota(jnp.int32, (1, LANES), 1)
    half_f = (jnp.where(lane1 < HEAD_DIM, 1.0, 0.0).astype(F32), jnp.where(lane1 >= HEAD_DIM, 1.0, 0.0).astype(F32))
    half_b = (half_f[0].astype(BF16), half_f[1].astype(BF16))
    zero_tile = jnp.zeros((HEAD_DIM, LANES), BF16)
    n_tiles = GROUP_W // LANES

    def bd(x):
        xb = x.astype(BF16)
        rows = []
        for j in range(GROUP_HEADS):
            t = (j * HEAD_DIM) // LANES
            piece = xb[:, t * LANES:(t + 1) * LANES] * half_b[(j * HEAD_DIM % LANES) // HEAD_DIM]
            rows.append(jnp.concatenate([piece if tt == t else zero_tile for tt in range(n_tiles)], axis=1))
        return jnp.concatenate(rows, axis=0)

    def head_blocks(full):
        tiles = []
        for t in range(n_tiles):
            acc = None
            for j in range(GROUP_HEADS):
                if (j * HEAD_DIM) // LANES == t:
                    part = (full[j * HEAD_DIM:(j + 1) * HEAD_DIM, t * LANES:(t + 1) * LANES]
                            * half_f[(j * HEAD_DIM % LANES) // HEAD_DIM])
                    acc = part if acc is None else acc + part
            tiles.append(acc)
        return jnp.concatenate(tiles, axis=1)

    def prod(a, b):
        return _dot(a.astype(BF16), bd(b))

    trow = lax.broadcasted_iota(jnp.int32, (CHUNK, GROUP_W), 0)
    scol = jnp.bitwise_and(lax.broadcasted_iota(jnp.int32, (CHUNK, GROUP_W), 1), CHUNK - 1)
    eye = jnp.where(trow == scol, 1.0, 0.0).astype(F32)
    same16 = (trow >> 4) == (scol >> 4)
    same32 = (trow >> 5) == (scol >> 5)
    tri_masks = []
    for before, upto in ((scol < trow, scol <= trow), (scol > trow, scol >= trow)):
        tri_masks.append(dict(
            neg_d16=jnp.where(jnp.logical_and(before, same16), -1.0, 0.0).astype(F32),
            o32=jnp.where(jnp.logical_and(before, jnp.logical_and(same32, jnp.logical_not(same16))), 1.0, 0.0).astype(BF16),
            o64=jnp.where(jnp.logical_and(before, jnp.logical_not(same32)), 1.0, 0.0).astype(BF16),
            strict=jnp.where(before, 1.0, 0.0).astype(F32),
            incl=jnp.where(upto, 1.0, 0.0).astype(F32)))

    n_chunk = tb // CHUNK
    s_cur = {(d, g): s_ref[d, g] for d in (0, 1) for g in range(n_groups)}

    def unit_stages(d, g, c):
        sl = slice(c * CHUNK, (c + 1) * CHUNK)
        cs = slice(g * GROUP_W, (g + 1) * GROUP_W)
        r_h, kap_h, k_h, b_h = (R[f'ops{d}'][0, m, 0, sl, cs] for m in range(N_OPERANDS))
        vc = R[f'v{d}'][0, sl, cs]
        p_tot = R[f'ptot{d}'][0, 0, 0, c:c + 1, cs]
        p_tot_b = p_tot.astype(BF16)
        k_t = k_h * p_tot_b
        b_t = b_h * p_tot_b
        mk = tri_masks[d]
        x2 = jnp.concatenate([kap_h, r_h], axis=0)
        gb = _dot_nt(x2, bd(b_h))
        yield
        gk = _dot_nt(x2, bd(k_h))
        yield
        ab = gb[:CHUNK]
        abb = ab.astype(BF16)
        n1 = ab * mk['neg_d16']
        o32 = abb * mk['o32']
        o64 = abb * mk['o64']
        a_rb = (gb[CHUNK:] * mk['incl']).astype(BF16)
        a_kk = jnp.concatenate([gk[:CHUNK] * mk['strict'], gk[CHUNK:] * mk['incl']], axis=0).astype(BF16)
        m1y = _dot(a_kk, bd(vc))
        yield
        n1b = n1.astype(BF16)
        nk = prod(n1b, n1b).astype(BF16)
        x = eye + n1
        yield
        for _ in range(2):
            both = _dot(jnp.concatenate([nk, x.astype(BF16)], axis=0), bd(nk))
            nk = both[:CHUNK].astype(BF16)
            x = x + both[CHUNK:]
            yield
        x = x + prod(x, nk)
        yield
        for off in (o32, o64):
            xb = x.astype(BF16)
            xo = prod(xb, off).astype(BF16)
            yield
            x = x - prod(xo, xb)
            yield
        t16 = x.astype(BF16)
        w = _dot(t16, bd(kap_h)).astype(BF16)
        yield
        ut = _dot(t16, bd(m1y[:CHUNK]))
        yield
        s_in = s_cur[d, g]
        wr = _dot_nt(jnp.concatenate([w, r_h], axis=0), bd(s_in))
        yield
        u = (-(wr[:CHUNK] + ut)).astype(BF16)
        s_cur[d, g] = s_in * p_tot + head_blocks(_dot_tn(jnp.concatenate([u, vc], axis=0),
                                                         jnp.concatenate([b_t, k_t], axis=0)))
        yield
        R[f'y{d}'][0, sl, cs] = wr[CHUNK:] + prod(a_rb, u) + m1y[CHUNK:]

    jobs = [(pos * SCAN_CHUNK_SKEW, unit_stages(d, g, c)) for d in (0, 1) for g in range(n_groups)
            for pos, c in enumerate(range(n_chunk) if d == 0 else range(n_chunk - 1, -1, -1))]
    tick = 0
    while jobs:
        for entry in list(jobs):
            if tick >= entry[0]:
                try:
                    next(entry[1])
                except StopIteration:
                    jobs.remove(entry)
        tick += 1
    for (d, g), s_val in s_cur.items():
        s_ref[d, g] = s_val

    @pl.when(i == nt - 1)
    def _():
        R['sfin'][:, 0] = s_ref[...]


def _scan_call(prep, s0):
    ops, ptot = prep['ops'], prep['ptot']
    _, _, b, t, _ = ops.shape
    tb = min(SCAN_TB, t)
    nt = t // tb
    ng = SCAN_GROUPS
    gw = ng * GROUP_W
    npair = N_GROUPS // ng
    use_s0 = s0 is not None
    tblk = (lambda i: i, lambda i: nt - 1 - i)
    names, in_specs, args = [], [], []

    def add(name, spec, arr):
        names.append(name)
        in_specs.append(spec)
        args.append(arr)

    for d in (0, 1):
        add(f'ops{d}', pl.BlockSpec((1, N_OPERANDS, 1, tb, gw), lambda bi, p, i, d=d: (d, 0, bi, tblk[d](i), p)), ops)
        add(f'ptot{d}', pl.BlockSpec((1, 1, 1, SUBLANES, gw), lambda bi, p, i, d=d: (d, bi, tblk[d](i), 0, p)), ptot)
        add(f'v{d}', pl.BlockSpec((1, tb, gw), lambda bi, p, i, d=d: (bi, tblk[d](i), p)), prep['vop'])
    state_spec = pl.BlockSpec((2, 1, ng, HEAD_DIM, GROUP_W), lambda bi, p, i: (0, bi, p, 0, 0))
    if use_s0:
        add('s0', state_spec, s0)

    out_names, out_shape, out_specs = [], [], []
    for d in (0, 1):
        out_names.append(f'y{d}')
        out_shape.append(jax.ShapeDtypeStruct((b, t, W_A), F32))
        out_specs.append(pl.BlockSpec((1, tb, gw), lambda bi, p, i, d=d: (bi, tblk[d](i), p)))
    out_names.append('sfin')
    out_shape.append(jax.ShapeDtypeStruct((2, b, N_GROUPS, HEAD_DIM, GROUP_W), F32))
    out_specs.append(state_spec)

    outs = pl.pallas_call(
        functools.partial(_scan_kernel, names=tuple(names + out_names + ['s_scr']), tb=tb, n_groups=ng,
                          use_s0=use_s0),
        out_shape=out_shape,
        grid=(b, npair, nt),
        in_specs=in_specs,
        out_specs=out_specs,
        scratch_shapes=[pltpu.VMEM((2, ng, HEAD_DIM, GROUP_W), F32)],
        compiler_params=_cparams(("arbitrary", "arbitrary", "arbitrary")),
    )(*args)
    return dict(zip(out_names, outs))


def _post1_kernel(y0, y1, bo, ga, gb, pc, pp, pn, mg0, mg1, gnw, gnb, pw, ps, woa, wob, bm_ref, o_ref, *, tm, t_total):
    i = pl.program_id(1)
    nt = pl.num_programs(1)
    bm = bm_ref[...]

    parts = []
    for hg in range(N_GROUPS):
        cs = slice(hg * GROUP_W, (hg + 1) * GROUP_W)
        y = y0[0, :, cs] + y1[0, :, cs]
        mu = _headsums([y], bm)[0] * (1.0 / HEAD_DIM)
        yc = y - mu
        var = _headsums([yc * yc], bm)[0] * (1.0 / HEAD_DIM)
        yn = yc * lax.rsqrt(var + GN_EPS) * gnw[:, cs] + gnb[:, cs]
        yn = yn + bo[0, :, cs]
        gate = ga[0, :, cs]
        parts.append((yn * (gate * _sigmoid(gate))).astype(BF16))
    y_a = _dot(jnp.concatenate(parts, axis=1), woa[...])

    pscale = jnp.where(i > 0, 1.0, 0.0).astype(F32)
    nscale = jnp.where(i < nt - 1, 1.0, 0.0).astype(F32)
    n_ext = tm + 2 * POOL_HALO
    tpos = (lax.broadcasted_iota(jnp.int32, (tm, 1), 0) + i * tm).astype(F32)
    parts = []
    for gi, win in enumerate(POOL_WINDOWS):
        left = win // 2
        right = win - 1 - left
        cs = slice(gi * POOL_GROUP_W, (gi + 1) * POOL_GROUP_W)
        p = pc[0, :, cs]
        ext = jnp.concatenate([pp[0, :, cs] * pscale, p, pn[0, :, cs] * nscale], axis=0)
        s = ext
        step = 1
        while step < win:
            s = s + pltpu.roll(s, step, axis=0)
            step *= 2
        if right > 0:
            s = pltpu.roll(s, n_ext - right, axis=0)
        s = s[POOL_HALO:POOL_HALO + tm]
        cnt = jnp.minimum(tpos + (right + 1), float(t_total)) - jnp.maximum(tpos - left, 0.0)
        pooled = s / cnt - p
        mixed = _dot(pooled.astype(BF16), pw[gi]) * ps[:, cs]
        gate = gb[0, :, cs]
        parts.append((mixed * (gate * _sigmoid(gate))).astype(BF16))
    y_b = _dot(jnp.concatenate(parts, axis=1), wob[...])

    merged = _sigmoid(mg0[0]) * y_a + _sigmoid(mg1[0]) * y_b
    o_ref[0] = merged.astype(BF16)


def _post1_call(sc, bonus, u, cols, lp):
    y0, y1 = sc['y0'], sc['y1']
    b, t, _ = y0.shape
    d = lp['w_out_a'].shape[1]
    tm = min(POST1_TM, t)
    nt = t // tm
    hb = tm // POOL_HALO
    nh = t // POOL_HALO
    blk = lambda c: pl.BlockSpec((1, tm, W_A), lambda bi, i: (bi, i, c))
    wide = lambda c: pl.BlockSpec((1, tm, d), lambda bi, i: (bi, i, c))
    const2 = lambda shape: pl.BlockSpec(shape, lambda bi, i: (0, 0))
    pool_col = cols['pool']
    in_specs = [blk(0), blk(0), blk(0),
                blk(cols['gate_a']), blk(cols['gate_b']), blk(pool_col),
                pl.BlockSpec((1, POOL_HALO, W_B), lambda bi, i: (bi, jnp.maximum(i * hb - 1, 0), pool_col)),
                pl.BlockSpec((1, POOL_HALO, W_B), lambda bi, i: (bi, jnp.minimum((i + 1) * hb, nh - 1), pool_col)),
                wide(cols['merge0']), wide(cols['merge1']),
                const2((1, W_A)), const2((1, W_A)),
                pl.BlockSpec((len(POOL_WINDOWS), POOL_GROUP_W, POOL_GROUP_W), lambda bi, i: (0, 0, 0)),
                const2((1, W_B)), const2((W_A, d)), const2((W_B, d)), const2((GROUP_W, GROUP_W))]
    args = [y0, y1, bonus, u, u, u, u, u, u, u,
            lp['gn_w'], lp['gn_b'], lp['pool_w'], lp['pool_scale'], lp['w_out_a'], lp['w_out_b'], lp['head_mask']]
    return pl.pallas_call(
        functools.partial(_post1_kernel, tm=tm, t_total=t),
        out_shape=jax.ShapeDtypeStruct((b, t, d), BF16),
        grid=(b, nt),
        in_specs=in_specs,
        out_specs=pl.BlockSpec((1, tm, d), lambda bi, i: (bi, i, 0)),
        compiler_params=_cparams(("arbitrary", "arbitrary")),
    )(*args)


def _post2_kernel(m_ref, h_ref, gate_ref, w_ref, fg_ref, o_ref, *, final_norm):
    out = _dot(m_ref[0], w_ref[...])
    h = h_ref[0] + gate_ref[0] * out
    if final_norm:
        ms = jnp.mean(h * h, axis=-1, keepdims=True)
        h = h * lax.rsqrt(ms + NORM_EPS) * fg_ref[...]
    o_ref[0] = h


def _post2_call(merged, h, gate, w_out, final_g, final_norm):
    bm, t, d = h.shape
    tm = min(POST2_TM, t)
    return pl.pallas_call(
        functools.partial(_post2_kernel, final_norm=final_norm),
        out_shape=jax.ShapeDtypeStruct((bm, t, d), F32),
        grid=(bm, t // tm),
        in_specs=[
            pl.BlockSpec((1, tm, d), lambda b, m: (b, m, 0)),
            pl.BlockSpec((1, tm, d), lambda b, m: (b, m, 0)),
            pl.BlockSpec((1, 1, d), lambda b, m: (b, 0, 0)),
            pl.BlockSpec((d, d), lambda b, m: (0, 0)),
            pl.BlockSpec((1, d), lambda b, m: (0, 0)),
        ],
        out_specs=pl.BlockSpec((1, tm, d), lambda b, m: (b, m, 0)),
        compiler_params=_cparams(("arbitrary", "arbitrary")),
    )(merged, h, gate, w_out, final_g)


def _pad_rows(w, rows):
    return jnp.pad(w, ((0, 0),) * (w.ndim - 2) + ((0, rows - w.shape[-2]), (0, 0)))


def kernel(x, c, ctx, c_ctx, ada_w, ada_b, norm_g, w_in, tok_mu, decay_w0, decay_lora_b, iclr_a0, iclr_lora_b, key_k, key_a, bonus_rk, gn_w, gn_b, vres_v0, vres_lora_a, vres_lora_b, pool_w, pool_scale, w_out_a, w_out_b, w_out, final_g):
    depth = w_in.shape[0]
    batch, seq, d = x.shape
    ctx_len = ctx.shape[1]
    assert seq % GRID_W == 0 and seq % CHUNK == 0 and ctx_len % CHUNK == 0
    assert d % LANES == 0

    off_merge = 3 * W_A + W_B + W_A + W_B
    off_dl = off_merge + 2 * d
    off_vres = off_dl + 2 * R_DECAY + 2 * R_ICLR
    nu = -(-(off_vres + LANES) // INPROJ_TN) * INPROJ_TN
    cols = dict(pool=REF_OFF_POOL // W_B, gate_a=REF_OFF_GATE_A // W_A, gate_b=REF_OFF_GATE_B // W_A,
                merge0=off_merge // d, merge1=off_merge // d + 1)
    assert off_merge % d == 0 and off_dl % GROUP_W == 0 and off_vres % LANES == 0

    head_mask = (jnp.arange(GROUP_W)[:, None] // HEAD_DIM == jnp.arange(GROUP_W)[None, :] // HEAD_DIM).astype(BF16)

    rows = -(-(batch + 1) // SUBLANES) * SUBLANES
    c_all = jnp.zeros((rows, d), F32).at[:batch].set(c).at[batch].set(c_ctx)
    mod = _ada_call(c_all, ada_w, ada_b)

    h_lat, h_ctx = x, ctx
    vfirst = {}
    tail = jnp.zeros((depth, d, nu - off_vres), F32)
    if depth > 1:
        tail = tail.at[1:, :, :R_VRES].set(vres_lora_a)
    w_perm = _wcast_call(w_in, tail, 2 * d)
    for l in range(depth):
        last = l == depth - 1

        dlw = decay_lora_b[l]
        dlw = jnp.stack([jnp.pad(dlw[0], ((0, LANES - R_DECAY), (0, 0))),
                         jnp.pad(dlw[1], ((R_DECAY, LANES - 2 * R_DECAY), (0, 0)))]).astype(BF16)
        ilw = iclr_lora_b[l]
        ilw = jnp.stack([jnp.pad(ilw[0], ((0, LANES - R_ICLR), (0, 0))),
                         jnp.pad(ilw[1], ((R_ICLR, LANES - 2 * R_ICLR), (0, 0)))]).astype(BF16)
        lp = dict(tok_mu=tok_mu[l], decay_w0=decay_w0[l], decay_lora=dlw, iclr_a0=iclr_a0[l], iclr_lora=ilw,
                  key_k=key_k[l].reshape(1, W_A), key_a=key_a[l], bonus_rk=bonus_rk[l].reshape(2, W_A),
                  head_mask=head_mask, gn_w=gn_w[l].reshape(1, W_A), gn_b=gn_b[l].reshape(1, W_A),
                  pool_w=pool_w[l].astype(BF16), pool_scale=pool_scale[l].reshape(1, W_B),
                  w_out_a=w_out_a[l].astype(BF16), w_out_b=w_out_b[l].astype(BF16))
        w_out_l = w_out[l].astype(BF16)
        g_l = norm_g[l].reshape(1, d)

        m = mod[l]
        shift_l, scale_l, gate_l = (m[:batch, j * d:(j + 1) * d].reshape(batch, 1, d) for j in range(3))
        shift_c, scale_c, gate_c = (m[batch:batch + 1, j * d:(j + 1) * d].reshape(1, 1, d) for j in range(3))

        u_c = _inproj_call(h_ctx.reshape(1, batch * ctx_len, d), shift_c, scale_c, g_l, w_perm, l)
        u_c = u_c.reshape(batch, ctx_len, nu)
        u_l = _inproj_call(h_lat, shift_l, scale_l, g_l, w_perm, l)

        vres_c = vres_l = None
        if l > 0:
            vb = _pad_rows(vres_lora_b[l - 1], LANES).astype(BF16)
            v0 = vres_v0[l - 1].reshape(1, W_A)
            vres_c = (vfirst['c'], off_vres // LANES, vb, v0)
            vres_l = (vfirst['l'], off_vres // LANES, vb, v0)
        keep_v = l == 0 and depth > 1
        prep_c = _prep_call(u_c, off_dl // GROUP_W, lp, False, not last, vres_c, keep_v)
        prep_l = _prep_call(u_l, off_dl // GROUP_W, lp, True, True, vres_l, keep_v)
        if keep_v:
            vfirst = dict(c=prep_c['vmix'], l=prep_l['vmix'])
        sc_c = _scan_call(prep_c, None)
        sc_l = _scan_call(prep_l, sc_c['sfin'])

        merged_l = _post1_call(sc_l, prep_l['bonus'], u_l, cols, lp)
        h_lat = _post2_call(merged_l, h_lat, gate_l, w_out_l, final_g.reshape(1, d), last)
        if not last:
            merged_c = _post1_call(sc_c, prep_c['bonus'], u_c, cols, lp)
            h_ctx = _post2_call(merged_c.reshape(1, batch * ctx_len, d), h_ctx.reshape(1, batch * ctx_len, d),
                                gate_c, w_out_l, final_g.reshape(1, d), False).reshape(batch, ctx_len, d)
    return h_lat
```

```python
import functools
import math

import jax
import jax.numpy as jnp
from jax import lax
from jax.experimental import pallas as pl
from jax.experimental.pallas import tpu as pltpu

F32 = jnp.float32
BF16 = jnp.bfloat16

GRID_W = 64
W_A = 1024
W_B = 1024
HEAD_DIM = 64
POOL_WINDOWS = (2, 4, 8, 16)
POOL_GROUP_W = W_B // len(POOL_WINDOWS)
R_DECAY = 64
R_ICLR = 64
R_VRES = 32
NORM_EPS = 1e-6
GN_EPS = HEAD_DIM * 1e-5
REF_OFF_POOL = 3 * W_A
REF_OFF_GATE_A = REF_OFF_POOL + W_B
REF_OFF_GATE_B = REF_OFF_GATE_A + W_A
REF_OFF_DECAY = REF_OFF_GATE_B + W_B
REF_OFF_ICLR = REF_OFF_DECAY + 2 * R_DECAY
REF_OFF_MERGE = REF_OFF_ICLR + 2 * R_ICLR

LANES = 128
SUBLANES = 8
MXU_DIM = 256
VMEM_LIMIT_BYTES = 56 * 1024 * 1024

GROUP_W = MXU_DIM
GROUP_HEADS = GROUP_W // HEAD_DIM
N_GROUPS = W_A // GROUP_W
SCAN_GROUPS = 4
SCAN_TB = 256
CHUNK = 64
SCAN_CHUNK_SKEW = 2
N_OPERANDS = 4
HALO = 64
POOL_HALO = 8
INPROJ_TM = 1024
INPROJ_TN = 1536
ADA_TN = 768
WCAST_ROWS = 256
POST1_TM = 256
POST2_TM = 512
INPROJ_NORM_ROWS = 16
LOG_DECAY_SCALE = math.exp(-0.5)


def _cparams(sem):
    return pltpu.CompilerParams(dimension_semantics=sem, vmem_limit_bytes=VMEM_LIMIT_BYTES)


def _sigmoid(x):
    return 1.0 / (1.0 + jnp.exp(-x))


def _dot(a, b):
    return jnp.dot(a, b, preferred_element_type=F32)


def _dot_nt(a, b):
    return lax.dot_general(a, b, (((1,), (1,)), ((), ())), preferred_element_type=F32)


def _dot_tn(a, b):
    return lax.dot_general(a, b, (((0,), (0,)), ((), ())), preferred_element_type=F32)


def _headsums(xs, bm):
    rows = xs[0].shape[0]
    s = _dot(jnp.concatenate([x.astype(BF16) for x in xs], axis=0), bm)
    return [s[n * rows:(n + 1) * rows] for n in range(len(xs))]


def _ada_kernel(c_ref, w_ref, b_ref, o_ref):
    c = c_ref[...]
    s = (c * _sigmoid(c)).astype(BF16)
    o_ref[0] = _dot(s, w_ref[0].astype(BF16)) + b_ref[0]


def _ada_call(c_all, ada_w, ada_b):
    n_layers, d, d3 = ada_w.shape
    rows = c_all.shape[0]
    tn = ADA_TN
    assert d3 % tn == 0
    return pl.pallas_call(
        _ada_kernel,
        out_shape=jax.ShapeDtypeStruct((n_layers, rows, d3), F32),
        grid=(n_layers, d3 // tn),
        in_specs=[
            pl.BlockSpec((rows, d), lambda l, n: (0, 0)),
            pl.BlockSpec((1, d, tn), lambda l, n: (l, 0, n)),
            pl.BlockSpec((1, 1, tn), lambda l, n: (l, 0, n)),
        ],
        out_specs=pl.BlockSpec((1, rows, tn), lambda l, n: (l, 0, n)),
        compiler_params=_cparams(("arbitrary", "arbitrary")),
    )(c_all, ada_w, ada_b.reshape(n_layers, 1, d3))


def _wcast_kernel(w_ref, t_ref, o_ref, *, merge_w):
    width = w_ref.shape[2]
    o_ref[0, :, :REF_OFF_DECAY] = w_ref[0, :, :REF_OFF_DECAY].astype(BF16)
    o_ref[0, :, REF_OFF_DECAY:REF_OFF_DECAY + merge_w] = w_ref[0, :, REF_OFF_MERGE:REF_OFF_MERGE + merge_w].astype(BF16)
    o_ref[0, :, REF_OFF_DECAY + merge_w:width] = w_ref[0, :, REF_OFF_DECAY:REF_OFF_MERGE].astype(BF16)
    o_ref[0, :, width:] = t_ref[0].astype(BF16)


def _wcast_call(w_in, tail, merge_w):
    n_layers, d, width = w_in.shape
    extra = tail.shape[2]
    tr = WCAST_ROWS
    assert REF_OFF_MERGE + merge_w == width and d % tr == 0
    assert all(x % LANES == 0 for x in (REF_OFF_DECAY, REF_OFF_MERGE, merge_w, width, extra))
    return pl.pallas_call(
        functools.partial(_wcast_kernel, merge_w=merge_w),
        out_shape=jax.ShapeDtypeStruct((n_layers, d, width + extra), BF16),
        grid=(n_layers, d // tr),
        in_specs=[pl.BlockSpec((1, tr, width), lambda l, j: (l, j, 0)),
                  pl.BlockSpec((1, tr, extra), lambda l, j: (l, j, 0))],
        out_specs=pl.BlockSpec((1, tr, width + extra), lambda l, j: (l, j, 0)),
        compiler_params=_cparams(("arbitrary", "arbitrary")),
    )(w_in, tail)


def _inproj_kernel(h_ref, shift_ref, scale_ref, g_ref, w_ref, o_ref, xn_ref):
    @pl.when(pl.program_id(2) == 0)
    def _():
        def rows(r, carry):
            sl = pl.ds(pl.multiple_of(r * INPROJ_NORM_ROWS, INPROJ_NORM_ROWS), INPROJ_NORM_ROWS)
            x = h_ref[0, sl, :]
            ms = jnp.mean(x * x, axis=-1, keepdims=True)
            y = x * lax.rsqrt(ms + NORM_EPS) * g_ref[...]
            xn_ref[sl, :] = (y * (1.0 + scale_ref[0]) + shift_ref[0]).astype(BF16)
            return carry

        lax.fori_loop(0, h_ref.shape[1] // INPROJ_NORM_ROWS, rows, 0, unroll=16)

    o_ref[0] = _dot(xn_ref[...], w_ref[0])


def _inproj_call(h, shift, scale, g, w, layer):
    bm, t, d = h.shape
    nu = w.shape[2]
    tm = min(INPROJ_TM, t)
    tn = INPROJ_TN
    assert nu % tn == 0
    return pl.pallas_call(
        _inproj_kernel,
        out_shape=jax.ShapeDtypeStruct((bm, t, nu), F32),
        grid=(bm, t // tm, nu // tn),
        in_specs=[
            pl.BlockSpec((1, tm, d), lambda b, m, n: (b, m, 0)),
            pl.BlockSpec((1, 1, d), lambda b, m, n: (b, 0, 0)),
            pl.BlockSpec((1, 1, d), lambda b, m, n: (b, 0, 0)),
            pl.BlockSpec((1, d), lambda b, m, n: (0, 0)),
            pl.BlockSpec((1, d, tn), lambda b, m, n: (layer, 0, n)),
        ],
        out_specs=pl.BlockSpec((1, tm, tn), lambda b, m, n: (b, m, n)),
        scratch_shapes=[pltpu.VMEM((tm, d), BF16)],
        compiler_params=_cparams(("arbitrary", "arbitrary", "arbitrary")),
    )(h, shift, scale, g, w)


def _shift_plan(grid_mode, hg):
    if grid_mode:
        return ((-1, 'first'), (1, 'last'), (-GRID_W, None), (GRID_W, None))[hg]
    return ((-1, None), (-1, None), (1, None), (1, None))[hg]


def _prep_kernel(*refs, names, tb, grid_mode, has_vres, emit_out, emit_v):
    R = dict(zip(names, refs))
    i = pl.program_id(1)
    nt = pl.num_programs(1)
    ps = jnp.where(i > 0, 1.0, 0.0).astype(F32)
    ns = jnp.where(i < nt - 1, 1.0, 0.0).astype(F32)
    n_ext = tb + 2 * HALO
    trow1 = lax.broadcasted_iota(jnp.int32, (tb, 1), 0)
    gcol = jnp.bitwise_and(trow1, GRID_W - 1)
    col_mask = dict(first=jnp.where(gcol != 0, 1.0, 0.0).astype(F32),
                    last=jnp.where(gcol != GRID_W - 1, 1.0, 0.0).astype(F32))
    bm = R['head_mask'][...]
    brow = lax.broadcasted_iota(jnp.int32, (tb, tb), 0)
    bcol = lax.broadcasted_iota(jnp.int32, (tb, tb), 1)
    same_chunk = (brow >> 6) == (bcol >> 6)
    tris = (jnp.where(jnp.logical_and(same_chunk, bcol <= brow), 1.0, 0.0).astype(BF16),
            jnp.where(jnp.logical_and(same_chunk, bcol >= brow), 1.0, 0.0).astype(BF16))
    n_chunk = tb // CHUNK
    R['ptot'][...] = jnp.zeros_like(R['ptot'])

    def mixed(hg, m):
        off, cmask = _shift_plan(grid_mode, hg)
        cs = slice(m * W_A + hg * GROUP_W, m * W_A + (hg + 1) * GROUP_W)
        x = R['cur'][0, :, cs]
        if off == -GRID_W:
            sh = jnp.concatenate([R['prv'][0, :, cs] * ps, x[:tb - HALO]], axis=0)
        elif off == GRID_W:
            sh = jnp.concatenate([x[HALO:], R['nxt'][0, :, cs] * ns], axis=0)
        else:
            ext = jnp.concatenate([R['prv'][0, :, cs] * ps, x, R['nxt'][0, :, cs] * ns], axis=0)
            sh = pltpu.roll(ext, 1 if off == -1 else n_ext - 1, axis=0)[HALO:HALO + tb]
        if cmask is not None:
            sh = sh * col_mask[cmask]
        return x + R['tok_mu'][m:m + 1, hg * GROUP_W:(hg + 1) * GROUP_W] * (sh - x)

    dl = R['dl'][0]
    dlo = jnp.tanh(dl[:, :LANES]).astype(BF16)
    alo = dl[:, LANES:].astype(BF16)
    for hg in range(N_GROUPS):
        cs = slice(hg * GROUP_W, (hg + 1) * GROUP_W)
        r = mixed(hg, 0)
        k = mixed(hg, 1)
        v = mixed(hg, 2)
        if has_vres:
            gate = _sigmoid(R['vres_v0'][:, cs] + _dot(R['lo'][0].astype(BF16), R['vres_b'][:, cs]))
            v = v + (R['vf'][0, :, cs] - v) * gate
        if emit_v:
            R['vmix'][0, :, cs] = v
        kk = k * R['key_k'][:, cs]
        a_d, kd_d, sums_in = [], [], [kk * kk]
        for d in (0, 1):
            a = _sigmoid(R['iclr_a0'][d:d + 1, cs] + _dot(alo, R['iclr_lora'][d, :, cs]))
            ka = R['key_a'][d:d + 1, cs]
            kd = k * ((1.0 - ka) + a * ka)
            a_d.append(a)
            kd_d.append(kd)
            if emit_out:
                sums_in.append(r * kd * R['bonus_rk'][d:d + 1, cs])
        sums = _headsums(sums_in, bm)
        kk = kk / jnp.maximum(jnp.sqrt(sums[0]), 1e-12)
        if emit_out:
            R['bonus'][0, :, cs] = (sums[1] + sums[2]) * v
        R['vop'][0, :, cs] = v.astype(BF16)
        for d in (0, 1):
            z = R['decay_w0'][d:d + 1, cs] + _dot(dlo, R['decay_lora'][d, :, cs])
            lg = -LOG_DECAY_SCALE * _sigmoid(z)
            g1 = lg.astype(BF16)
            g2 = (lg - g1.astype(F32)).astype(BF16)
            cum = _dot(tris[d], g1) + _dot(tris[d], g2)
            bb = kk * a_d[d]
            kd = kd_d[d]
            for c in range(n_chunk):
                sl = slice(c * CHUNK, (c + 1) * CHUNK)
                lc = cum[sl]
                ltot = lc[CHUNK - 1:CHUNK] if d == 0 else lc[0:1]
                e_pos = jnp.exp(lc)
                e_neg = 1.0 / e_pos
                R['ptot'][d, 0, 0, c:c + 1, cs] = jnp.exp(ltot)
                operands = (r[sl] * e_pos, kk[sl] * jnp.exp(lc - lg[sl]), kd[sl] * e_neg, bb[sl] * e_neg)
                for m in range(N_OPERANDS):
                    R['ops'][d, m, 0, sl, cs] = operands[m].astype(BF16)


def _prep_call(u, dl_col, lp, grid_mode, emit_out, vres=None, emit_v=False):
    b, t, _ = u.shape
    tb = min(SCAN_TB, t)
    nt = t // tb
    hb = tb // HALO
    nh = t // HALO
    w3 = 3 * W_A
    has_vres = vres is not None
    assert GROUP_W == W_A // 4 and tb % HALO == 0 and tb % CHUNK == 0 and tb // CHUNK <= SUBLANES
    names, in_specs, args = [], [], []

    def add(name, spec, arr):
        names.append(name)
        in_specs.append(spec)
        args.append(arr)

    add('cur', pl.BlockSpec((1, tb, w3), lambda bi, i: (bi, i, 0)), u)
    add('prv', pl.BlockSpec((1, HALO, w3), lambda bi, i: (bi, jnp.maximum(i * hb - 1, 0), 0)), u)
    add('nxt', pl.BlockSpec((1, HALO, w3), lambda bi, i: (bi, jnp.minimum((i + 1) * hb, nh - 1), 0)), u)
    add('dl', pl.BlockSpec((1, tb, GROUP_W), lambda bi, i: (bi, i, dl_col)), u)
    if has_vres:
        v_first, vres_col, vres_b, vres_v0 = vres
        add('vf', pl.BlockSpec((1, tb, W_A), lambda bi, i: (bi, i, 0)), v_first)
        add('lo', pl.BlockSpec((1, tb, LANES), lambda bi, i: (bi, i, vres_col)), u)
        add('vres_b', pl.BlockSpec((LANES, W_A), lambda bi, i: (0, 0)), vres_b)
        add('vres_v0', pl.BlockSpec((1, W_A), lambda bi, i: (0, 0)), vres_v0)
    vec = lambda rows: pl.BlockSpec((rows, W_A), lambda bi, i: (0, 0))
    lora = pl.BlockSpec((2, LANES, W_A), lambda bi, i: (0, 0, 0))
    add('tok_mu', vec(3), lp['tok_mu'])
    add('decay_w0', vec(2), lp['decay_w0'])
    add('decay_lora', lora, lp['decay_lora'])
    add('iclr_a0', vec(2), lp['iclr_a0'])
    add('iclr_lora', lora, lp['iclr_lora'])
    add('key_k', vec(1), lp['key_k'])
    add('key_a', vec(2), lp['key_a'])
    add('bonus_rk', vec(2), lp['bonus_rk'])
    add('head_mask', pl.BlockSpec((GROUP_W, GROUP_W), lambda bi, i: (0, 0)), lp['head_mask'])

    out_names = ['ops', 'ptot', 'vop']
    out_shape = [jax.ShapeDtypeStruct((2, N_OPERANDS, b, t, W_A), BF16),
                 jax.ShapeDtypeStruct((2, b, nt, SUBLANES, W_A), F32),
                 jax.ShapeDtypeStruct((b, t, W_A), BF16)]
    out_specs = [pl.BlockSpec((2, N_OPERANDS, 1, tb, W_A), lambda bi, i: (0, 0, bi, i, 0)),
                 pl.BlockSpec((2, 1, 1, SUBLANES, W_A), lambda bi, i: (0, bi, i, 0, 0)),
                 pl.BlockSpec((1, tb, W_A), lambda bi, i: (bi, i, 0))]
    for flag, nm in ((emit_out, 'bonus'), (emit_v, 'vmix')):
        if flag:
            out_names.append(nm)
            out_shape.append(jax.ShapeDtypeStruct((b, t, W_A), F32))
            out_specs.append(pl.BlockSpec((1, tb, W_A), lambda bi, i: (bi, i, 0)))

    outs = pl.pallas_call(
        functools.partial(_prep_kernel, names=tuple(names + out_names), tb=tb, grid_mode=grid_mode,
                          has_vres=has_vres, emit_out=emit_out, emit_v=emit_v),
        out_shape=out_shape,
        grid=(b, nt),
        in_specs=in_specs,
        out_specs=out_specs,
        compiler_params=_cparams(("arbitrary", "arbitrary")),
    )(*args)
    return dict(zip(out_names, outs))


def _scan_kernel(*refs, names, tb, n_groups, use_s0):
    R = dict(zip(names, refs))
    s_ref = R['s_scr']
    i = pl.program_id(2)
    nt = pl.num_programs(2)

    @pl.when(i == 0)
    def _():
        if use_s0:
            s_ref[...] = R['s0'][:, 0]
        else:
            s_ref[...] = jnp.zeros_like(s_ref)

    lane1 = lax.broadcasted_iota(jnp.int32, (1, LANES), 1)
    half_f = (jnp.where(lane1 < HEAD_DIM, 1.0, 0.0).astype(F32), jnp.where(lane1 >= HEAD_DIM, 1.0, 0.0).astype(F32))
    half_b = (half_f[0].astype(BF16), half_f[1].astype(BF16))
    zero_tile = jnp.zeros((HEAD_DIM, LANES), BF16)
    n_tiles = GROUP_W // LANES

    def bd(x):
        xb = x.astype(BF16)
        rows = []
        for j in range(GROUP_HEADS):
            t = (j * HEAD_DIM) // LANES
            piece = xb[:, t * LANES:(t + 1) * LANES] * half_b[(j * HEAD_DIM % LANES) // HEAD_DIM]
            rows.append(jnp.concatenate([piece if tt == t else zero_tile for tt in range(n_tiles)], axis=1))
        return jnp.concatenate(rows, axis=0)

    def head_blocks(full):
        tiles = []
        for t in range(n_tiles):
            acc = None
            for j in range(GROUP_HEADS):
                if (j * HEAD_DIM) // LANES == t:
                    part = (full[j * HEAD_DIM:(j + 1) * HEAD_DIM, t * LANES:(t + 1) * LANES]
                            * half_f[(j * HEAD_DIM % LANES) // HEAD_DIM])
                    acc = part if acc is None else acc + part
            tiles.append(acc)
        return jnp.concatenate(tiles, axis=1)

    def prod(a, b):
        return _dot(a.astype(BF16), bd(b))

    trow = lax.broadcasted_iota(jnp.int32, (CHUNK, GROUP_W), 0)
    scol = jnp.bitwise_and(lax.broadcasted_iota(jnp.int32, (CHUNK, GROUP_W), 1), CHUNK - 1)
    eye = jnp.where(trow == scol, 1.0, 0.0).astype(F32)
    same16 = (trow >> 4) == (scol >> 4)
    same32 = (trow >> 5) == (scol >> 5)
    tri_masks = []
    for before, upto in ((scol < trow, scol <= trow), (scol > trow, scol >= trow)):
        tri_masks.append(dict(
            neg_d16=jnp.where(jnp.logical_and(before, same16), -1.0, 0.0).astype(F32),
            o32=jnp.where(jnp.logical_and(before, jnp.logical_and(same32, jnp.logical_not(same16))), 1.0, 0.0).astype(BF16),
            o64=jnp.where(jnp.logical_and(before, jnp.logical_not(same32)), 1.0, 0.0).astype(BF16),
            strict=jnp.where(before, 1.0, 0.0).astype(F32),
            incl=jnp.where(upto, 1.0, 0.0).astype(F32)))

    n_chunk = tb // CHUNK
    s_cur = {(d, g): s_ref[d, g] for d in (0, 1) for g in range(n_groups)}

    def unit_stages(d, g, c):
        sl = slice(c * CHUNK, (c + 1) * CHUNK)
        cs = slice(g * GROUP_W, (g + 1) * GROUP_W)
        r_h, kap_h, k_h, b_h = (R[f'ops{d}'][0, m, 0, sl, cs] for m in range(N_OPERANDS))
        vc = R[f'v{d}'][0, sl, cs]
        p_tot = R[f'ptot{d}'][0, 0, 0, c:c + 1, cs]
        p_tot_b = p_tot.astype(BF16)
        k_t = k_h * p_tot_b
        b_t = b_h * p_tot_b
        mk = tri_masks[d]
        x2 = jnp.concatenate([kap_h, r_h], axis=0)
        gb = _dot_nt(x2, bd(b_h))
        yield
        gk = _dot_nt(x2, bd(k_h))
        yield
        ab = gb[:CHUNK]
        abb = ab.astype(BF16)
        n1 = ab * mk['neg_d16']
        o32 = abb * mk['o32']
        o64 = abb * mk['o64']
        a_rb = (gb[CHUNK:] * mk['incl']).astype(BF16)
        a_kk = jnp.concatenate([gk[:CHUNK] * mk['strict'], gk[CHUNK:] * mk['incl']], axis=0).astype(BF16)
        m1y = _dot(a_kk, bd(vc))
        yield
        n1b = n1.astype(BF16)
        nk = prod(n1b, n1b).astype(BF16)
        x = eye + n1
        yield
        for _ in range(2):
            both = _dot(jnp.concatenate([nk, x.astype(BF16)], axis=0), bd(nk))
            nk = both[:CHUNK].astype(BF16)
            x = x + both[CHUNK:]
            yield
        x = x + prod(x, nk)
        yield
        for off in (o32, o64):
            xb = x.astype(BF16)
            xo = prod(xb, off).astype(BF16)
            yield
            x = x - prod(xo, xb)
            yield
        t16 = x.astype(BF16)
        w = _dot(t16, bd(kap_h)).astype(BF16)
        yield
        ut = _dot(t16, bd(m1y[:CHUNK]))
        yield
        s_in = s_cur[d, g]
        wr = _dot_nt(jnp.concatenate([w, r_h], axis=0), bd(s_in))
        yield
        u = (-(wr[:CHUNK] + ut)).astype(BF16)
        s_cur[d, g] = s_in * p_tot + head_blocks(_dot_tn(jnp.concatenate([u, vc], axis=0),
                                                         jnp.concatenate([b_t, k_t], axis=0)))
        yield
        R[f'y{d}'][0, sl, cs] = wr[CHUNK:] + prod(a_rb, u) + m1y[CHUNK:]

    def gate_stages():
        for j in (0, 1):
            for n in range(R[f'mg{j}'].shape[2] // GROUP_W):
                cols_n = slice(n * GROUP_W, (n + 1) * GROUP_W)
                R[f'sg{j}'][0, :, cols_n] = _sigmoid(R[f'mg{j}'][0, :, cols_n]).astype(BF16)
                yield

    jobs = [(pos * SCAN_CHUNK_SKEW, unit_stages(d, g, c)) for d in (0, 1) for g in range(n_groups)
            for pos, c in enumerate(range(n_chunk) if d == 0 else range(n_chunk - 1, -1, -1))]
    jobs.append((0, gate_stages()))
    tick = 0
    while jobs:
        for entry in list(jobs):
            if tick >= entry[0]:
                try:
                    next(entry[1])
                except StopIteration:
                    jobs.remove(entry)
        tick += 1
    for (d, g), s_val in s_cur.items():
        s_ref[d, g] = s_val

    @pl.when(i == nt - 1)
    def _():
        R['sfin'][:, 0] = s_ref[...]


def _scan_call(prep, s0, u, cols, d_model):
    ops, ptot = prep['ops'], prep['ptot']
    _, _, b, t, _ = ops.shape
    tb = min(SCAN_TB, t)
    nt = t // tb
    ng = SCAN_GROUPS
    gw = ng * GROUP_W
    npair = N_GROUPS // ng
    use_s0 = s0 is not None
    tblk = (lambda i: i, lambda i: nt - 1 - i)
    names, in_specs, args = [], [], []

    def add(name, spec, arr):
        names.append(name)
        in_specs.append(spec)
        args.append(arr)

    for d in (0, 1):
        add(f'ops{d}', pl.BlockSpec((1, N_OPERANDS, 1, tb, gw), lambda bi, p, i, d=d: (d, 0, bi, tblk[d](i), p)), ops)
        add(f'ptot{d}', pl.BlockSpec((1, 1, 1, SUBLANES, gw), lambda bi, p, i, d=d: (d, bi, tblk[d](i), 0, p)), ptot)
        add(f'v{d}', pl.BlockSpec((1, tb, gw), lambda bi, p, i, d=d: (bi, tblk[d](i), p)), prep['vop'])
    state_spec = pl.BlockSpec((2, 1, ng, HEAD_DIM, GROUP_W), lambda bi, p, i: (0, bi, p, 0, 0))
    if use_s0:
        add('s0', state_spec, s0)
    assert npair == 1
    for j in (0, 1):
        add(f'mg{j}', pl.BlockSpec((1, tb, d_model), lambda bi, p, i, j=j: (bi, i, cols[f'merge{j}'])), u)

    out_names, out_shape, out_specs = [], [], []
    for j in (0, 1):
        out_names.append(f'sg{j}')
        out_shape.append(jax.ShapeDtypeStruct((b, t, d_model), BF16))
        out_specs.append(pl.BlockSpec((1, tb, d_model), lambda bi, p, i: (bi, i, 0)))
    for d in (0, 1):
        out_names.append(f'y{d}')
        out_shape.append(jax.ShapeDtypeStruct((b, t, W_A), F32))
        out_specs.append(pl.BlockSpec((1, tb, gw), lambda bi, p, i, d=d: (bi, tblk[d](i), p)))
    out_names.append('sfin')
    out_shape.append(jax.ShapeDtypeStruct((2, b, N_GROUPS, HEAD_DIM, GROUP_W), F32))
    out_specs.append(state_spec)

    outs = pl.pallas_call(
        functools.partial(_scan_kernel, names=tuple(names + out_names + ['s_scr']), tb=tb, n_groups=ng,
                          use_s0=use_s0),
        out_shape=out_shape,
        grid=(b, npair, nt),
        in_specs=in_specs,
        out_specs=out_specs,
        scratch_shapes=[pltpu.VMEM((2, ng, HEAD_DIM, GROUP_W), F32)],
        compiler_params=_cparams(("arbitrary", "arbitrary", "arbitrary")),
    )(*args)
    return dict(zip(out_names, outs))


def _post1_kernel(y0, y1, bo, ga, gb, pc, pp, pn, mg0, mg1, gnw, gnb, pw, ps, woa, wob, bm_ref, o_ref, *, tm, t_total):
    i = pl.program_id(1)
    nt = pl.num_programs(1)
    bm = bm_ref[...]

    parts = []
    for hg in range(N_GROUPS):
        cs = slice(hg * GROUP_W, (hg + 1) * GROUP_W)
        y = y0[0, :, cs] + y1[0, :, cs]
        mu = _headsums([y], bm)[0] * (1.0 / HEAD_DIM)
        yc = y - mu
        var = _headsums([yc * yc], bm)[0] * (1.0 / HEAD_DIM)
        yn = yc * lax.rsqrt(var + GN_EPS) * gnw[:, cs] + gnb[:, cs]
        yn = yn + bo[0, :, cs]
        gate = ga[0, :, cs]
        parts.append((yn * (gate * _sigmoid(gate))).astype(BF16))
    y_a = _dot(jnp.concatenate(parts, axis=1), woa[...])

    pscale = jnp.where(i > 0, 1.0, 0.0).astype(F32)
    nscale = jnp.where(i < nt - 1, 1.0, 0.0).astype(F32)
    n_ext = tm + 2 * POOL_HALO
    tpos = (lax.broadcasted_iota(jnp.int32, (tm, 1), 0) + i * tm).astype(F32)
    parts = []
    for gi, win in enumerate(POOL_WINDOWS):
        left = win // 2
        right = win - 1 - left
        cs = slice(gi * POOL_GROUP_W, (gi + 1) * POOL_GROUP_W)
        p = pc[0, :, cs]
        ext = jnp.concatenate([pp[0, :, cs] * pscale, p, pn[0, :, cs] * nscale], axis=0)
        s = ext
        step = 1
        while step < win:
            s = s + pltpu.roll(s, step, axis=0)
            step *= 2
        if right > 0:
            s = pltpu.roll(s, n_ext - right, axis=0)
        s = s[POOL_HALO:POOL_HALO + tm]
        cnt = jnp.minimum(tpos + (right + 1), float(t_total)) - jnp.maximum(tpos - left, 0.0)
        pooled = s / cnt - p
        mixed = _dot(pooled.astype(BF16), pw[gi]) * ps[:, cs]
        gate = gb[0, :, cs]
        parts.append((mixed * (gate * _sigmoid(gate))).astype(BF16))
    y_b = _dot(jnp.concatenate(parts, axis=1), wob[...])

    merged = mg0[0].astype(F32) * y_a + mg1[0].astype(F32) * y_b
    o_ref[0] = merged.astype(BF16)


def _post1_call(sc, bonus, u, cols, lp):
    y0, y1 = sc['y0'], sc['y1']
    b, t, _ = y0.shape
    d = lp['w_out_a'].shape[1]
    tm = min(POST1_TM, t)
    nt = t // tm
    hb = tm // POOL_HALO
    nh = t // POOL_HALO
    blk = lambda c: pl.BlockSpec((1, tm, W_A), lambda bi, i: (bi, i, c))
    wide = lambda c: pl.BlockSpec((1, tm, d), lambda bi, i: (bi, i, c))
    const2 = lambda shape: pl.BlockSpec(shape, lambda bi, i: (0, 0))
    pool_col = cols['pool']
    in_specs = [blk(0), blk(0), blk(0),
                blk(cols['gate_a']), blk(cols['gate_b']), blk(pool_col),
                pl.BlockSpec((1, POOL_HALO, W_B), lambda bi, i: (bi, jnp.maximum(i * hb - 1, 0), pool_col)),
                pl.BlockSpec((1, POOL_HALO, W_B), lambda bi, i: (bi, jnp.minimum((i + 1) * hb, nh - 1), pool_col)),
                wide(0), wide(0),
                const2((1, W_A)), const2((1, W_A)),
                pl.BlockSpec((len(POOL_WINDOWS), POOL_GROUP_W, POOL_GROUP_W), lambda bi, i: (0, 0, 0)),
                const2((1, W_B)), const2((W_A, d)), const2((W_B, d)), const2((GROUP_W, GROUP_W))]
    args = [y0, y1, bonus, u, u, u, u, u, sc['sg0'], sc['sg1'],
            lp['gn_w'], lp['gn_b'], lp['pool_w'], lp['pool_scale'], lp['w_out_a'], lp['w_out_b'], lp['head_mask']]
    return pl.pallas_call(
        functools.partial(_post1_kernel, tm=tm, t_total=t),
        out_shape=jax.ShapeDtypeStruct((b, t, d), BF16),
        grid=(b, nt),
        in_specs=in_specs,
        out_specs=pl.BlockSpec((1, tm, d), lambda bi, i: (bi, i, 0)),
        compiler_params=_cparams(("arbitrary", "arbitrary")),
    )(*args)


def _post2_kernel(m_ref, h_ref, gate_ref, w_ref, fg_ref, o_ref, *, final_norm):
    out = _dot(m_ref[0], w_ref[...])
    h = h_ref[0] + gate_ref[0] * out
    if final_norm:
        ms = jnp.mean(h * h, axis=-1, keepdims=True)
        h = h * lax.rsqrt(ms + NORM_EPS) * fg_ref[...]
    o_ref[0] = h


def _post2_call(merged, h, gate, w_out, final_g, final_norm):
    bm, t, d = h.shape
    tm = min(POST2_TM, t)
    return pl.pallas_call(
        functools.partial(_post2_kernel, final_norm=final_norm),
        out_shape=jax.ShapeDtypeStruct((bm, t, d), F32),
        grid=(bm, t // tm),
        in_specs=[
            pl.BlockSpec((1, tm, d), lambda b, m: (b, m, 0)),
            pl.BlockSpec((1, tm, d), lambda b, m: (b, m, 0)),
            pl.BlockSpec((1, 1, d), lambda b, m: (b, 0, 0)),
            pl.BlockSpec((d, d), lambda b, m: (0, 0)),
            pl.BlockSpec((1, d), lambda b, m: (0, 0)),
        ],
        out_specs=pl.BlockSpec((1, tm, d), lambda b, m: (b, m, 0)),
        compiler_params=_cparams(("arbitrary", "arbitrary")),
    )(merged, h, gate, w_out, final_g)


def _pad_rows(w, rows):
    return jnp.pad(w, ((0, 0),) * (w.ndim - 2) + ((0, rows - w.shape[-2]), (0, 0)))


def kernel(x, c, ctx, c_ctx, ada_w, ada_b, norm_g, w_in, tok_mu, decay_w0, decay_lora_b, iclr_a0, iclr_lora_b, key_k, key_a, bonus_rk, gn_w, gn_b, vres_v0, vres_lora_a, vres_lora_b, pool_w, pool_scale, w_out_a, w_out_b, w_out, final_g):
    depth = w_in.shape[0]
    batch, seq, d = x.shape
    ctx_len = ctx.shape[1]
    assert seq % GRID_W == 0 and seq % CHUNK == 0 and ctx_len % CHUNK == 0
    assert d % LANES == 0

    off_merge = 3 * W_A + W_B + W_A + W_B
    off_dl = off_merge + 2 * d
    off_vres = off_dl + 2 * R_DECAY + 2 * R_ICLR
    nu = -(-(off_vres + LANES) // INPROJ_TN) * INPROJ_TN
    cols = dict(pool=REF_OFF_POOL // W_B, gate_a=REF_OFF_GATE_A // W_A, gate_b=REF_OFF_GATE_B // W_A,
                merge0=off_merge // d, merge1=off_merge // d + 1)
    assert off_merge % d == 0 and off_dl % GROUP_W == 0 and off_vres % LANES == 0

    head_mask = (jnp.arange(GROUP_W)[:, None] // HEAD_DIM == jnp.arange(GROUP_W)[None, :] // HEAD_DIM).astype(BF16)

    rows = -(-(batch + 1) // SUBLANES) * SUBLANES
    c_all = jnp.zeros((rows, d), F32).at[:batch].set(c).at[batch].set(c_ctx)
    mod = _ada_call(c_all, ada_w, ada_b)

    h_lat, h_ctx = x, ctx
    vfirst = {}
    tail = jnp.zeros((depth, d, nu - off_vres), F32)
    if depth > 1:
        tail = tail.at[1:, :, :R_VRES].set(vres_lora_a)
    w_perm = _wcast_call(w_in, tail, 2 * d)
    for l in range(depth):
        last = l == depth - 1

        dlw = decay_lora_b[l]
        dlw = jnp.stack([jnp.pad(dlw[0], ((0, LANES - R_DECAY), (0, 0))),
                         jnp.pad(dlw[1], ((R_DECAY, LANES - 2 * R_DECAY), (0, 0)))]).astype(BF16)
        ilw = iclr_lora_b[l]
        ilw = jnp.stack([jnp.pad(ilw[0], ((0, LANES - R_ICLR), (0, 0))),
                         jnp.pad(ilw[1], ((R_ICLR, LANES - 2 * R_ICLR), (0, 0)))]).astype(BF16)
        lp = dict(tok_mu=tok_mu[l], decay_w0=decay_w0[l], decay_lora=dlw, iclr_a0=iclr_a0[l], iclr_lora=ilw,
                  key_k=key_k[l].reshape(1, W_A), key_a=key_a[l], bonus_rk=bonus_rk[l].reshape(2, W_A),
                  head_mask=head_mask, gn_w=gn_w[l].reshape(1, W_A), gn_b=gn_b[l].reshape(1, W_A),
                  pool_w=pool_w[l].astype(BF16), pool_scale=pool_scale[l].reshape(1, W_B),
                  w_out_a=w_out_a[l].astype(BF16), w_out_b=w_out_b[l].astype(BF16))
        w_out_l = w_out[l].astype(BF16)
        g_l = norm_g[l].reshape(1, d)

        m = mod[l]
        shift_l, scale_l, gate_l = (m[:batch, j * d:(j + 1) * d].reshape(batch, 1, d) for j in range(3))
        shift_c, scale_c, gate_c = (m[batch:batch + 1, j * d:(j + 1) * d].reshape(1, 1, d) for j in range(3))

        u_c = _inproj_call(h_ctx.reshape(1, batch * ctx_len, d), shift_c, scale_c, g_l, w_perm, l)
        u_c = u_c.reshape(batch, ctx_len, nu)
        u_l = _inproj_call(h_lat, shift_l, scale_l, g_l, w_perm, l)

        vres_c = vres_l = None
        if l > 0:
            vb = _pad_rows(vres_lora_b[l - 1], LANES).astype(BF16)
            v0 = vres_v0[l - 1].reshape(1, W_A)
            vres_c = (vfirst['c'], off_vres // LANES, vb, v0)
            vres_l = (vfirst['l'], off_vres // LANES, vb, v0)
        keep_v = l == 0 and depth > 1
        prep_c = _prep_call(u_c, off_dl // GROUP_W, lp, False, not last, vres_c, keep_v)
        prep_l = _prep_call(u_l, off_dl // GROUP_W, lp, True, True, vres_l, keep_v)
        if keep_v:
            vfirst = dict(c=prep_c['vmix'], l=prep_l['vmix'])
        sc_c = _scan_call(prep_c, None, u_c, cols, d)
        sc_l = _scan_call(prep_l, sc_c['sfin'], u_l, cols, d)

        merged_l = _post1_call(sc_l, prep_l['bonus'], u_l, cols, lp)
        h_lat = _post2_call(merged_l, h_lat, gate_l, w_out_l, final_g.reshape(1, d), last)
        if not last:
            merged_c = _post1_call(sc_c, prep_c['bonus'], u_c, cols, lp)
            h_ctx = _post2_call(merged_c.reshape(1, batch * ctx_len, d), h_ctx.reshape(1, batch * ctx_len, d),
                                gate_c, w_out_l, final_g.reshape(1, d), False).reshape(batch, ctx_len, d)
    return h_lat
```
